```python
import math
import jax
import jax.numpy as jnp
from jax import lax
import numpy as np

D_MODEL = 4096
BATCH = 4
SEQ = 4096
DEPTH = 1

GDN_HEAD_DIM = 128
GDN_HEADS = D_MODEL // 256
GDN_WIDTH = GDN_HEADS * GDN_HEAD_DIM
GDN_CHUNK = 64
CONV_WIDTH = 4
RWKV_HEAD_DIM = 64
RWKV_HEADS = D_MODEL // 128
RWKV_WIDTH = RWKV_HEADS * RWKV_HEAD_DIM
DECAY_LORA = 96
AAA_LORA = 96
GATE_LORA = 256
RWKV_GN_EPS = 64e-5
N_GROUPS = 8
EXPERTS_PER_GROUP = 8
N_EXPERTS = N_GROUPS * EXPERTS_PER_GROUP
TOP_K = 2
D_EXPERT = 512
MOE_BLOCK = 128
RMS_EPS = 1e-6
N_MOD = 6

GDN_CONV_COLS = 3 * GDN_WIDTH
OFF_GDN_Z = GDN_CONV_COLS
OFF_GDN_A = OFF_GDN_Z + GDN_WIDTH
OFF_GDN_B = OFF_GDN_A + GDN_HEADS
OFF_RWKV = OFF_GDN_B + GDN_HEADS
RWKV_SHIFT_COLS = 3 * RWKV_WIDTH + DECAY_LORA + AAA_LORA + GATE_LORA
OFF_GATES = OFF_RWKV + RWKV_SHIFT_COLS
IN_COLS = OFF_GATES + 2 * D_MODEL

kernel_name = 'hybrid_gdn_rwkv7_hmoe_block'


def rmsnorm(x, gain, eps=RMS_EPS):
    xf = x.astype(jnp.float32)
    y = xf * lax.rsqrt(jnp.mean(xf * xf, axis=-1, keepdims=True) + eps)
    return (y * gain.astype(jnp.float32)).astype(x.dtype)


def l2norm(x, eps=1e-6):
    xf = x.astype(jnp.float32)
    return xf * lax.rsqrt(jnp.sum(xf * xf, axis=-1, keepdims=True) + eps)


def causal_depthwise_conv(x, w):
    C = x.shape[-1]
    return lax.conv_general_dilated(
        x, w[:, None, :].astype(x.dtype), window_strides=(1,), padding=[(CONV_WIDTH - 1, 0)],
        dimension_numbers=('NWC', 'WIO', 'NWC'), feature_group_count=C)


def token_shift(x):
    return jnp.pad(x, ((0, 0), (1, 0), (0, 0)))[:, :-1]


def gated_delta_rule_chunked(q, k, v, g, beta):
    B, T, H, Dk = q.shape
    Dv = v.shape[-1]
    C = GDN_CHUNK
    n = T // C

    def blocks(t):
        return jnp.moveaxis(t.reshape((B, n, C, H) + t.shape[3:]), 3, 1)

    q, k, v, g, beta = blocks(q), blocks(k), blocks(v), blocks(g), blocks(beta)
    gc = jnp.cumsum(g, axis=-1)
    idx = jnp.arange(C)
    causal = idx[:, None] >= idx[None, :]
    strict = idx[:, None] > idx[None, :]
    decay = jnp.exp(jnp.where(causal, gc[..., :, None] - gc[..., None, :], -jnp.inf))
    kb = k * beta[..., None]
    L = jnp.where(strict, jnp.einsum('bhnid,bhnjd->bhnij', kb, k) * decay, 0.0)
    a_mat = jnp.eye(C, dtype=L.dtype) + L
    u = lax.linalg.triangular_solve(a_mat, v * beta[..., None], left_side=True, lower=True, unit_diagonal=True)
    w = lax.linalg.triangular_solve(a_mat, kb * jnp.exp(gc)[..., None], left_side=True, lower=True, unit_diagonal=True)
    attn = jnp.einsum('bhnid,bhnjd->bhnij', q, k) * decay
    g_last = gc[..., -1]
    q_dec = q * jnp.exp(gc)[..., None]
    k_dec = k * jnp.exp(g_last[..., None] - gc)[..., None]

    def step(S, inp):
        q_i, k_i, u_i, w_i, a_i, gl_i = inp
        v_new = u_i - jnp.einsum('bhcd,bhde->bhce', w_i, S)
        o = jnp.einsum('bhcd,bhde->bhce', q_i, S) + jnp.einsum('bhij,bhje->bhie', a_i, v_new)
        S = S * jnp.exp(gl_i)[..., None, None] + jnp.einsum('bhcd,bhce->bhde', k_i, v_new)
        return S, o

    xs = tuple(jnp.moveaxis(t, 2, 0) for t in (q_dec, k_dec, u, w, attn, g_last))
    _, o = lax.scan(step, jnp.zeros((B, H, Dk, Dv), jnp.float32), xs)
    return jnp.moveaxis(o, 0, 2).reshape(B, H, T, Dv).transpose(0, 2, 1, 3)


def rwkv7_recurrence(r, w, k, v, a, b):
    B, T, H, N = r.shape

    def step(S, inp):
        r_t, w_t, k_t, v_t, a_t, b_t = inp
        sa = jnp.einsum('bhvk,bhk->bhv', S, a_t)
        S = S * w_t[:, :, None, :] + sa[..., None] * b_t[:, :, None, :] + v_t[..., None] * k_t[:, :, None, :]
        return S, jnp.einsum('bhvk,bhk->bhv', S, r_t)

    xs = tuple(jnp.moveaxis(t, 1, 0) for t in (r, w, k, v, a, b))
    _, y = lax.scan(step, jnp.zeros((B, H, N, N), jnp.float32), xs)
    return jnp.moveaxis(y, 0, 1)


def hybrid_mixer(h, w_in, conv_w, gdn_a_log, gdn_dt_bias, gdn_onorm_g, rwkv_mu, rwkv_w0, rwkv_w_up,
                 rwkv_a0, rwkv_a_up, rwkv_g_up, rwkv_k_k, rwkv_k_a, rwkv_r_k, rwkv_ln_w, rwkv_ln_b,
                 w_gdn_o, w_rwkv_o, w_out):
    B, T, _ = h.shape
    f32 = jnp.float32
    z = h @ w_in

    qkv = jax.nn.silu(causal_depthwise_conv(z[..., :GDN_CONV_COLS], conv_w))
    q, k, v = jnp.split(qkv, 3, axis=-1)
    hd = (B, T, GDN_HEADS, GDN_HEAD_DIM)
    q = l2norm(q.reshape(hd)) * (GDN_HEAD_DIM ** -0.5)
    k = l2norm(k.reshape(hd))
    v = v.reshape(hd).astype(f32)
    gz = z[..., OFF_GDN_Z:OFF_GDN_A].reshape(hd).astype(f32)
    g = -jnp.exp(gdn_a_log.astype(f32)) * jax.nn.softplus(z[..., OFF_GDN_A:OFF_GDN_B].astype(f32) + gdn_dt_bias.astype(f32))
    beta = jax.nn.sigmoid(z[..., OFF_GDN_B:OFF_RWKV].astype(f32))
    o = gated_delta_rule_chunked(q, k, v, g, beta)
    o = o * lax.rsqrt(jnp.mean(o * o, axis=-1, keepdims=True) + RMS_EPS) * gdn_onorm_g.astype(f32) * jax.nn.silu(gz)
    h_a = o.reshape(B, T, GDN_WIDTH).astype(h.dtype) @ w_gdn_o

    W = RWKV_WIDTH
    zr = z[..., OFF_RWKV:OFF_GATES]
    zr = zr + (token_shift(zr) - zr) * rwkv_mu
    r, kr, vr = zr[..., :W], zr[..., W:2 * W], zr[..., 2 * W:3 * W]
    wd = zr[..., 3 * W:3 * W + DECAY_LORA]
    ad = zr[..., 3 * W + DECAY_LORA:3 * W + DECAY_LORA + AAA_LORA]
    gd = zr[..., 3 * W + DECAY_LORA + AAA_LORA:]
    w_log = -jax.nn.softplus(-(rwkv_w0 + jnp.tanh(wd) @ rwkv_w_up).astype(f32)) - 0.5
    decay = jnp.exp(-jnp.exp(w_log))
    a_lr = jax.nn.sigmoid((rwkv_a0 + ad @ rwkv_a_up).astype(f32))
    gate = (jax.nn.sigmoid(gd) @ rwkv_g_up).astype(f32)
    rh = (B, T, RWKV_HEADS, RWKV_HEAD_DIM)
    kk = l2norm((kr * rwkv_k_k).reshape(rh))
    k_eff = (kr.astype(f32) * (1 + (a_lr - 1) * rwkv_k_a.astype(f32))).reshape(rh)
    r_h = r.reshape(rh).astype(f32)
    v_h = vr.reshape(rh).astype(f32)
    y = rwkv7_recurrence(r_h, decay.reshape(rh), k_eff, v_h, -kk, kk * a_lr.reshape(rh))
    y_mu = jnp.mean(y, axis=-1, keepdims=True)
    y_var = jnp.mean(jnp.square(y - y_mu), axis=-1, keepdims=True)
    y = (y - y_mu) * lax.rsqrt(y_var + RWKV_GN_EPS)
    y = y * rwkv_ln_w.astype(f32).reshape(RWKV_HEADS, RWKV_HEAD_DIM) + rwkv_ln_b.astype(f32).reshape(RWKV_HEADS, RWKV_HEAD_DIM)
    y = y + jnp.sum(r_h * k_eff * rwkv_r_k.astype(f32), axis=-1, keepdims=True) * v_h
    y = y.reshape(B, T, W) * gate
    h_b = y.astype(h.dtype) @ w_rwkv_o

    gates = jax.nn.sigmoid(z[..., OFF_GATES:].astype(f32))
    m = gates[..., :D_MODEL] * h_a + gates[..., D_MODEL:] * h_b
    return m.astype(h.dtype) @ w_out


def hierarchical_moe(h, w_group, b_group, w_expert, b_expert, w1, w3, w2):
    B, T, D = h.shape
    N = B * T
    A = N * TOP_K
    f32 = jnp.float32
    hf = h.reshape(N, D)
    g_logits = (hf @ w_group).astype(f32) + b_group.astype(f32)
    g_sel = jnp.argmax(g_logits, axis=-1)
    g_prob = jnp.take_along_axis(jax.nn.softmax(g_logits, axis=-1), g_sel[:, None], axis=-1)
    e_logits = ((hf @ w_expert).astype(f32) + b_expert.astype(f32)).reshape(N, N_GROUPS, EXPERTS_PER_GROUP)
    e_logits = jnp.take_along_axis(e_logits, g_sel[:, None, None], axis=1)[:, 0]
    top_p, top_i = lax.top_k(jax.nn.softmax(e_logits, axis=-1), TOP_K)
    top_p = top_p / jnp.sum(top_p, axis=-1, keepdims=True)
    expert_id = (g_sel[:, None] * EXPERTS_PER_GROUP + top_i).reshape(A).astype(jnp.int32)
    weight = (g_prob * top_p).reshape(A)
    token_id = jnp.repeat(jnp.arange(N, dtype=jnp.int32), TOP_K)
    order = jnp.argsort(expert_id)
    e_sorted = expert_id[order]
    counts = jax.ops.segment_sum(jnp.ones((A,), jnp.int32), expert_id, num_segments=N_EXPERTS)
    padded = (counts + MOE_BLOCK - 1) // MOE_BLOCK * MOE_BLOCK
    padded_end = jnp.cumsum(padded)
    dest = (padded_end - padded)[e_sorted] + jnp.arange(A, dtype=jnp.int32) - (jnp.cumsum(counts) - counts)[e_sorted]
    P = (A + N_EXPERTS * (MOE_BLOCK - 1) + MOE_BLOCK - 1) // MOE_BLOCK * MOE_BLOCK
    n_blocks = P // MOE_BLOCK
    buf_tok = jnp.full((P,), N, jnp.int32).at[dest].set(token_id[order])
    buf_w = jnp.zeros((P,), f32).at[dest].set(weight[order])
    block_expert = jnp.minimum(
        jnp.searchsorted(padded_end, jnp.arange(n_blocks, dtype=jnp.int32) * MOE_BLOCK, side='right'),
        N_EXPERTS - 1)
    h_pad = jnp.concatenate([hf, jnp.zeros((1, D), hf.dtype)], axis=0)
    xb = h_pad[buf_tok].reshape(n_blocks, MOE_BLOCK, D)

    def expert_block(args):
        xblk, e = args
        return (jax.nn.silu(xblk @ w1[e]) * (xblk @ w3[e])) @ w2[e]

    yb = lax.map(expert_block, (xb, block_expert)).reshape(P, D)
    out = jax.ops.segment_sum(yb.astype(f32) * buf_w[:, None], buf_tok, num_segments=N + 1)[:N]
    return out.reshape(B, T, D).astype(h.dtype)


def setup_inputs(seed: int = 0) -> dict:
    key = jax.random.key(seed)
    ks = iter(jax.random.split(key, 48))
    f32 = jnp.float32
    L = DEPTH

    def nrm(shape, scale):
        return scale * jax.random.normal(next(ks), shape, f32)

    def unif(shape, lo, hi):
        return jax.random.uniform(next(ks), shape, f32, minval=lo, maxval=hi)

    dt = jnp.exp(unif((L, GDN_HEADS), math.log(1e-3), math.log(0.1)))
    return {
        'x': nrm((BATCH, SEQ, D_MODEL), 1.0),
        'c': nrm((BATCH, D_MODEL), 1.0),
        'w_ada': nrm((L, D_MODEL, N_MOD * D_MODEL), 0.5 * D_MODEL ** -0.5),
        'b_ada': nrm((L, N_MOD * D_MODEL), 0.02),
        'norm1_g': 1.0 + nrm((L, D_MODEL), 0.02),
        'w_in': nrm((L, D_MODEL, IN_COLS), D_MODEL ** -0.5),
        'conv_w': nrm((L, CONV_WIDTH, GDN_CONV_COLS), CONV_WIDTH ** -0.5),
        'gdn_a_log': jnp.log(unif((L, GDN_HEADS), 1.0, 16.0)),
        'gdn_dt_bias': dt + jnp.log(-jnp.expm1(-dt)),
        'gdn_onorm_g': 1.0 + nrm((L, GDN_HEAD_DIM), 0.02),
        'rwkv_mu': unif((L, RWKV_SHIFT_COLS), 0.0, 1.0),
        'rwkv_w0': unif((L, RWKV_WIDTH), -6.0, -1.0),
        'rwkv_w_up': nrm((L, DECAY_LORA, RWKV_WIDTH), 0.1 * DECAY_LORA ** -0.5),
        'rwkv_a0': nrm((L, RWKV_WIDTH), 0.1),
        'rwkv_a_up': nrm((L, AAA_LORA, RWKV_WIDTH), AAA_LORA ** -0.5),
        'rwkv_g_up': nrm((L, GATE_LORA, RWKV_WIDTH), GATE_LORA ** -0.5),
        'rwkv_k_k': 0.85 + nrm((L, RWKV_WIDTH), 0.02),
        'rwkv_k_a': 1.0 + nrm((L, RWKV_WIDTH), 0.02),
        'rwkv_r_k': nrm((L, RWKV_HEADS, RWKV_HEAD_DIM), 0.1),
        'rwkv_ln_w': 1.0 + nrm((L, RWKV_WIDTH), 0.02),
        'rwkv_ln_b': nrm((L, RWKV_WIDTH), 0.02),
        'w_gdn_o': nrm((L, GDN_WIDTH, D_MODEL), GDN_WIDTH ** -0.5),
        'w_rwkv_o': nrm((L, RWKV_WIDTH, D_MODEL), RWKV_WIDTH ** -0.5),
        'w_out': nrm((L, D_MODEL, D_MODEL), D_MODEL ** -0.5),
        'norm2_g': 1.0 + nrm((L, D_MODEL), 0.02),
        'w_group': nrm((L, D_MODEL, N_GROUPS), D_MODEL ** -0.5),
        'b_group': nrm((L, N_GROUPS), 0.01),
        'w_expert': nrm((L, D_MODEL, N_EXPERTS), D_MODEL ** -0.5),
        'b_expert': nrm((L, N_EXPERTS), 0.01),
        'w1': nrm((L, N_EXPERTS, D_MODEL, D_EXPERT), D_MODEL ** -0.5),
        'w3': nrm((L, N_EXPERTS, D_MODEL, D_EXPERT), D_MODEL ** -0.5),
        'w2': nrm((L, N_EXPERTS, D_EXPERT, D_MODEL), D_EXPERT ** -0.5),
        'norm_f_g': 1.0 + nrm((D_MODEL,), 0.02),
    }


def reference(x, c, w_ada, b_ada, norm1_g, w_in, conv_w, gdn_a_log, gdn_dt_bias, gdn_onorm_g, rwkv_mu,
              rwkv_w0, rwkv_w_up, rwkv_a0, rwkv_a_up, rwkv_g_up, rwkv_k_k, rwkv_k_a, rwkv_r_k, rwkv_ln_w,
              rwkv_ln_b, w_gdn_o, w_rwkv_o, w_out, norm2_g, w_group, b_group, w_expert, b_expert, w1, w3,
              w2, norm_f_g):
    for l in range(DEPTH):
        mod = jax.nn.silu(c) @ w_ada[l] + b_ada[l]
        sh1, sc1, g1, sh2, sc2, g2 = jnp.split(mod[:, None, :], N_MOD, axis=-1)
        h = rmsnorm(x, norm1_g[l]) * (1 + sc1) + sh1
        x = x + g1 * hybrid_mixer(h, w_in[l], conv_w[l], gdn_a_log[l], gdn_dt_bias[l], gdn_onorm_g[l],
                                  rwkv_mu[l], rwkv_w0[l], rwkv_w_up[l], rwkv_a0[l], rwkv_a_up[l],
                                  rwkv_g_up[l], rwkv_k_k[l], rwkv_k_a[l], rwkv_r_k[l], rwkv_ln_w[l],
                                  rwkv_ln_b[l], w_gdn_o[l], w_rwkv_o[l], w_out[l])
        h = rmsnorm(x, norm2_g[l]) * (1 + sc2) + sh2
        x = x + g2 * hierarchical_moe(h, w_group[l], b_group[l], w_expert[l], b_expert[l], w1[l], w3[l], w2[l])
    return rmsnorm(x, norm_f_g)
```

```python
import functools
import math

import jax
import jax.numpy as jnp
from jax import lax
from jax.experimental import pallas as pl
from jax.experimental.pallas import tpu as pltpu

F32 = jnp.float32
BF16 = jnp.bfloat16
HI = lax.Precision.HIGHEST

LANE = 128
CHUNK = 64
CONV_WIDTH = 4
RMS_EPS = 1e-6
RWKV_GN_EPS = 64e-5
MOE_BLOCK = 128
TOP_K = 2
VMEM_LIMIT = 56 * 1024 * 1024


def _cparams(sem):
    return pltpu.CompilerParams(dimension_semantics=sem, vmem_limit_bytes=VMEM_LIMIT)


def _dot(a, b):
    return jnp.dot(a.astype(BF16), b.astype(BF16), preferred_element_type=F32)


def _dot_nt(a, b):
    return lax.dot_general(a.astype(BF16), b.astype(BF16), (((1,), (1,)), ((), ())),
                           preferred_element_type=F32)


def _dot_tn(a, b):
    return lax.dot_general(a.astype(BF16), b.astype(BF16), (((0,), (0,)), ((), ())),
                           preferred_element_type=F32)


def _dot_hi(a, b):
    return jnp.dot(a, b, precision=HI, preferred_element_type=F32)


def _sigmoid(x):
    return 1.0 / (1.0 + jnp.exp(-x))


def _silu(x):
    return x * _sigmoid(x)


def _softplus(x):
    return jnp.maximum(x, 0.0) + jnp.log(1.0 + jnp.exp(-jnp.abs(x)))


def _inv_unit_lower(L, n, top):
    r = lax.broadcasted_iota(jnp.int32, (n, n), 0)
    c = lax.broadcasted_iota(jnp.int32, (n, n), 1)
    eye = (r == c).astype(F32)
    same16 = (r // 16) == (c // 16)
    Ld = jnp.where(same16, L, 0.0)
    X = eye - Ld
    P = _dot_hi(Ld, Ld)
    X = X + _dot_hi(X, P)
    P = _dot_hi(P, P)
    X = X + _dot_hi(X, P)
    P = _dot_hi(P, P)
    X = X + _dot_hi(X, P)
    bs = 32
    while bs <= top:
        inner = (r // (bs // 2)) == (c // (bs // 2))
        outer = (r // bs) == (c // bs)
        Lo = jnp.where(jnp.logical_and(outer, jnp.logical_not(inner)), L, 0.0)
        X = X - _dot_hi(X, _dot_hi(Lo, X))
        bs *= 2
    return X


def _mm_kernel(a_ref, w_ref, o_ref):
    o_ref[...] = jnp.dot(a_ref[...], w_ref[...], preferred_element_type=F32).astype(o_ref.dtype)


def _matmul(a, w, tm, tn, out_dtype):
    M, K = a.shape
    Nn = w.shape[1]
    return pl.pallas_call(
        _mm_kernel,
        grid=(M // tm, Nn // tn),
        in_specs=[pl.BlockSpec((tm, K), lambda i, j: (i, 0)),
                  pl.BlockSpec((K, tn), lambda i, j: (0, j))],
        out_specs=pl.BlockSpec((tm, tn), lambda i, j: (i, j)),
        out_shape=jax.ShapeDtypeStruct((M, Nn), out_dtype),
        compiler_params=_cparams(("parallel", "arbitrary")),
        name="matmul",
    )(a, w)


def _gdn_kernel(zq_ref, zk_ref, zv_ref, gz_ref, sm_ref, cwq_ref, cwk_ref, cwv_ref, alog_ref, dtb_ref,
                on_ref, o_ref, s_ref, xs_ref, *, tb, a_lane0, b_lane0, q_scale):
    C = CHUNK
    h = pl.program_id(1)
    t = pl.program_id(2)

    @pl.when(t == 0)
    def _():
        s_ref[...] = jnp.zeros_like(s_ref)
        xs_ref[:, 0:8, :] = jnp.zeros((3, 8, LANE), F32)

    def conv_silu(idx, z_ref, cw_ref):
        xs_ref[idx, 8:8 + tb, :] = z_ref[...]
        cw = cw_ref[...]
        acc = cw[CONV_WIDTH - 1:CONV_WIDTH, :] * xs_ref[idx, 8:8 + tb, :]
        for i in range(1, CONV_WIDTH):
            acc = acc + cw[CONV_WIDTH - 1 - i:CONV_WIDTH - i, :] * xs_ref[idx, 8 - i:8 - i + tb, :]
        xs_ref[idx, 0:8, :] = xs_ref[idx, tb:tb + 8, :]
        return _silu(acc)

    def l2n(x):
        return x * lax.rsqrt(jnp.sum(x * x, axis=-1, keepdims=True) + 1e-6)

    q_all = l2n(conv_silu(0, zq_ref, cwq_ref)) * q_scale
    k_all = l2n(conv_silu(1, zk_ref, cwk_ref))
    v_all = conv_silu(2, zv_ref, cwv_ref)

    sm = sm_ref[...]
    lane = lax.broadcasted_iota(jnp.int32, (1, LANE), 1)
    g_full = -jnp.exp(alog_ref[...]) * _softplus(sm + dtb_ref[...])
    g_all = jnp.sum(jnp.where(lane == a_lane0 + h, g_full, 0.0), axis=1, keepdims=True)
    beta_all = jnp.sum(jnp.where(lane == b_lane0 + h, _sigmoid(sm), 0.0), axis=1, keepdims=True)

    r = lax.broadcasted_iota(jnp.int32, (C, C), 0)
    c = lax.broadcasted_iota(jnp.int32, (C, C), 1)
    causal = r >= c
    strict = r > c
    tril = causal.astype(F32)
    ones = jnp.ones((C, C), F32)
    onorm = on_ref[...]

    for ci in range(tb // C):
        sl = slice(ci * C, (ci + 1) * C)
        q, k, v = q_all[sl], k_all[sl], v_all[sl]
        g, beta = g_all[sl], beta_all[sl]
        gb = jnp.broadcast_to(g, (C, C))
        gc_col = _dot_hi(tril, gb)
        gc_row = _dot_hi(ones, jnp.where(r <= c, gb, 0.0))
        decay = jnp.where(causal, jnp.exp(jnp.minimum(gc_col - gc_row, 0.0)), 0.0)
        gc = gc_col[:, 0:1]
        g_last = gc_col[C - 1:C, 0:1]
        e_gc = jnp.exp(gc)
        kb = k * beta
        sc = _dot_nt(jnp.concatenate([kb, q], axis=0), k)
        L = jnp.where(strict, sc[:C] * decay, 0.0)
        attn = sc[C:] * decay
        T = _inv_unit_lower(L, C, C)
        uw = _dot_hi(T, jnp.concatenate([v * beta, kb * e_gc], axis=1))
        u, w = uw[:, :LANE], uw[:, LANE:]
        q_dec = q * e_gc
        k_dec = k * jnp.exp(g_last - gc)
        S = s_ref[...]
        ws = _dot(jnp.concatenate([w, q_dec], axis=0), S)
        v_new = u - ws[:C]
        o = ws[C:] + _dot(attn, v_new)
        s_ref[...] = S * jnp.exp(g_last) + _dot_tn(k_dec, v_new)
        gz = gz_ref[sl, :]
        o = o * lax.rsqrt(jnp.mean(o * o, axis=-1, keepdims=True) + RMS_EPS) * onorm * _silu(gz)
        o_ref[sl, :] = o.astype(o_ref.dtype)


def _gdn_branch(z, conv_w, alog_pad, dtb_pad, onorm_g, *, batch, seq, heads, cb_q, cb_gz, cb_sm,
                a_lane0, b_lane0, tb):
    n = batch * seq
    nt = seq // tb
    row = lambda b, h, t: b * nt + t
    zspec = lambda cb: pl.BlockSpec((tb, LANE), lambda b, h, t: (row(b, h, t), cb + h))
    cwspec = lambda cb: pl.BlockSpec((CONV_WIDTH, LANE), lambda b, h, t: (0, cb + h))
    vec = pl.BlockSpec((1, LANE), lambda b, h, t: (0, 0))
    kern = functools.partial(_gdn_kernel, tb=tb, a_lane0=a_lane0, b_lane0=b_lane0,
                             q_scale=float(LANE) ** -0.5)
    return pl.pallas_call(
        kern,
        grid=(batch, heads, nt),
        in_specs=[zspec(cb_q), zspec(cb_q + heads), zspec(cb_q + 2 * heads), zspec(cb_gz),
                  pl.BlockSpec((tb, LANE), lambda b, h, t: (row(b, h, t), cb_sm)),
                  cwspec(0), cwspec(heads), cwspec(2 * heads), vec, vec, vec],
        out_specs=pl.BlockSpec((tb, LANE), lambda b, h, t: (row(b, h, t), h)),
        out_shape=jax.ShapeDtypeStruct((n, heads * LANE), BF16),
        scratch_shapes=[pltpu.VMEM((LANE, LANE), F32), pltpu.VMEM((3, tb + 8, LANE), F32)],
        compiler_params=_cparams(("parallel", "parallel", "arbitrary")),
        name="gdn",
    )(z, z, z, z, z, conv_w, conv_w, conv_w, alog_pad, dtb_pad, onorm_g)


_RW_ROWS = ("w0", "a0", "k_k", "k_a", "r_k", "ln_w", "ln_b", "mu_r", "mu_k", "mu_v")


def _rwkv_kernel(zr_ref, zk_ref, zv_ref, sm0_ref, sm1_ref, sm2_ref, sm3_ref, pv_ref, musm_ref,
                 wup_ref, aup_ref, gup_ref, o_ref, h_ref, c_ref, *, tb, hd):
    C = CHUNK
    t = pl.program_id(2)

    @pl.when(t == 0)
    def _():
        h_ref[...] = jnp.zeros_like(h_ref)
        c_ref[...] = jnp.zeros_like(c_ref)

    pv = pv_ref[...]
    prow = lambda name: pv[_RW_ROWS.index(name):_RW_ROWS.index(name) + 1, :]
    musm = musm_ref[...]
    row0 = lax.broadcasted_iota(jnp.int32, (tb, LANE), 0) == 0

    def shift(idx, ref, mu):
        x = ref[...]
        prev = jnp.where(row0, c_ref[idx, 0:1, :], pltpu.roll(x, 1, 0))
        c_ref[idx, 0:1, :] = x[tb - 1:tb, :]
        return x + (prev - x) * mu

    r_all = shift(0, zr_ref, prow("mu_r"))
    k_all = shift(1, zk_ref, prow("mu_k"))
    v_all = shift(2, zv_ref, prow("mu_v"))
    wd = shift(3, sm0_ref, musm[0:1, 0:LANE])
    ad = shift(4, sm1_ref, musm[0:1, LANE:2 * LANE])
    gd0 = shift(5, sm2_ref, musm[0:1, 2 * LANE:3 * LANE])
    gd1 = shift(6, sm3_ref, musm[0:1, 3 * LANE:4 * LANE])

    w_lin = prow("w0") + _dot(jnp.tanh(wd), wup_ref[...])
    logw_all = -jnp.exp(-_softplus(-w_lin) - 0.5)
    alr_all = _sigmoid(prow("a0") + _dot(ad, aup_ref[...]))
    gate_all = _dot(_sigmoid(gd0), gup_ref[0:LANE, :]) + _dot(_sigmoid(gd1), gup_ref[LANE:2 * LANE, :])

    rr = lax.broadcasted_iota(jnp.int32, (LANE, LANE), 0)
    cc = lax.broadcasted_iota(jnp.int32, (LANE, LANE), 1)
    same_head = (rr // hd) == (cc // hd)
    bd_ones = same_head.astype(F32)
    strict = jnp.logical_and(same_head, rr > cc)
    incl = jnp.logical_and(same_head, rr >= cc)
    r64 = lax.broadcasted_iota(jnp.int32, (C, C), 0)
    c64 = lax.broadcasted_iota(jnp.int32, (C, C), 1)
    tril = (r64 >= c64).astype(F32)
    lane = lax.broadcasted_iota(jnp.int32, (1, LANE), 1)
    m0 = (lane < hd).astype(F32)
    m1 = 1.0 - m0

    kraw = k_all * prow("k_k")
    kk_all = kraw * lax.rsqrt(_dot_hi(kraw * kraw, bd_ones) + 1e-6)
    keff_all = k_all * (1.0 + (alr_all - 1.0) * prow("k_a"))
    bonus_all = _dot_hi(r_all * keff_all * prow("r_k"), bd_ones) * v_all

    for ci in range(tb // C):
        sl = slice(ci * C, (ci + 1) * C)
        r, v, keff, kk, alr = r_all[sl], v_all[sl], keff_all[sl], kk_all[sl], alr_all[sl]
        lw = logw_all[sl]
        cum = _dot_hi(tril, lw)
        cum_last = cum[C - 1:C, :]
        e_neg = jnp.exp(-cum)
        e_end = jnp.exp(cum_last - cum)
        rt = r * jnp.exp(cum)
        at = -kk * jnp.exp(cum - lw)
        b = kk * alr
        bt, kt = b * e_neg, keff * e_neg
        bh, kh = b * e_end, keff * e_end
        X = jnp.concatenate([at * m0, at * m1, rt * m0, rt * m1], axis=0)
        Y = jnp.concatenate([bt, bt, kt, kt], axis=0)
        sc = _dot_nt(X, Y)
        T = _inv_unit_lower(jnp.where(strict, -sc[:LANE, :LANE], 0.0), LANE, hd)
        a_ak = jnp.where(strict, sc[:LANE, LANE:], 0.0)
        a_rb = jnp.where(incl, sc[LANE:, :LANE], 0.0)
        a_rk = jnp.where(incl, sc[LANE:, LANE:], 0.0)
        v_st = jnp.concatenate([v * m0, v * m1], axis=0)
        H = h_ref[...]
        xh = _dot_nt(X, H)
        U = _dot_hi(T, xh[:LANE] + _dot(a_ak, v_st))
        o_st = xh[LANE:] + _dot(a_rb, U) + _dot(a_rk, v_st)
        y = o_st[:C] + o_st[C:]
        b_st = jnp.concatenate([bh * m0, bh * m1], axis=0)
        k_st = jnp.concatenate([kh * m0, kh * m1], axis=0)
        h_ref[...] = H * jnp.exp(cum_last) + _dot_tn(U, b_st) + _dot_tn(v_st, k_st)
        mu = _dot_hi(y, bd_ones) * (1.0 / hd)
        yc = y - mu
        var = _dot_hi(yc * yc, bd_ones) * (1.0 / hd)
        y = yc * lax.rsqrt(var + RWKV_GN_EPS) * prow("ln_w") + prow("ln_b") + bonus_all[sl]
        o_ref[sl, :] = (y * gate_all[sl]).astype(o_ref.dtype)


def _rwkv_branch(z, pv, musm, wup, aup, gup, *, batch, seq, width, hd, cb_r, cb_sm, tb):
    n = batch * seq
    nt = seq // tb
    hp = width // LANE
    row = lambda b, p, t: b * nt + t
    zspec = lambda cb: pl.BlockSpec((tb, LANE), lambda b, p, t: (row(b, p, t), cb + p))
    smspec = lambda i: pl.BlockSpec((tb, LANE), lambda b, p, t: (row(b, p, t), cb_sm + i))
    colblk = lambda rows: pl.BlockSpec((rows, LANE), lambda b, p, t: (0, p))
    kern = functools.partial(_rwkv_kernel, tb=tb, hd=hd)
    return pl.pallas_call(
        kern,
        grid=(batch, hp, nt),
        in_specs=[zspec(cb_r), zspec(cb_r + hp), zspec(cb_r + 2 * hp),
                  smspec(0), smspec(1), smspec(2), smspec(3),
                  colblk(pv.shape[0]),
                  pl.BlockSpec(musm.shape, lambda b, p, t: (0, 0)),
                  colblk(LANE), colblk(LANE), colblk(2 * LANE)],
        out_specs=pl.BlockSpec((tb, LANE), lambda b, p, t: (row(b, p, t), p)),
        out_shape=jax.ShapeDtypeStruct((n, width), BF16),
        scratch_shapes=[pltpu.VMEM((LANE, LANE), F32), pltpu.VMEM((7, 8, LANE), F32)],
        compiler_params=_cparams(("parallel", "parallel", "arbitrary")),
        name="rwkv7",
    )(z, z, z, z, z, z, z, pv, musm, wup, aup, gup)


class _Layout:
    def __init__(self, d_model, gdn_heads, rwkv_width, decay_lora, aaa_lora, gate_lora, tn):
        self.d_model, self.gh, self.rw = d_model, gdn_heads, rwkv_width
        self.dl, self.al, self.gl = decay_lora, aaa_lora, gate_lora
        self.gw = gdn_heads * LANE
        assert decay_lora + 2 * gdn_heads <= LANE and aaa_lora <= LANE and gate_lora == 2 * LANE
        self.a_lane0 = decay_lora
        self.b_lane0 = decay_lora + gdn_heads
        self.cb_q = 0
        self.cb_gz = 3 * self.gw // LANE
        self.cb_r = self.cb_gz + self.gw // LANE
        self.cb_sm = self.cb_r + 3 * rwkv_width // LANE
        self.cb_gates = self.cb_sm + 4
        cols = (self.cb_gates) * LANE + 2 * d_model
        self.cols = -(-cols // tn) * tn
        self.off_z = 3 * self.gw
        self.off_a = self.off_z + self.gw
        self.off_b = self.off_a + gdn_heads
        self.off_rwkv = self.off_b + gdn_heads
        self.off_gates = self.off_rwkv + 3 * rwkv_width + decay_lora + aaa_lora + gate_lora

    def pack_cols(self, w):
        lead = w.shape[:-1]
        zeros = lambda n: jnp.zeros(lead + (n,), w.dtype)
        rw, dl, al, gl, gh = self.rw, self.dl, self.al, self.gl, self.gh
        o = self.off_rwkv + 3 * rw
        parts = [w[..., :self.off_a],
                 w[..., self.off_rwkv:o],
                 w[..., o:o + dl], w[..., self.off_a:self.off_b], w[..., self.off_b:self.off_rwkv],
                 zeros(LANE - dl - 2 * gh),
                 w[..., o + dl:o + dl + al], zeros(LANE - al),
                 w[..., o + dl + al:o + dl + al + gl],
                 w[..., self.off_gates:]]
        packed = jnp.concatenate(parts, axis=-1)
        return jnp.concatenate([packed, zeros(self.cols - packed.shape[-1])], axis=-1)


def _pad_rows(w, rows):
    return jnp.concatenate([w, jnp.zeros((rows - w.shape[0],) + w.shape[1:], w.dtype)], axis=0)


def _gdn_from_packed(z, p, lay, batch, seq, tb=256):
    lane_vec = lambda v, lane0: jnp.zeros((1, LANE), F32).at[0, lane0:lane0 + v.shape[0]].set(v)
    return _gdn_branch(z, p["conv_w"], lane_vec(p["gdn_a_log"], lay.a_lane0),
                       lane_vec(p["gdn_dt_bias"], lay.a_lane0), p["gdn_onorm_g"].reshape(1, LANE),
                       batch=batch, seq=seq, heads=lay.gh, cb_q=lay.cb_q, cb_gz=lay.cb_gz,
                       cb_sm=lay.cb_sm, a_lane0=lay.a_lane0, b_lane0=lay.b_lane0, tb=min(tb, seq))


def _rwkv_from_packed(z, p, lay, batch, seq, tb=256):
    rw = lay.rw
    hd = p["rwkv_r_k"].shape[-1]
    assert LANE % hd == 0 and hd * 2 == LANE
    mu = p["rwkv_mu"]
    rows = {"w0": p["rwkv_w0"], "a0": p["rwkv_a0"], "k_k": p["rwkv_k_k"], "k_a": p["rwkv_k_a"],
            "r_k": p["rwkv_r_k"].reshape(rw), "ln_w": p["rwkv_ln_w"], "ln_b": p["rwkv_ln_b"],
            "mu_r": mu[:rw], "mu_k": mu[rw:2 * rw], "mu_v": mu[2 * rw:3 * rw]}
    pv = _pad_rows(jnp.stack([rows[k] for k in _RW_ROWS]).astype(F32), 16)
    o = 3 * rw
    zl = lambda n: jnp.zeros((n,), F32)
    musm = jnp.concatenate([mu[o:o + lay.dl], zl(LANE - lay.dl), mu[o + lay.dl:o + lay.dl + lay.al],
                            zl(LANE - lay.al), mu[o + lay.dl + lay.al:]])
    musm = _pad_rows(musm.reshape(1, 4 * LANE), 8)
    wup = _pad_rows(p["rwkv_w_up"], LANE).astype(BF16)
    aup = _pad_rows(p["rwkv_a_up"], LANE).astype(BF16)
    gup = p["rwkv_g_up"].astype(BF16)
    return _rwkv_branch(z, pv, musm, wup, aup, gup, batch=batch, seq=seq, width=rw, hd=hd,
                        cb_r=lay.cb_r, cb_sm=lay.cb_sm, tb=min(tb, seq))


def _make_layout(p, tn):
    return _Layout(p["norm1_g"].shape[-1], p["gdn_a_log"].shape[-1], p["rwkv_w0"].shape[-1],
                   p["rwkv_w_up"].shape[0], p["rwkv_a_up"].shape[0], p["rwkv_g_up"].shape[0], tn)


def _ada_kernel(c_ref, w_ref, b_ref, o_ref):
    o_ref[...] = _dot(_silu(c_ref[...]), w_ref[...]) + b_ref[...]


def _ada_mod(c_pad, w_ada, b_ada, tn):
    d, cols = w_ada.shape
    return pl.pallas_call(
        _ada_kernel,
        grid=(cols // tn,),
        in_specs=[pl.BlockSpec((8, d), lambda j: (0, 0)),
                  pl.BlockSpec((d, tn), lambda j: (0, j)),
                  pl.BlockSpec((1, tn), lambda j: (0, j))],
        out_specs=pl.BlockSpec((8, tn), lambda j: (0, j)),
        out_shape=jax.ShapeDtypeStruct((8, cols), F32),
        compiler_params=_cparams(("arbitrary",)),
        name="ada_mod",
    )(c_pad, w_ada, b_ada.reshape(1, cols))


def _modnorm(x, g, sc, sh):
    y = x * lax.rsqrt(jnp.mean(x * x, axis=-1, keepdims=True) + RMS_EPS)
    return y * g * (1.0 + sc) + sh


def _norm1_kernel(x_ref, g_ref, sh_ref, sc_ref, o_ref):
    b = pl.program_id(0)
    h = _modnorm(x_ref[...], g_ref[...], sc_ref[pl.ds(b, 1), :], sh_ref[pl.ds(b, 1), :])
    o_ref[...] = h.astype(o_ref.dtype)


def _norm1(x2, g, mod, *, batch, seq, tm):
    n, d = x2.shape
    nt = seq // tm
    return pl.pallas_call(
        _norm1_kernel,
        grid=(batch, nt),
        in_specs=[pl.BlockSpec((tm, d), lambda b, t: (b * nt + t, 0)),
                  pl.BlockSpec((1, d), lambda b, t: (0, 0)),
                  pl.BlockSpec((8, d), lambda b, t: (0, 0)),
                  pl.BlockSpec((8, d), lambda b, t: (0, 1))],
        out_specs=pl.BlockSpec((tm, d), lambda b, t: (b * nt + t, 0)),
        out_shape=jax.ShapeDtypeStruct((n, d), BF16),
        compiler_params=_cparams(("parallel", "arbitrary")),
        name="norm1",
    )(x2, g, mod, mod)


def _merge_kernel(o_ref, y_ref, wa_ref, wb_ref, ga_ref, gb_ref, m_ref):
    ha = jnp.dot(o_ref[...], wa_ref[...], preferred_element_type=F32)
    hb = jnp.dot(y_ref[...], wb_ref[...], preferred_element_type=F32)
    m_ref[...] = (_sigmoid(ga_ref[...]) * ha + _sigmoid(gb_ref[...]) * hb).astype(m_ref.dtype)


def _merge(o, y, wa, wb, z, *, cb_gates, tm, tn):
    n, d = o.shape[0], wa.shape[1]
    tn = math.gcd(tn, cb_gates * LANE)
    gb0 = cb_gates * LANE // tn
    return pl.pallas_call(
        _merge_kernel,
        grid=(n // tm, d // tn),
        in_specs=[pl.BlockSpec((tm, o.shape[1]), lambda i, j: (i, 0)),
                  pl.BlockSpec((tm, y.shape[1]), lambda i, j: (i, 0)),
                  pl.BlockSpec((wa.shape[0], tn), lambda i, j: (0, j)),
                  pl.BlockSpec((wb.shape[0], tn), lambda i, j: (0, j)),
                  pl.BlockSpec((tm, tn), lambda i, j: (i, gb0 + j)),
                  pl.BlockSpec((tm, tn), lambda i, j: (i, gb0 + d // tn + j))],
        out_specs=pl.BlockSpec((tm, tn), lambda i, j: (i, j)),
        out_shape=jax.ShapeDtypeStruct((n, d), BF16),
        compiler_params=_cparams(("parallel", "arbitrary")),
        name="merge",
    )(o, y, wa, wb, z, z)


def _outproj_kernel(m_ref, w_ref, x_ref, g_ref, o_ref, *, tiles_per_batch):
    b = pl.program_id(0) // tiles_per_batch
    y = jnp.dot(m_ref[...], w_ref[...], preferred_element_type=F32)
    o_ref[...] = x_ref[...] + g_ref[pl.ds(b, 1), :] * y


def _outproj(m, w, x2, mod, *, seq, tm, tn):
    n, d = x2.shape
    gcol = 2 * d // tn
    return pl.pallas_call(
        functools.partial(_outproj_kernel, tiles_per_batch=seq // tm),
        grid=(n // tm, d // tn),
        in_specs=[pl.BlockSpec((tm, d), lambda i, j: (i, 0)),
                  pl.BlockSpec((d, tn), lambda i, j: (0, j)),
                  pl.BlockSpec((tm, tn), lambda i, j: (i, j)),
                  pl.BlockSpec((8, tn), lambda i, j: (0, gcol + j))],
        out_specs=pl.BlockSpec((tm, tn), lambda i, j: (i, j)),
        out_shape=jax.ShapeDtypeStruct((n, d), F32),
        compiler_params=_cparams(("parallel", "arbitrary")),
        name="outproj",
    )(m, w, x2, mod)


def _router_kernel(x_ref, g_ref, sh_ref, sc_ref, wr_ref, br_ref, h_ref, id_ref, wt_ref, *,
                   n_groups, per_group):
    b = pl.program_id(0)
    h = _modnorm(x_ref[...], g_ref[...], sc_ref[pl.ds(b, 1), :], sh_ref[pl.ds(b, 1), :])
    h_ref[...] = h
    logits = _dot(h, wr_ref[...]) + br_ref[...]
    tm = logits.shape[0]
    lane = lax.broadcasted_iota(jnp.int32, (tm, LANE), 1)
    neg = jnp.float32(-jnp.inf)
    big = jnp.int32(LANE)

    def first_argmax(vals, mask):
        vm = jnp.where(mask, vals, neg)
        mx = jnp.max(vm, axis=1, keepdims=True)
        idx = jnp.min(jnp.where(jnp.logical_and(mask, vm == mx), lane, big), axis=1, keepdims=True)
        return mx, idx

    gmask = lane < n_groups
    gmax, g_sel = first_argmax(logits, gmask)
    g_prob = 1.0 / jnp.sum(jnp.where(gmask, jnp.exp(logits - gmax), 0.0), axis=1, keepdims=True)
    e_lane = lane - n_groups
    emask = jnp.logical_and(e_lane >= g_sel * per_group, e_lane < (g_sel + 1) * per_group)
    emax = jnp.max(jnp.where(emask, logits, neg), axis=1, keepdims=True)
    ex = jnp.where(emask, jnp.exp(logits - emax), 0.0)
    probs = ex / jnp.sum(ex, axis=1, keepdims=True)
    p0, i0 = first_argmax(probs, emask)
    p1, i1 = first_argmax(probs, jnp.logical_and(emask, lane != i0))
    den = p0 + p1
    id_ref[...] = jnp.where(lane == 0, i0 - n_groups, jnp.where(lane == 1, i1 - n_groups, 0))
    wt_ref[...] = jnp.where(lane == 0, g_prob * (p0 / den), jnp.where(lane == 1, g_prob * (p1 / den), 0.0))


def _router(x1, g, mod, wr, br, *, batch, seq, tm, n_groups, per_group):
    n, d = x1.shape
    nt = seq // tm
    rowblk = lambda w: pl.BlockSpec((tm, w), lambda b, t: (b * nt + t, 0))
    return pl.pallas_call(
        functools.partial(_router_kernel, n_groups=n_groups, per_group=per_group),
        grid=(batch, nt),
        in_specs=[rowblk(d),
                  pl.BlockSpec((1, d), lambda b, t: (0, 0)),
                  pl.BlockSpec((8, d), lambda b, t: (0, 3)),
                  pl.BlockSpec((8, d), lambda b, t: (0, 4)),
                  pl.BlockSpec((d, LANE), lambda b, t: (0, 0)),
                  pl.BlockSpec((1, LANE), lambda b, t: (0, 0))],
        out_specs=[rowblk(d), rowblk(LANE), rowblk(LANE)],
        out_shape=[jax.ShapeDtypeStruct((n, d), F32), jax.ShapeDtypeStruct((n, LANE), jnp.int32),
                   jax.ShapeDtypeStruct((n, LANE), F32)],
        compiler_params=_cparams(("parallel", "arbitrary")),
        name="router",
    )(x1, g, mod, mod, wr, br)


def _expert_kernel(be_ref, tok_ref, h_hbm, w1_ref, w3_ref, w2_ref, bw_ref, o_ref, xbuf, sem):
    i = pl.program_id(0)
    nb = pl.num_programs(0)

    def row_copy(blk, slot, r):
        tok = tok_ref[blk * MOE_BLOCK + r]
        return pltpu.make_async_copy(h_hbm.at[pl.ds(tok, 1), :], xbuf.at[slot, pl.ds(r, 1), :],
                                     sem.at[slot])

    def start_block(blk, slot):
        def body(r, carry):
            row_copy(blk, slot, r).start()
            return carry
        lax.fori_loop(0, MOE_BLOCK, body, 0)

    @pl.when(i == 0)
    def _():
        start_block(0, 0)

    @pl.when(i + 1 < nb)
    def _():
        start_block(i + 1, (i + 1) % 2)

    slot = i % 2

    def wait_body(r, carry):
        row_copy(i, slot, r).wait()
        return carry
    lax.fori_loop(0, MOE_BLOCK, wait_body, 0)

    x = xbuf[slot]
    a = _dot(x, w1_ref[...])
    g = _dot(x, w3_ref[...])
    y = _dot(_silu(a) * g, w2_ref[...])
    o_ref[...] = y * bw_ref[...]


def _experts(h2, w1, w3, w2, block_expert, buf_tok, buf_w):
    d = h2.shape[1]
    de = w1.shape[2]
    p_rows = buf_tok.shape[0]
    wspec = lambda shape: pl.BlockSpec((None,) + shape, lambda i, be, tok: (be[i], 0, 0))
    grid_spec = pltpu.PrefetchScalarGridSpec(
        num_scalar_prefetch=2,
        grid=(p_rows // MOE_BLOCK,),
        in_specs=[pl.BlockSpec(memory_space=pl.ANY),
                  wspec((d, de)), wspec((d, de)), wspec((de, d)),
                  pl.BlockSpec((MOE_BLOCK, 1), lambda i, be, tok: (i, 0))],
        out_specs=pl.BlockSpec((MOE_BLOCK, d), lambda i, be, tok: (i, 0)),
        scratch_shapes=[pltpu.VMEM((2, MOE_BLOCK, d), F32), pltpu.SemaphoreType.DMA((2,))],
    )
    return pl.pallas_call(
        _expert_kernel,
        grid_spec=grid_spec,
        out_shape=jax.ShapeDtypeStruct((p_rows, d), F32),
        compiler_params=_cparams(("arbitrary",)),
        name="experts",
    )(block_expert, buf_tok, h2, w1, w3, w2, buf_w)


def _combine_kernel(pos_ref, x_ref, g_ref, gf_ref, y_hbm, o_ref, ybuf, sem, *, tm, tiles_per_batch):
    i = pl.program_id(0)
    nb = pl.num_programs(0)
    rows = TOP_K * tm

    def row_copy(blk, slot, r):
        return pltpu.make_async_copy(y_hbm.at[pl.ds(pos_ref[blk * rows + r], 1), :],
                                     ybuf.at[slot, pl.ds(r, 1), :], sem.at[slot])

    def start_block(blk, slot):
        def body(r, carry):
            row_copy(blk, slot, r).start()
            return carry
        lax.fori_loop(0, rows, body, 0)

    @pl.when(i == 0)
    def _():
        start_block(0, 0)

    @pl.when(i + 1 < nb)
    def _():
        start_block(i + 1, (i + 1) % 2)

    slot = i % 2

    def wait_body(r, carry):
        row_copy(i, slot, r).wait()
        return carry
    lax.fori_loop(0, rows, wait_body, 0)

    b = i // tiles_per_batch
    moe = ybuf[slot, 0:tm, :] + ybuf[slot, tm:rows, :]
    x = x_ref[...] + g_ref[pl.ds(b, 1), :] * moe
    o_ref[...] = x * lax.rsqrt(jnp.mean(x * x, axis=-1, keepdims=True) + RMS_EPS) * gf_ref[...]


def _combine(pos, x1, mod, gf, yb, *, seq, tm):
    n, d = x1.shape
    grid_spec = pltpu.PrefetchScalarGridSpec(
        num_scalar_prefetch=1,
        grid=(n // tm,),
        in_specs=[pl.BlockSpec((tm, d), lambda i, pos: (i, 0)),
                  pl.BlockSpec((8, d), lambda i, pos: (0, 5)),
                  pl.BlockSpec((1, d), lambda i, pos: (0, 0)),
                  pl.BlockSpec(memory_space=pl.ANY)],
        out_specs=pl.BlockSpec((tm, d), lambda i, pos: (i, 0)),
        scratch_shapes=[pltpu.VMEM((2, TOP_K * tm, d), F32), pltpu.SemaphoreType.DMA((2,))],
    )
    return pl.pallas_call(
        functools.partial(_combine_kernel, tm=tm, tiles_per_batch=seq // tm),
        grid_spec=grid_spec,
        out_shape=jax.ShapeDtypeStruct((n, d), F32),
        compiler_params=_cparams(("arbitrary",)),
        name="combine",
    )(pos, x1, mod, gf, yb)


def _dispatch(ids, n_experts):
    n = ids.shape[0]
    a = n * TOP_K
    expert_id = ids.reshape(a)
    order = jnp.argsort(expert_id)
    e_sorted = expert_id[order]
    counts = jnp.zeros((n_experts,), jnp.int32).at[expert_id].add(1)
    padded = (counts + MOE_BLOCK - 1) // MOE_BLOCK * MOE_BLOCK
    padded_end = jnp.cumsum(padded)
    dest = ((padded_end - padded)[e_sorted] + jnp.arange(a, dtype=jnp.int32)
            - (jnp.cumsum(counts) - counts)[e_sorted])
    p_rows = (a + n_experts * (MOE_BLOCK - 1) + MOE_BLOCK - 1) // MOE_BLOCK * MOE_BLOCK
    n_blocks = p_rows // MOE_BLOCK
    pos = jnp.zeros((a,), jnp.int32).at[order].set(dest)
    block_expert = jnp.minimum(
        jnp.searchsorted(padded_end, jnp.arange(n_blocks, dtype=jnp.int32) * MOE_BLOCK, side="right"),
        n_experts - 1).astype(jnp.int32)
    return pos, dest, order, block_expert, p_rows


def _pick(n, pref):
    t = min(n, pref)
    while n % t:
        t //= 2
    return t


def kernel(x, c, w_ada, b_ada, norm1_g, w_in, conv_w, gdn_a_log, gdn_dt_bias, gdn_onorm_g, rwkv_mu,
           rwkv_w0, rwkv_w_up, rwkv_a0, rwkv_a_up, rwkv_g_up, rwkv_k_k, rwkv_k_a, rwkv_r_k, rwkv_ln_w,
           rwkv_ln_b, w_gdn_o, w_rwkv_o, w_out, norm2_g, w_group, b_group, w_expert, b_expert, w1, w3,
           w2, norm_f_g):
    batch, seq, d = x.shape
    n = batch * seq
    depth = w_ada.shape[0]
    assert depth == 1, "the final norm is fused into the last layer's combine"
    assert batch <= 8 and seq % CHUNK == 0 and d % 512 == 0
    x2 = x.reshape(n, d)
    c_pad = _pad_rows(c, 8)
    tm_big = _pick(seq, 1024)
    tm_row = _pick(seq, 256)
    tn = 512
    for l in range(depth):
        p = {"norm1_g": norm1_g[l], "conv_w": conv_w[l], "gdn_a_log": gdn_a_log[l],
             "gdn_dt_bias": gdn_dt_bias[l], "gdn_onorm_g": gdn_onorm_g[l], "rwkv_mu": rwkv_mu[l],
             "rwkv_w0": rwkv_w0[l], "rwkv_w_up": rwkv_w_up[l], "rwkv_a0": rwkv_a0[l],
             "rwkv_a_up": rwkv_a_up[l], "rwkv_g_up": rwkv_g_up[l], "rwkv_k_k": rwkv_k_k[l],
             "rwkv_k_a": rwkv_k_a[l], "rwkv_r_k": rwkv_r_k[l], "rwkv_ln_w": rwkv_ln_w[l],
             "rwkv_ln_b": rwkv_ln_b[l]}
        lay = _make_layout(p, tn)
        mod = _ada_mod(c_pad, w_ada[l], b_ada[l], tn)

        h1 = _norm1(x2, norm1_g[l].reshape(1, d), mod, batch=batch, seq=seq, tm=tm_row)
        z = _matmul(h1, lay.pack_cols(w_in[l]).astype(BF16), tm_big, tn, F32)
        o_gdn = _gdn_from_packed(z, p, lay, batch, seq)
        y_rwkv = _rwkv_from_packed(z, p, lay, batch, seq)
        m = _merge(o_gdn, y_rwkv, w_gdn_o[l].astype(BF16), w_rwkv_o[l].astype(BF16), z,
                   cb_gates=lay.cb_gates, tm=tm_big, tn=tn)
        x1 = _outproj(m, w_out[l].astype(BF16), x2, mod, seq=seq, tm=tm_big, tn=tn)

        n_groups, n_experts = w_group.shape[-1], w_expert.shape[-1]
        assert n_groups + n_experts <= LANE
        wr = jnp.concatenate([w_group[l], w_expert[l],
                              jnp.zeros((d, LANE - n_groups - n_experts), F32)], axis=1).astype(BF16)
        br = jnp.concatenate([b_group[l], b_expert[l],
                              jnp.zeros((LANE - n_groups - n_experts,), F32)]).reshape(1, LANE)
        h2, ids, wts = _router(x1, norm2_g[l].reshape(1, d), mod, wr, br, batch=batch, seq=seq,
                               tm=tm_row, n_groups=n_groups, per_group=n_experts // n_groups)
        pos, dest, order, block_expert, p_rows = _dispatch(ids[:, :TOP_K], n_experts)
        token_id = jnp.repeat(jnp.arange(n, dtype=jnp.int32), TOP_K)
        buf_tok = jnp.zeros((p_rows,), jnp.int32).at[dest].set(token_id[order])
        buf_w = jnp.zeros((p_rows,), F32).at[dest].set(wts[:, :TOP_K].reshape(n * TOP_K)[order])
        yb = _experts(h2, w1[l].astype(BF16), w3[l].astype(BF16), w2[l].astype(BF16), block_expert,
                      buf_tok, buf_w.reshape(p_rows, 1))
        tm_c = _pick(seq, 128)
        pos_tiles = pos.reshape(n // tm_c, tm_c, TOP_K).transpose(0, 2, 1).reshape(n * TOP_K)
        x2 = _combine(pos_tiles, x1, mod, norm_f_g.reshape(1, d), yb, seq=seq, tm=tm_c)
    return x2.reshape(batch, seq, d)
```

```python
import functools
import math

import jax
import jax.numpy as jnp
from jax import lax
from jax.experimental import pallas as pl
from jax.experimental.pallas import tpu as pltpu

F32 = jnp.float32
BF16 = jnp.bfloat16

LANE = 128
CHUNK = 64
CONV_WIDTH = 4
RMS_EPS = 1e-6
RWKV_GN_EPS = 64e-5
MOE_BLOCK = 128
TOP_K = 2
VMEM_LIMIT = 56 * 1024 * 1024


def _cparams(sem):
    return pltpu.CompilerParams(dimension_semantics=sem, vmem_limit_bytes=VMEM_LIMIT)


def _dot(a, b):
    return jnp.dot(a.astype(BF16), b.astype(BF16), preferred_element_type=F32)


def _dot_nt(a, b):
    return lax.dot_general(a.astype(BF16), b.astype(BF16), (((1,), (1,)), ((), ())),
                           preferred_element_type=F32)


def _dot_tn(a, b):
    return lax.dot_general(a.astype(BF16), b.astype(BF16), (((0,), (0,)), ((), ())),
                           preferred_element_type=F32)


def _split(x, terms):
    pieces = []
    for _ in range(terms - 1):
        hi = x.astype(BF16)
        pieces.append(hi)
        x = x - hi.astype(F32)
    pieces.append(x.astype(BF16))
    return pieces


def _dot_sel(sel, x, terms, left=True):
    out = None
    for piece in _split(x, terms):
        ops = (sel, piece) if left else (piece, sel)
        d = jnp.dot(*ops, preferred_element_type=F32)
        out = d if out is None else out + d
    return out


def _sigmoid(x):
    return 1.0 / (1.0 + jnp.exp(-x))


def _silu(x):
    return x * _sigmoid(x)


def _softplus(x):
    return jnp.maximum(x, 0.0) + jnp.log(1.0 + jnp.exp(-jnp.abs(x)))


def _inv_unit_lower(L, n, top):
    r = lax.broadcasted_iota(jnp.int32, (n, n), 0)
    c = lax.broadcasted_iota(jnp.int32, (n, n), 1)
    eye = (r == c).astype(F32)
    same16 = (r // 16) == (c // 16)
    Ld = jnp.where(same16, L, 0.0)
    X = eye - Ld
    P = _dot(Ld, Ld)
    yield
    for _ in range(2):
        X = X + _dot(X, P)
        P = _dot(P, P)
        yield
    X = X + _dot(X, P)
    yield
    bs = 32
    while bs <= top:
        inner = (r // (bs // 2)) == (c // (bs // 2))
        outer = (r // bs) == (c // bs)
        Lo = jnp.where(jnp.logical_and(outer, jnp.logical_not(inner)), L, 0.0)
        Y = _dot(Lo, X)
        yield
        X = X - _dot(X, Y)
        yield
        bs *= 2
    return X


def _round_robin(gens):
    gens = list(gens)
    while gens:
        for g in list(gens):
            try:
                more = next(g)
            except StopIteration:
                gens.remove(g)
                continue
            if more:
                gens.extend(more)


def _mm_kernel(a_ref, w_ref, o_ref):
    o_ref[...] = jnp.dot(a_ref[...], w_ref[...], preferred_element_type=F32).astype(o_ref.dtype)


def _matmul(a, w, tm, tn, out_dtype):
    M, K = a.shape
    Nn = w.shape[1]
    return pl.pallas_call(
        _mm_kernel,
        grid=(M // tm, Nn // tn),
        in_specs=[pl.BlockSpec((tm, K), lambda i, j: (i, 0)),
                  pl.BlockSpec((K, tn), lambda i, j: (0, j))],
        out_specs=pl.BlockSpec((tm, tn), lambda i, j: (i, j)),
        out_shape=jax.ShapeDtypeStruct((M, Nn), out_dtype),
        compiler_params=_cparams(("parallel", "arbitrary")),
        name="matmul",
    )(a, w)


def _gdn_kernel(zq_ref, zk_ref, zv_ref, gz_ref, sm_ref, cwq_ref, cwk_ref, cwv_ref, alog_ref, dtb_ref,
                on_ref, o_ref, s_ref, xs_ref, pw_ref, pu_ref, pattn_ref, *, tb, nt, a_lane0, b_lane0,
                q_scale):
    C = CHUNK
    h = pl.program_id(1)
    t = pl.program_id(2)

    @pl.when(t == 0)
    def _():
        s_ref[...] = jnp.zeros_like(s_ref)
        xs_ref[:, 0:8, :] = jnp.zeros((3, 8, LANE), F32)

    def conv_silu(idx, z_ref, cw_ref):
        xs_ref[idx, 8:8 + tb, :] = z_ref[...]
        cw = cw_ref[...]
        acc = cw[CONV_WIDTH - 1:CONV_WIDTH, :] * xs_ref[idx, 8:8 + tb, :]
        for i in range(1, CONV_WIDTH):
            acc = acc + cw[CONV_WIDTH - 1 - i:CONV_WIDTH - i, :] * xs_ref[idx, 8 - i:8 - i + tb, :]
        xs_ref[idx, 0:8, :] = xs_ref[idx, tb:tb + 8, :]
        return _silu(acc)

    def l2n(x):
        return x * lax.rsqrt(jnp.sum(x * x, axis=-1, keepdims=True) + 1e-6)

    def phase_ab(out):
        q_all = l2n(conv_silu(0, zq_ref, cwq_ref)) * q_scale
        yield
        k_all = l2n(conv_silu(1, zk_ref, cwk_ref))
        yield
        v_all = conv_silu(2, zv_ref, cwv_ref)
        sm = sm_ref[...]
        lane = lax.broadcasted_iota(jnp.int32, (1, LANE), 1)
        g_full = -jnp.exp(alog_ref[...]) * _softplus(sm + dtb_ref[...])
        g_all = jnp.sum(jnp.where(lane == a_lane0 + h, g_full, 0.0), axis=1, keepdims=True)
        beta_all = jnp.sum(jnp.where(lane == b_lane0 + h, _sigmoid(sm), 0.0), axis=1, keepdims=True)
        yield
        r = lax.broadcasted_iota(jnp.int32, (tb, tb), 0)
        c = lax.broadcasted_iota(jnp.int32, (tb, tb), 1)
        same = (r // C) == (c // C)
        causal = jnp.logical_and(same, r >= c)
        strict = jnp.logical_and(same, r > c)
        gb = jnp.broadcast_to(g_all, (tb, tb))
        gc_col = _dot_sel(causal.astype(BF16), gb, 3)
        yield
        gc_row = _dot_sel(jnp.ones((tb, tb), BF16),
                          jnp.where(jnp.logical_and(same, r <= c), gb, 0.0), 3)
        yield
        g_tot = _dot_sel(same.astype(BF16), gb, 3)
        yield
        decay = jnp.where(causal, jnp.exp(jnp.minimum(gc_col - gc_row, 0.0)), 0.0)
        gc = gc_col[:, 0:1]
        e_gc = jnp.exp(gc)
        kb = k_all * beta_all
        sc = _dot_nt(jnp.concatenate([kb, q_all], axis=0), k_all)
        yield
        L = sc[:tb] * jnp.where(strict, decay, 0.0)
        T = yield from _inv_unit_lower(L, tb, C)
        uw = _dot(T, jnp.concatenate([v_all * beta_all, kb * e_gc], axis=1))
        yield
        out["w"] = uw[:, LANE:].astype(BF16)
        out["q_dec"] = (q_all * e_gc).astype(BF16)
        out["k_dec"] = (k_all * jnp.exp(g_tot[:, 0:1] - gc)).astype(BF16)
        out["u"] = uw[:, :LANE]
        out["e_gl"] = jnp.exp(g_tot[:, :LANE])
        out["gzn"] = on_ref[...] * _silu(gz_ref[...])
        out["attn"] = (sc[tb:] * decay).astype(BF16)

    wslot = lax.rem(t, 2)
    rslot = 1 - wslot

    def store(out):
        pw_ref[wslot, 0], pw_ref[wslot, 1], pw_ref[wslot, 2] = out["w"], out["q_dec"], out["k_dec"]
        pu_ref[wslot, 0], pu_ref[wslot, 1], pu_ref[wslot, 2] = out["u"], out["e_gl"], out["gzn"]
        pattn_ref[wslot] = out["attn"]

    def load():
        return dict(w=pw_ref[rslot, 0], q_dec=pw_ref[rslot, 1], k_dec=pw_ref[rslot, 2],
                    u=pu_ref[rslot, 0], e_gl=pu_ref[rslot, 1], gzn=pu_ref[rslot, 2],
                    attn=pattn_ref[rslot])

    def phase_c(p):
        S = s_ref[...]
        v_news, o_qs = [], []
        for ci in range(tb // C):
            sl = slice(ci * C, (ci + 1) * C)
            ws = _dot(jnp.concatenate([p["w"][sl], p["q_dec"][sl]], axis=0), S)
            yield
            v_new = p["u"][sl] - ws[:C]
            v_news.append(v_new)
            o_qs.append(ws[C:])
            S = S * p["e_gl"][ci * C:ci * C + 1, :] + _dot_tn(p["k_dec"][sl], v_new)
            yield
        s_ref[...] = S
        o = jnp.concatenate(o_qs, axis=0) + _dot(p["attn"], jnp.concatenate(v_news, axis=0))
        yield
        o = o * lax.rsqrt(jnp.mean(o * o, axis=-1, keepdims=True) + RMS_EPS) * p["gzn"]
        o_ref[...] = o.astype(o_ref.dtype)

    @pl.when(t == 0)
    def _():
        out = {}
        _round_robin([phase_ab(out)])
        store(out)

    @pl.when(jnp.logical_and(t > 0, t < nt))
    def _():
        prev, out = load(), {}
        _round_robin([phase_c(prev), phase_ab(out)])
        store(out)

    @pl.when(t == nt)
    def _():
        _round_robin([phase_c(load())])


def _gdn_branch(z, conv_w, alog_pad, dtb_pad, onorm_g, *, batch, seq, heads, cb_q, cb_gz, cb_sm,
                a_lane0, b_lane0, tb):
    n = batch * seq
    nt = seq // tb
    row_in = lambda b, t: b * nt + jnp.minimum(t, nt - 1)
    row_out = lambda b, t: b * nt + jnp.maximum(t - 1, 0)
    zspec = lambda cb: pl.BlockSpec((tb, LANE), lambda b, h, t: (row_in(b, t), cb + h))
    cwspec = lambda cb: pl.BlockSpec((CONV_WIDTH, LANE), lambda b, h, t: (0, cb + h))
    vec = pl.BlockSpec((1, LANE), lambda b, h, t: (0, 0))
    kern = functools.partial(_gdn_kernel, tb=tb, nt=nt, a_lane0=a_lane0, b_lane0=b_lane0,
                             q_scale=float(LANE) ** -0.5)
    return pl.pallas_call(
        kern,
        grid=(batch, heads, nt + 1),
        in_specs=[zspec(cb_q), zspec(cb_q + heads), zspec(cb_q + 2 * heads), zspec(cb_gz),
                  pl.BlockSpec((tb, LANE), lambda b, h, t: (row_in(b, t), cb_sm)),
                  cwspec(0), cwspec(heads), cwspec(2 * heads), vec, vec, vec],
        out_specs=pl.BlockSpec((tb, LANE), lambda b, h, t: (row_out(b, t), h)),
        out_shape=jax.ShapeDtypeStruct((n, heads * LANE), BF16),
        scratch_shapes=[pltpu.VMEM((LANE, LANE), F32), pltpu.VMEM((3, tb + 8, LANE), F32),
                        pltpu.VMEM((2, 3, tb, LANE), BF16), pltpu.VMEM((2, 3, tb, LANE), F32),
                        pltpu.VMEM((2, tb, tb), BF16)],
        compiler_params=_cparams(("parallel", "parallel", "arbitrary")),
        name="gdn",
    )(z, z, z, z, z, conv_w, conv_w, conv_w, alog_pad, dtb_pad, onorm_g)


_RW_ROWS = ("w0", "a0", "k_k", "k_a", "r_k", "ln_w", "ln_b", "mu_r", "mu_k", "mu_v")


def _rwkv_kernel(zr_ref, zk_ref, zv_ref, sm0_ref, sm1_ref, sm2_ref, sm3_ref, pv_ref, musm_ref,
                 wup_ref, aup_ref, gup_ref, o_ref, h_ref, c_ref, pwx_ref, pbs_ref, puo_ref, phk_ref,
                 parb_ref, pvec_ref, *, tb, nt, hd):
    C = CHUNK
    G2, SG = 2 * C, 4 * C
    n_groups = tb // G2
    t = pl.program_id(2)

    @pl.when(t == 0)
    def _():
        h_ref[...] = jnp.zeros_like(h_ref)
        c_ref[...] = jnp.zeros_like(c_ref)

    pv = pv_ref[...]
    prow = lambda name: pv[_RW_ROWS.index(name):_RW_ROWS.index(name) + 1, :]
    rr = lax.broadcasted_iota(jnp.int32, (LANE, LANE), 0)
    cc = lax.broadcasted_iota(jnp.int32, (LANE, LANE), 1)
    bd_ones = ((rr // hd) == (cc // hd)).astype(BF16)
    lane = lax.broadcasted_iota(jnp.int32, (1, LANE), 1)
    m0 = (lane < hd).astype(F32)
    m1 = 1.0 - m0

    def phase_a(a, out):
        musm = musm_ref[...]
        row0 = lax.broadcasted_iota(jnp.int32, (tb, LANE), 0) == 0

        def shift(idx, ref, mu):
            x = ref[...]
            prev = jnp.where(row0, c_ref[idx, 0:1, :], pltpu.roll(x, 1, 0))
            c_ref[idx, 0:1, :] = x[tb - 1:tb, :]
            return x + (prev - x) * mu

        r_all = shift(0, zr_ref, prow("mu_r"))
        k_all = shift(1, zk_ref, prow("mu_k"))
        v_all = shift(2, zv_ref, prow("mu_v"))
        yield
        wd = shift(3, sm0_ref, musm[0:1, 0:LANE])
        ad = shift(4, sm1_ref, musm[0:1, LANE:2 * LANE])
        gd0 = shift(5, sm2_ref, musm[0:1, 2 * LANE:3 * LANE])
        gd1 = shift(6, sm3_ref, musm[0:1, 3 * LANE:4 * LANE])
        w_lin = prow("w0") + _dot(jnp.tanh(wd), wup_ref[...])
        logw = -jnp.exp(-_softplus(-w_lin) - 0.5)
        yield
        alr = _sigmoid(prow("a0") + _dot(ad, aup_ref[...]))
        gate = _dot(_sigmoid(gd0), gup_ref[0:LANE, :]) + _dot(_sigmoid(gd1), gup_ref[LANE:2 * LANE, :])
        yield
        kraw = k_all * prow("k_k")
        kk = kraw * lax.rsqrt(_dot_sel(bd_ones, kraw * kraw, 2, left=False) + 1e-6)
        keff = k_all * (1.0 + (alr - 1.0) * prow("k_a"))
        bonus = _dot_sel(bd_ones, r_all * keff * prow("r_k"), 2, left=False) * v_all
        yield
        rt_i = lax.broadcasted_iota(jnp.int32, (tb, tb), 0)
        ct_i = lax.broadcasted_iota(jnp.int32, (tb, tb), 1)
        same_chunk = (rt_i // C) == (ct_i // C)
        cum = _dot_sel(jnp.logical_and(same_chunk, rt_i >= ct_i).astype(BF16), logw, 3)
        yield
        ctot = _dot_sel(same_chunk.astype(BF16), logw, 3)
        yield
        e_neg = jnp.exp(-cum)
        e_end = jnp.exp(ctot - cum)
        b_all = kk * alr
        a.update(v=v_all, rt=r_all * jnp.exp(cum), at=-kk * jnp.exp(cum - logw),
                 bt=b_all * e_neg, kt=keff * e_neg, bh=b_all * e_end, kh=keff * e_end)
        out.update(e_tot=jnp.exp(ctot), bonus=bonus, gate=gate)
        yield [phase_b(a, out, g) for g in range(n_groups)]

    def phase_b(a, out, g):
        r4 = lax.broadcasted_iota(jnp.int32, (SG, SG), 0)
        c4 = lax.broadcasted_iota(jnp.int32, (SG, SG), 1)
        same_unit = (r4 // C) == (c4 // C)
        strict = jnp.logical_and(same_unit, r4 > c4)
        incl = jnp.logical_and(same_unit, r4 >= c4)

        def halves(x):
            return x[g * G2:g * G2 + C], x[g * G2 + C:(g + 1) * G2]

        def stack(x):
            x0, x1 = halves(x)
            return jnp.concatenate([x0 * m0, x0 * m1, x1 * m0, x1 * m1], axis=0)

        def dup(x):
            x0, x1 = halves(x)
            return jnp.concatenate([x0, x0, x1, x1], axis=0)

        xa, xr = stack(a["at"]), stack(a["rt"])
        v_st, b_st, k_st = stack(a["v"]), stack(a["bh"]), stack(a["kh"])
        sc = _dot_nt(jnp.concatenate([xa, xr], axis=0),
                     jnp.concatenate([dup(a["bt"]), dup(a["kt"])], axis=0))
        yield
        T = yield from _inv_unit_lower(jnp.where(strict, -sc[:SG, :SG], 0.0), SG, hd)
        av = _dot(jnp.where(strict, sc[:SG, SG:], 0.0), v_st)
        wt = _dot(T, xa)
        yield
        ut = _dot(T, av)
        ork = _dot(jnp.where(incl, sc[SG:, SG:], 0.0), v_st)
        yield
        hk = [_dot_tn(v_st[i * G2:(i + 1) * G2], k_st[i * G2:(i + 1) * G2]) for i in range(2)]
        out[g] = dict(wt=wt.astype(BF16), xr=xr.astype(BF16), b_st=b_st.astype(BF16), ut=ut, ork=ork,
                      hk=hk, a_rb=jnp.where(incl, sc[SG:, :SG], 0.0).astype(BF16))

    ws = lax.rem(t, 2)
    rs = 1 - ws

    def store(out):
        for g in range(n_groups):
            o = out[g]
            pwx_ref[ws, g, 0], pwx_ref[ws, g, 1], pbs_ref[ws, g] = o["wt"], o["xr"], o["b_st"]
            puo_ref[ws, g, 0], puo_ref[ws, g, 1] = o["ut"], o["ork"]
            phk_ref[ws, 2 * g], phk_ref[ws, 2 * g + 1] = o["hk"]
            parb_ref[ws, g] = o["a_rb"]
        pvec_ref[ws, 0], pvec_ref[ws, 1], pvec_ref[ws, 2] = out["e_tot"], out["bonus"], out["gate"]

    def load():
        p = {g: dict(wt=pwx_ref[rs, g, 0], xr=pwx_ref[rs, g, 1], b_st=pbs_ref[rs, g],
                     ut=puo_ref[rs, g, 0], ork=puo_ref[rs, g, 1],
                     hk=[phk_ref[rs, 2 * g], phk_ref[rs, 2 * g + 1]], a_rb=parb_ref[rs, g])
             for g in range(n_groups)}
        p.update(e_tot=pvec_ref[rs, 0], bonus=pvec_ref[rs, 1], gate=pvec_ref[rs, 2])
        return p

    def phase_c(p):
        H = h_ref[...]
        us, xrs = [], []
        for ci in range(tb // C):
            gr = p[ci // 2]
            rows = slice((ci % 2) * G2, (ci % 2 + 1) * G2)
            xh = _dot_nt(jnp.concatenate([gr["wt"][rows], gr["xr"][rows]], axis=0), H)
            yield
            u_c = xh[:G2] + gr["ut"][rows]
            H = (H * p["e_tot"][ci * C:ci * C + 1, :] + _dot_tn(u_c, gr["b_st"][rows])
                 + gr["hk"][ci % 2])
            us.append(u_c)
            xrs.append(xh[G2:])
            yield
        h_ref[...] = H
        ys = []
        for g in range(n_groups):
            o_st = (jnp.concatenate(xrs[2 * g:2 * g + 2], axis=0)
                    + _dot(p[g]["a_rb"], jnp.concatenate(us[2 * g:2 * g + 2], axis=0)) + p[g]["ork"])
            ys += [o_st[0:C] + o_st[C:2 * C], o_st[2 * C:3 * C] + o_st[3 * C:4 * C]]
            yield
        y = jnp.concatenate(ys, axis=0)
        mu = _dot_sel(bd_ones, y, 2, left=False) * (1.0 / hd)
        yield
        yc = y - mu
        var = _dot_sel(bd_ones, yc * yc, 2, left=False) * (1.0 / hd)
        yield
        y = yc * lax.rsqrt(var + RWKV_GN_EPS) * prow("ln_w") + prow("ln_b") + p["bonus"]
        o_ref[...] = (y * p["gate"]).astype(o_ref.dtype)

    @pl.when(t == 0)
    def _():
        out = {}
        _round_robin([phase_a({}, out)])
        store(out)

    @pl.when(jnp.logical_and(t > 0, t < nt))
    def _():
        prev, out = load(), {}
        _round_robin([phase_c(prev), phase_a({}, out)])
        store(out)

    @pl.when(t == nt)
    def _():
        _round_robin([phase_c(load())])


def _rwkv_branch(z, pv, musm, wup, aup, gup, *, batch, seq, width, hd, cb_r, cb_sm, tb):
    n = batch * seq
    nt = seq // tb
    hp = width // LANE
    row_in = lambda b, t: b * nt + jnp.minimum(t, nt - 1)
    row_out = lambda b, t: b * nt + jnp.maximum(t - 1, 0)
    zspec = lambda cb: pl.BlockSpec((tb, LANE), lambda b, p, t: (row_in(b, t), cb + p))
    smspec = lambda i: pl.BlockSpec((tb, LANE), lambda b, p, t: (row_in(b, t), cb_sm + i))
    colblk = lambda rows: pl.BlockSpec((rows, LANE), lambda b, p, t: (0, p))
    kern = functools.partial(_rwkv_kernel, tb=tb, nt=nt, hd=hd)
    ng, sg = tb // (2 * CHUNK), 4 * CHUNK
    return pl.pallas_call(
        kern,
        grid=(batch, hp, nt + 1),
        in_specs=[zspec(cb_r), zspec(cb_r + hp), zspec(cb_r + 2 * hp),
                  smspec(0), smspec(1), smspec(2), smspec(3),
                  colblk(pv.shape[0]),
                  pl.BlockSpec(musm.shape, lambda b, p, t: (0, 0)),
                  colblk(LANE), colblk(LANE), colblk(2 * LANE)],
        out_specs=pl.BlockSpec((tb, LANE), lambda b, p, t: (row_out(b, t), p)),
        out_shape=jax.ShapeDtypeStruct((n, width), BF16),
        scratch_shapes=[pltpu.VMEM((LANE, LANE), F32), pltpu.VMEM((7, 8, LANE), F32),
                        pltpu.VMEM((2, ng, 2, sg, LANE), BF16), pltpu.VMEM((2, ng, sg, LANE), BF16),
                        pltpu.VMEM((2, ng, 2, sg, LANE), F32), pltpu.VMEM((2, 2 * ng, LANE, LANE), F32),
                        pltpu.VMEM((2, ng, sg, sg), BF16), pltpu.VMEM((2, 3, tb, LANE), F32)],
        compiler_params=_cparams(("parallel", "parallel", "arbitrary")),
        name="rwkv7",
    )(z, z, z, z, z, z, z, pv, musm, wup, aup, gup)


class _Layout:
    def __init__(self, d_model, gdn_heads, rwkv_width, decay_lora, aaa_lora, gate_lora, tn):
        self.d_model, self.gh, self.rw = d_model, gdn_heads, rwkv_width
        self.dl, self.al, self.gl = decay_lora, aaa_lora, gate_lora
        self.gw = gdn_heads * LANE
        assert decay_lora + 2 * gdn_heads <= LANE and aaa_lora <= LANE and gate_lora == 2 * LANE
        self.a_lane0 = decay_lora
        self.b_lane0 = decay_lora + gdn_heads
        self.cb_q = 0
        self.cb_gz = 3 * self.gw // LANE
        self.cb_r = self.cb_gz + self.gw // LANE
        self.cb_sm = self.cb_r + 3 * rwkv_width // LANE
        self.cb_gates = self.cb_sm + 4
        cols = (self.cb_gates) * LANE + 2 * d_model
        self.cols = -(-cols // tn) * tn
        self.off_z = 3 * self.gw
        self.off_a = self.off_z + self.gw
        self.off_b = self.off_a + gdn_heads
        self.off_rwkv = self.off_b + gdn_heads
        self.off_gates = self.off_rwkv + 3 * rwkv_width + decay_lora + aaa_lora + gate_lora

    def pack_cols(self, w):
        lead = w.shape[:-1]
        zeros = lambda n: jnp.zeros(lead + (n,), w.dtype)
        rw, dl, al, gl, gh = self.rw, self.dl, self.al, self.gl, self.gh
        o = self.off_rwkv + 3 * rw
        parts = [w[..., :self.off_a],
                 w[..., self.off_rwkv:o],
                 w[..., o:o + dl], w[..., self.off_a:self.off_b], w[..., self.off_b:self.off_rwkv],
                 zeros(LANE - dl - 2 * gh),
                 w[..., o + dl:o + dl + al], zeros(LANE - al),
                 w[..., o + dl + al:o + dl + al + gl],
                 w[..., self.off_gates:]]
        packed = jnp.concatenate(parts, axis=-1)
        return jnp.concatenate([packed, zeros(self.cols - packed.shape[-1])], axis=-1)


def _pad_rows(w, rows):
    return jnp.concatenate([w, jnp.zeros((rows - w.shape[0],) + w.shape[1:], w.dtype)], axis=0)


def _gdn_from_packed(z, p, lay, batch, seq, tb=256):
    lane_vec = lambda v, lane0: jnp.zeros((1, LANE), F32).at[0, lane0:lane0 + v.shape[0]].set(v)
    return _gdn_branch(z, p["conv_w"], lane_vec(p["gdn_a_log"], lay.a_lane0),
                       lane_vec(p["gdn_dt_bias"], lay.a_lane0), p["gdn_onorm_g"].reshape(1, LANE),
                       batch=batch, seq=seq, heads=lay.gh, cb_q=lay.cb_q, cb_gz=lay.cb_gz,
                       cb_sm=lay.cb_sm, a_lane0=lay.a_lane0, b_lane0=lay.b_lane0, tb=min(tb, seq))


def _rwkv_from_packed(z, p, lay, batch, seq, tb=256):
    rw = lay.rw
    hd = p["rwkv_r_k"].shape[-1]
    assert LANE % hd == 0 and hd * 2 == LANE
    mu = p["rwkv_mu"]
    rows = {"w0": p["rwkv_w0"], "a0": p["rwkv_a0"], "k_k": p["rwkv_k_k"], "k_a": p["rwkv_k_a"],
            "r_k": p["rwkv_r_k"].reshape(rw), "ln_w": p["rwkv_ln_w"], "ln_b": p["rwkv_ln_b"],
            "mu_r": mu[:rw], "mu_k": mu[rw:2 * rw], "mu_v": mu[2 * rw:3 * rw]}
    pv = _pad_rows(jnp.stack([rows[k] for k in _RW_ROWS]).astype(F32), 16)
    o = 3 * rw
    zl = lambda n: jnp.zeros((n,), F32)
    musm = jnp.concatenate([mu[o:o + lay.dl], zl(LANE - lay.dl), mu[o + lay.dl:o + lay.dl + lay.al],
                            zl(LANE - lay.al), mu[o + lay.dl + lay.al:]])
    musm = _pad_rows(musm.reshape(1, 4 * LANE), 8)
    wup = _pad_rows(p["rwkv_w_up"], LANE).astype(BF16)
    aup = _pad_rows(p["rwkv_a_up"], LANE).astype(BF16)
    gup = p["rwkv_g_up"].astype(BF16)
    return _rwkv_branch(z, pv, musm, wup, aup, gup, batch=batch, seq=seq, width=rw, hd=hd,
                        cb_r=lay.cb_r, cb_sm=lay.cb_sm, tb=min(tb, seq))


def _make_layout(p, tn):
    return _Layout(p["norm1_g"].shape[-1], p["gdn_a_log"].shape[-1], p["rwkv_w0"].shape[-1],
                   p["rwkv_w_up"].shape[0], p["rwkv_a_up"].shape[0], p["rwkv_g_up"].shape[0], tn)


def _ada_kernel(c_ref, w_ref, b_ref, o_ref):
    o_ref[...] = _dot(_silu(c_ref[...]), w_ref[...]) + b_ref[...]


def _ada_mod(c_pad, w_ada, b_ada, tn):
    d, cols = w_ada.shape
    return pl.pallas_call(
        _ada_kernel,
        grid=(cols // tn,),
        in_specs=[pl.BlockSpec((8, d), lambda j: (0, 0)),
                  pl.BlockSpec((d, tn), lambda j: (0, j)),
                  pl.BlockSpec((1, tn), lambda j: (0, j))],
        out_specs=pl.BlockSpec((8, tn), lambda j: (0, j)),
        out_shape=jax.ShapeDtypeStruct((8, cols), F32),
        compiler_params=_cparams(("arbitrary",)),
        name="ada_mod",
    )(c_pad, w_ada, b_ada.reshape(1, cols))


def _modnorm(x, g, sc, sh):
    y = x * lax.rsqrt(jnp.mean(x * x, axis=-1, keepdims=True) + RMS_EPS)
    return y * g * (1.0 + sc) + sh


def _norm1_kernel(x_ref, g_ref, sh_ref, sc_ref, o_ref):
    b = pl.program_id(0)
    h = _modnorm(x_ref[...], g_ref[...], sc_ref[pl.ds(b, 1), :], sh_ref[pl.ds(b, 1), :])
    o_ref[...] = h.astype(o_ref.dtype)


def _norm1(x2, g, mod, *, batch, seq, tm):
    n, d = x2.shape
    nt = seq // tm
    return pl.pallas_call(
        _norm1_kernel,
        grid=(batch, nt),
        in_specs=[pl.BlockSpec((tm, d), lambda b, t: (b * nt + t, 0)),
                  pl.BlockSpec((1, d), lambda b, t: (0, 0)),
                  pl.BlockSpec((8, d), lambda b, t: (0, 0)),
                  pl.BlockSpec((8, d), lambda b, t: (0, 1))],
        out_specs=pl.BlockSpec((tm, d), lambda b, t: (b * nt + t, 0)),
        out_shape=jax.ShapeDtypeStruct((n, d), BF16),
        compiler_params=_cparams(("parallel", "arbitrary")),
        name="norm1",
    )(x2, g, mod, mod)


def _merge_kernel(o_ref, y_ref, wa_ref, wb_ref, ga_ref, gb_ref, m_ref):
    ha = jnp.dot(o_ref[...], wa_ref[...], preferred_element_type=F32)
    hb = jnp.dot(y_ref[...], wb_ref[...], preferred_element_type=F32)
    m_ref[...] = (_sigmoid(ga_ref[...]) * ha + _sigmoid(gb_ref[...]) * hb).astype(m_ref.dtype)


def _merge(o, y, wa, wb, z, *, cb_gates, tm, tn):
    n, d = o.shape[0], wa.shape[1]
    tn = math.gcd(tn, cb_gates * LANE)
    gb0 = cb_gates * LANE // tn
    return pl.pallas_call(
        _merge_kernel,
        grid=(n // tm, d // tn),
        in_specs=[pl.BlockSpec((tm, o.shape[1]), lambda i, j: (i, 0)),
                  pl.BlockSpec((tm, y.shape[1]), lambda i, j: (i, 0)),
                  pl.BlockSpec((wa.shape[0], tn), lambda i, j: (0, j)),
                  pl.BlockSpec((wb.shape[0], tn), lambda i, j: (0, j)),
                  pl.BlockSpec((tm, tn), lambda i, j: (i, gb0 + j)),
                  pl.BlockSpec((tm, tn), lambda i, j: (i, gb0 + d // tn + j))],
        out_specs=pl.BlockSpec((tm, tn), lambda i, j: (i, j)),
        out_shape=jax.ShapeDtypeStruct((n, d), BF16),
        compiler_params=_cparams(("parallel", "arbitrary")),
        name="merge",
    )(o, y, wa, wb, z, z)


def _outproj_kernel(m_ref, w_ref, x_ref, g_ref, o_ref, *, tiles_per_batch):
    b = pl.program_id(0) // tiles_per_batch
    y = jnp.dot(m_ref[...], w_ref[...], preferred_element_type=F32)
    o_ref[...] = x_ref[...] + g_ref[pl.ds(b, 1), :] * y


def _outproj(m, w, x2, mod, *, seq, tm, tn):
    n, d = x2.shape
    gcol = 2 * d // tn
    return pl.pallas_call(
        functools.partial(_outproj_kernel, tiles_per_batch=seq // tm),
        grid=(n // tm, d // tn),
        in_specs=[pl.BlockSpec((tm, d), lambda i, j: (i, 0)),
                  pl.BlockSpec((d, tn), lambda i, j: (0, j)),
                  pl.BlockSpec((tm, tn), lambda i, j: (i, j)),
                  pl.BlockSpec((8, tn), lambda i, j: (0, gcol + j))],
        out_specs=pl.BlockSpec((tm, tn), lambda i, j: (i, j)),
        out_shape=jax.ShapeDtypeStruct((n, d), F32),
        compiler_params=_cparams(("parallel", "arbitrary")),
        name="outproj",
    )(m, w, x2, mod)


def _router_kernel(x_ref, g_ref, sh_ref, sc_ref, wr_ref, br_ref, h_ref, id_ref, wt_ref, *,
                   n_groups, per_group):
    b = pl.program_id(0)
    h = _modnorm(x_ref[...], g_ref[...], sc_ref[pl.ds(b, 1), :], sh_ref[pl.ds(b, 1), :])
    h_ref[...] = h
    logits = _dot(h, wr_ref[...]) + br_ref[...]
    tm = logits.shape[0]
    lane = lax.broadcasted_iota(jnp.int32, (tm, LANE), 1)
    neg = jnp.float32(-jnp.inf)
    big = jnp.int32(LANE)

    def first_argmax(vals, mask):
        vm = jnp.where(mask, vals, neg)
        mx = jnp.max(vm, axis=1, keepdims=True)
        idx = jnp.min(jnp.where(jnp.logical_and(mask, vm == mx), lane, big), axis=1, keepdims=True)
        return mx, idx

    gmask = lane < n_groups
    gmax, g_sel = first_argmax(logits, gmask)
    g_prob = 1.0 / jnp.sum(jnp.where(gmask, jnp.exp(logits - gmax), 0.0), axis=1, keepdims=True)
    e_lane = lane - n_groups
    emask = jnp.logical_and(e_lane >= g_sel * per_group, e_lane < (g_sel + 1) * per_group)
    emax = jnp.max(jnp.where(emask, logits, neg), axis=1, keepdims=True)
    ex = jnp.where(emask, jnp.exp(logits - emax), 0.0)
    probs = ex / jnp.sum(ex, axis=1, keepdims=True)
    p0, i0 = first_argmax(probs, emask)
    p1, i1 = first_argmax(probs, jnp.logical_and(emask, lane != i0))
    den = p0 + p1
    id_ref[...] = jnp.where(lane == 0, i0 - n_groups, jnp.where(lane == 1, i1 - n_groups, 0))
    wt_ref[...] = jnp.where(lane == 0, g_prob * (p0 / den), jnp.where(lane == 1, g_prob * (p1 / den), 0.0))


def _router(x1, g, mod, wr, br, *, batch, seq, tm, n_groups, per_group):
    n, d = x1.shape
    nt = seq // tm
    rowblk = lambda w: pl.BlockSpec((tm, w), lambda b, t: (b * nt + t, 0))
    return pl.pallas_call(
        functools.partial(_router_kernel, n_groups=n_groups, per_group=per_group),
        grid=(batch, nt),
        in_specs=[rowblk(d),
                  pl.BlockSpec((1, d), lambda b, t: (0, 0)),
                  pl.BlockSpec((8, d), lambda b, t: (0, 3)),
                  pl.BlockSpec((8, d), lambda b, t: (0, 4)),
                  pl.BlockSpec((d, LANE), lambda b, t: (0, 0)),
                  pl.BlockSpec((1, LANE), lambda b, t: (0, 0))],
        out_specs=[rowblk(d), rowblk(LANE), rowblk(LANE)],
        out_shape=[jax.ShapeDtypeStruct((n, d), F32), jax.ShapeDtypeStruct((n, LANE), jnp.int32),
                   jax.ShapeDtypeStruct((n, LANE), F32)],
        compiler_params=_cparams(("parallel", "arbitrary")),
        name="router",
    )(x1, g, mod, mod, wr, br)


def _expert_kernel(be_ref, tok_ref, h_hbm, w1_ref, w3_ref, w2_ref, bw_ref, o_ref, xbuf, sem):
    i = pl.program_id(0)
    nb = pl.num_programs(0)

    def row_copy(blk, slot, r):
        tok = tok_ref[blk * MOE_BLOCK + r]
        return pltpu.make_async_copy(h_hbm.at[pl.ds(tok, 1), :], xbuf.at[slot, pl.ds(r, 1), :],
                                     sem.at[slot])

    def start_block(blk, slot):
        def body(r, carry):
            row_copy(blk, slot, r).start()
            return carry
        lax.fori_loop(0, MOE_BLOCK, body, 0)

    @pl.when(i == 0)
    def _():
        start_block(0, 0)

    @pl.when(i + 1 < nb)
    def _():
        start_block(i + 1, (i + 1) % 2)

    slot = i % 2

    def wait_body(r, carry):
        row_copy(i, slot, r).wait()
        return carry
    lax.fori_loop(0, MOE_BLOCK, wait_body, 0)

    x = xbuf[slot]
    a = _dot(x, w1_ref[...])
    g = _dot(x, w3_ref[...])
    y = _dot(_silu(a) * g, w2_ref[...])
    o_ref[...] = y * bw_ref[...]


def _experts(h2, w1, w3, w2, block_expert, buf_tok, buf_w):
    d = h2.shape[1]
    de = w1.shape[2]
    p_rows = buf_tok.shape[0]
    wspec = lambda shape: pl.BlockSpec((None,) + shape, lambda i, be, tok: (be[i], 0, 0))
    grid_spec = pltpu.PrefetchScalarGridSpec(
        num_scalar_prefetch=2,
        grid=(p_rows // MOE_BLOCK,),
        in_specs=[pl.BlockSpec(memory_space=pl.ANY),
                  wspec((d, de)), wspec((d, de)), wspec((de, d)),
                  pl.BlockSpec((MOE_BLOCK, 1), lambda i, be, tok: (i, 0))],
        out_specs=pl.BlockSpec((MOE_BLOCK, d), lambda i, be, tok: (i, 0)),
        scratch_shapes=[pltpu.VMEM((2, MOE_BLOCK, d), F32), pltpu.SemaphoreType.DMA((2,))],
    )
    return pl.pallas_call(
        _expert_kernel,
        grid_spec=grid_spec,
        out_shape=jax.ShapeDtypeStruct((p_rows, d), F32),
        compiler_params=_cparams(("arbitrary",)),
        name="experts",
    )(block_expert, buf_tok, h2, w1, w3, w2, buf_w)


def _combine_kernel(pos_ref, x_ref, g_ref, gf_ref, y_hbm, o_ref, ybuf, sem, *, tm, tiles_per_batch):
    i = pl.program_id(0)
    nb = pl.num_programs(0)
    rows = TOP_K * tm

    def row_copy(blk, slot, r):
        return pltpu.make_async_copy(y_hbm.at[pl.ds(pos_ref[blk * rows + r], 1), :],
                                     ybuf.at[slot, pl.ds(r, 1), :], sem.at[slot])

    def start_block(blk, slot):
        def body(r, carry):
            row_copy(blk, slot, r).start()
            return carry
        lax.fori_loop(0, rows, body, 0)

    @pl.when(i == 0)
    def _():
        start_block(0, 0)

    @pl.when(i + 1 < nb)
    def _():
        start_block(i + 1, (i + 1) % 2)

    slot = i % 2

    def wait_body(r, carry):
        row_copy(i, slot, r).wait()
        return carry
    lax.fori_loop(0, rows, wait_body, 0)

    b = i // tiles_per_batch
    moe = ybuf[slot, 0:tm, :] + ybuf[slot, tm:rows, :]
    x = x_ref[...] + g_ref[pl.ds(b, 1), :] * moe
    o_ref[...] = x * lax.rsqrt(jnp.mean(x * x, axis=-1, keepdims=True) + RMS_EPS) * gf_ref[...]


def _combine(pos, x1, mod, gf, yb, *, seq, tm):
    n, d = x1.shape
    grid_spec = pltpu.PrefetchScalarGridSpec(
        num_scalar_prefetch=1,
        grid=(n // tm,),
        in_specs=[pl.BlockSpec((tm, d), lambda i, pos: (i, 0)),
                  pl.BlockSpec((8, d), lambda i, pos: (0, 5)),
                  pl.BlockSpec((1, d), lambda i, pos: (0, 0)),
                  pl.BlockSpec(memory_space=pl.ANY)],
        out_specs=pl.BlockSpec((tm, d), lambda i, pos: (i, 0)),
        scratch_shapes=[pltpu.VMEM((2, TOP_K * tm, d), F32), pltpu.SemaphoreType.DMA((2,))],
    )
    return pl.pallas_call(
        functools.partial(_combine_kernel, tm=tm, tiles_per_batch=seq // tm),
        grid_spec=grid_spec,
        out_shape=jax.ShapeDtypeStruct((n, d), F32),
        compiler_params=_cparams(("arbitrary",)),
        name="combine",
    )(pos, x1, mod, gf, yb)


def _dispatch(ids, n_experts):
    n = ids.shape[0]
    a = n * TOP_K
    expert_id = ids.reshape(a)
    order = jnp.argsort(expert_id)
    e_sorted = expert_id[order]
    counts = jnp.zeros((n_experts,), jnp.int32).at[expert_id].add(1)
    padded = (counts + MOE_BLOCK - 1) // MOE_BLOCK * MOE_BLOCK
    padded_end = jnp.cumsum(padded)
    dest = ((padded_end - padded)[e_sorted] + jnp.arange(a, dtype=jnp.int32)
            - (jnp.cumsum(counts) - counts)[e_sorted])
    p_rows = (a + n_experts * (MOE_BLOCK - 1) + MOE_BLOCK - 1) // MOE_BLOCK * MOE_BLOCK
    n_blocks = p_rows // MOE_BLOCK
    pos = jnp.zeros((a,), jnp.int32).at[order].set(dest)
    block_expert = jnp.minimum(
        jnp.searchsorted(padded_end, jnp.arange(n_blocks, dtype=jnp.int32) * MOE_BLOCK, side="right"),
        n_experts - 1).astype(jnp.int32)
    return pos, dest, order, block_expert, p_rows


def _pick(n, pref):
    t = min(n, pref)
    while n % t:
        t //= 2
    return t


def kernel(x, c, w_ada, b_ada, norm1_g, w_in, conv_w, gdn_a_log, gdn_dt_bias, gdn_onorm_g, rwkv_mu,
           rwkv_w0, rwkv_w_up, rwkv_a0, rwkv_a_up, rwkv_g_up, rwkv_k_k, rwkv_k_a, rwkv_r_k, rwkv_ln_w,
           rwkv_ln_b, w_gdn_o, w_rwkv_o, w_out, norm2_g, w_group, b_group, w_expert, b_expert, w1, w3,
           w2, norm_f_g):
    batch, seq, d = x.shape
    n = batch * seq
    depth = w_ada.shape[0]
    assert depth == 1, "the final norm is fused into the last layer's combine"
    assert batch <= 8 and seq % CHUNK == 0 and d % 512 == 0
    x2 = x.reshape(n, d)
    c_pad = _pad_rows(c, 8)
    tm_big = _pick(seq, 1024)
    tm_row = _pick(seq, 256)
    tn = 512
    for l in range(depth):
        p = {"norm1_g": norm1_g[l], "conv_w": conv_w[l], "gdn_a_log": gdn_a_log[l],
             "gdn_dt_bias": gdn_dt_bias[l], "gdn_onorm_g": gdn_onorm_g[l], "rwkv_mu": rwkv_mu[l],
             "rwkv_w0": rwkv_w0[l], "rwkv_w_up": rwkv_w_up[l], "rwkv_a0": rwkv_a0[l],
             "rwkv_a_up": rwkv_a_up[l], "rwkv_g_up": rwkv_g_up[l], "rwkv_k_k": rwkv_k_k[l],
             "rwkv_k_a": rwkv_k_a[l], "rwkv_r_k": rwkv_r_k[l], "rwkv_ln_w": rwkv_ln_w[l],
             "rwkv_ln_b": rwkv_ln_b[l]}
        lay = _make_layout(p, tn)
        mod = _ada_mod(c_pad, w_ada[l], b_ada[l], tn)

        h1 = _norm1(x2, norm1_g[l].reshape(1, d), mod, batch=batch, seq=seq, tm=tm_row)
        z = _matmul(h1, lay.pack_cols(w_in[l]).astype(BF16), tm_big, tn, F32)
        o_gdn = _gdn_from_packed(z, p, lay, batch, seq)
        y_rwkv = _rwkv_from_packed(z, p, lay, batch, seq)
        m = _merge(o_gdn, y_rwkv, w_gdn_o[l].astype(BF16), w_rwkv_o[l].astype(BF16), z,
                   cb_gates=lay.cb_gates, tm=tm_big, tn=tn)
        x1 = _outproj(m, w_out[l].astype(BF16), x2, mod, seq=seq, tm=tm_big, tn=tn)

        n_groups, n_experts = w_group.shape[-1], w_expert.shape[-1]
        assert n_groups + n_experts <= LANE
        wr = jnp.concatenate([w_group[l], w_expert[l],
                              jnp.zeros((d, LANE - n_groups - n_experts), F32)], axis=1).astype(BF16)
        br = jnp.concatenate([b_group[l], b_expert[l],
                              jnp.zeros((LANE - n_groups - n_experts,), F32)]).reshape(1, LANE)
        h2, ids, wts = _router(x1, norm2_g[l].reshape(1, d), mod, wr, br, batch=batch, seq=seq,
                               tm=tm_row, n_groups=n_groups, per_group=n_experts // n_groups)
        pos, dest, order, block_expert, p_rows = _dispatch(ids[:, :TOP_K], n_experts)
        token_id = jnp.repeat(jnp.arange(n, dtype=jnp.int32), TOP_K)
        buf_tok = jnp.zeros((p_rows,), jnp.int32).at[dest].set(token_id[order])
        buf_w = jnp.zeros((p_rows,), F32).at[dest].set(wts[:, :TOP_K].reshape(n * TOP_K)[order])
        yb = _experts(h2, w1[l].astype(BF16), w3[l].astype(BF16), w2[l].astype(BF16), block_expert,
                      buf_tok, buf_w.reshape(p_rows, 1))
        tm_c = _pick(seq, 128)
        pos_tiles = pos.reshape(n // tm_c, tm_c, TOP_K).transpose(0, 2, 1).reshape(n * TOP_K)
        x2 = _combine(pos_tiles, x1, mod, norm_f_g.reshape(1, d), yb, seq=seq, tm=tm_c)
    return x2.reshape(batch, seq, d)
```

```python
import functools
import math

import jax
import jax.numpy as jnp
from jax import lax
from jax.experimental import pallas as pl
from jax.experimental.pallas import tpu as pltpu

F32 = jnp.float32
BF16 = jnp.bfloat16

LANE = 128
CHUNK = 64
CONV_WIDTH = 4
RMS_EPS = 1e-6
RWKV_GN_EPS = 64e-5
MOE_BLOCK = 128
TOP_K = 2
DMA_UNROLL = 8
VMEM_LIMIT = 56 * 1024 * 1024


def _cparams(sem):
    return pltpu.CompilerParams(dimension_semantics=sem, vmem_limit_bytes=VMEM_LIMIT)


def _dot(a, b):
    return jnp.dot(a.astype(BF16), b.astype(BF16), preferred_element_type=F32)


def _dot_nt(a, b):
    return lax.dot_general(a.astype(BF16), b.astype(BF16), (((1,), (1,)), ((), ())),
                           preferred_element_type=F32)


def _dot_tn(a, b):
    return lax.dot_general(a.astype(BF16), b.astype(BF16), (((0,), (0,)), ((), ())),
                           preferred_element_type=F32)


def _split(x, terms):
    pieces = []
    for _ in range(terms - 1):
        hi = x.astype(BF16)
        pieces.append(hi)
        x = x - hi.astype(F32)
    pieces.append(x.astype(BF16))
    return pieces


def _dot_sel(sel, x, terms, left=True):
    out = None
    for piece in _split(x, terms):
        ops = (sel, piece) if left else (piece, sel)
        d = jnp.dot(*ops, preferred_element_type=F32)
        out = d if out is None else out + d
    return out


def _sigmoid(x):
    return 1.0 / (1.0 + jnp.exp(-x))


def _silu(x):
    return x * _sigmoid(x)


def _softplus(x):
    return jnp.maximum(x, 0.0) + jnp.log(1.0 + jnp.exp(-jnp.abs(x)))


def _inv_unit_lower(L, n, top):
    r = lax.broadcasted_iota(jnp.int32, (n, n), 0)
    c = lax.broadcasted_iota(jnp.int32, (n, n), 1)
    eye = (r == c).astype(F32)
    same16 = (r // 16) == (c // 16)
    Ld = jnp.where(same16, L, 0.0)
    X = eye - Ld
    P = _dot(Ld, Ld)
    yield
    for _ in range(2):
        X = X + _dot(X, P)
        P = _dot(P, P)
        yield
    X = X + _dot(X, P)
    yield
    bs = 32
    while bs <= top:
        inner = (r // (bs // 2)) == (c // (bs // 2))
        outer = (r // bs) == (c // bs)
        Lo = jnp.where(jnp.logical_and(outer, jnp.logical_not(inner)), L, 0.0)
        Y = _dot(Lo, X)
        yield
        X = X - _dot(X, Y)
        yield
        bs *= 2
    return X


def _round_robin(gens):
    gens = list(gens)
    while gens:
        for g in list(gens):
            try:
                more = next(g)
            except StopIteration:
                gens.remove(g)
                continue
            if more:
                gens.extend(more)


def _mm_kernel(a_ref, w_ref, o_ref, *, act):
    y = jnp.dot(a_ref[...], w_ref[...], preferred_element_type=F32)
    o_ref[...] = (act(y) if act else y).astype(o_ref.dtype)


def _matmul(a, w, tm, tn, out_dtype, act=None, name="matmul"):
    M, K = a.shape
    Nn = w.shape[1]
    tn = math.gcd(tn, Nn)
    return pl.pallas_call(
        functools.partial(_mm_kernel, act=act),
        grid=(M // tm, Nn // tn),
        in_specs=[pl.BlockSpec((tm, K), lambda i, j: (i, 0)),
                  pl.BlockSpec((K, tn), lambda i, j: (0, j))],
        out_specs=pl.BlockSpec((tm, tn), lambda i, j: (i, j)),
        out_shape=jax.ShapeDtypeStruct((M, Nn), out_dtype),
        compiler_params=_cparams(("parallel", "arbitrary")),
        name=name,
    )(a, w)


def _gdn_kernel(zq_ref, zk_ref, zv_ref, gz_ref, sm_ref, cwq_ref, cwk_ref, cwv_ref, alog_ref, dtb_ref,
                on_ref, o_ref, s_ref, xs_ref, pw_ref, pu_ref, pattn_ref, *, tb, nt, a_lane0, b_lane0,
                q_scale):
    C = CHUNK
    h = pl.program_id(1)
    t = pl.program_id(2)

    @pl.when(t == 0)
    def _():
        s_ref[...] = jnp.zeros_like(s_ref)
        xs_ref[:, 0:8, :] = jnp.zeros((3, 8, LANE), F32)

    def conv_silu(idx, z_ref, cw_ref):
        xs_ref[idx, 8:8 + tb, :] = z_ref[...]
        cw = cw_ref[...]
        acc = cw[CONV_WIDTH - 1:CONV_WIDTH, :] * xs_ref[idx, 8:8 + tb, :]
        for i in range(1, CONV_WIDTH):
            acc = acc + cw[CONV_WIDTH - 1 - i:CONV_WIDTH - i, :] * xs_ref[idx, 8 - i:8 - i + tb, :]
        xs_ref[idx, 0:8, :] = xs_ref[idx, tb:tb + 8, :]
        return _silu(acc)

    def l2n(x):
        return x * lax.rsqrt(jnp.sum(x * x, axis=-1, keepdims=True) + 1e-6)

    def phase_ab(out):
        q_all = l2n(conv_silu(0, zq_ref, cwq_ref)) * q_scale
        yield
        k_all = l2n(conv_silu(1, zk_ref, cwk_ref))
        yield
        v_all = conv_silu(2, zv_ref, cwv_ref)
        sm = sm_ref[...]
        lane = lax.broadcasted_iota(jnp.int32, (1, LANE), 1)
        g_full = -jnp.exp(alog_ref[...]) * _softplus(sm + dtb_ref[...])
        g_all = jnp.sum(jnp.where(lane == a_lane0 + h, g_full, 0.0), axis=1, keepdims=True)
        beta_all = jnp.sum(jnp.where(lane == b_lane0 + h, _sigmoid(sm), 0.0), axis=1, keepdims=True)
        yield
        r = lax.broadcasted_iota(jnp.int32, (tb, tb), 0)
        c = lax.broadcasted_iota(jnp.int32, (tb, tb), 1)
        same = (r // C) == (c // C)
        causal = jnp.logical_and(same, r >= c)
        strict = jnp.logical_and(same, r > c)
        gb = jnp.broadcast_to(g_all, (tb, tb))
        gc_col = _dot_sel(causal.astype(BF16), gb, 3)
        yield
        gc_row = _dot_sel(jnp.ones((tb, tb), BF16),
                          jnp.where(jnp.logical_and(same, r <= c), gb, 0.0), 3)
        yield
        g_tot = _dot_sel(same.astype(BF16), gb, 3)
        yield
        decay = jnp.where(causal, jnp.exp(jnp.minimum(gc_col - gc_row, 0.0)), 0.0)
        gc = gc_col[:, 0:1]
        e_gc = jnp.exp(gc)
        kb = k_all * beta_all
        sc = _dot_nt(jnp.concatenate([kb, q_all], axis=0), k_all)
        yield
        L = sc[:tb] * jnp.where(strict, decay, 0.0)
        T = yield from _inv_unit_lower(L, tb, C)
        uw = _dot(T, jnp.concatenate([v_all * beta_all, kb * e_gc], axis=1))
        yield
        out["w"] = uw[:, LANE:].astype(BF16)
        out["q_dec"] = (q_all * e_gc).astype(BF16)
        out["k_dec"] = (k_all * jnp.exp(g_tot[:, 0:1] - gc)).astype(BF16)
        out["u"] = uw[:, :LANE]
        out["e_gl"] = jnp.exp(g_tot[:, :LANE])
        out["gzn"] = on_ref[...] * _silu(gz_ref[...])
        out["attn"] = (sc[tb:] * decay).astype(BF16)

    wslot = lax.rem(t, 2)
    rslot = 1 - wslot

    def store(out):
        pw_ref[wslot, 0], pw_ref[wslot, 1], pw_ref[wslot, 2] = out["w"], out["q_dec"], out["k_dec"]
        pu_ref[wslot, 0], pu_ref[wslot, 1], pu_ref[wslot, 2] = out["u"], out["e_gl"], out["gzn"]
        pattn_ref[wslot] = out["attn"]

    def load():
        return dict(w=pw_ref[rslot, 0], q_dec=pw_ref[rslot, 1], k_dec=pw_ref[rslot, 2],
                    u=pu_ref[rslot, 0], e_gl=pu_ref[rslot, 1], gzn=pu_ref[rslot, 2],
                    attn=pattn_ref[rslot])

    def phase_c(p):
        S = s_ref[...]
        v_news, o_qs = [], []
        for ci in range(tb // C):
            sl = slice(ci * C, (ci + 1) * C)
            ws = _dot(jnp.concatenate([p["w"][sl], p["q_dec"][sl]], axis=0), S)
            yield
            v_new = p["u"][sl] - ws[:C]
            v_news.append(v_new)
            o_qs.append(ws[C:])
            S = S * p["e_gl"][ci * C:ci * C + 1, :] + _dot_tn(p["k_dec"][sl], v_new)
            yield
        s_ref[...] = S
        o = jnp.concatenate(o_qs, axis=0) + _dot(p["attn"], jnp.concatenate(v_news, axis=0))
        yield
        o = o * lax.rsqrt(jnp.mean(o * o, axis=-1, keepdims=True) + RMS_EPS) * p["gzn"]
        o_ref[...] = o.astype(o_ref.dtype)

    @pl.when(t == 0)
    def _():
        out = {}
        _round_robin([phase_ab(out)])
        store(out)

    @pl.when(jnp.logical_and(t > 0, t < nt))
    def _():
        prev, out = load(), {}
        _round_robin([phase_c(prev), phase_ab(out)])
        store(out)

    @pl.when(t == nt)
    def _():
        _round_robin([phase_c(load())])


def _gdn_branch(za, zb, conv_w, alog_pad, dtb_pad, onorm_g, *, batch, seq, heads, cb_q, cb_gz, cb_sm,
                a_lane0, b_lane0, tb):
    n = batch * seq
    nt = seq // tb
    row_in = lambda b, t: b * nt + jnp.minimum(t, nt - 1)
    row_out = lambda b, t: b * nt + jnp.maximum(t - 1, 0)
    zspec = lambda cb: pl.BlockSpec((tb, LANE), lambda b, h, t: (row_in(b, t), cb + h))
    cwspec = lambda cb: pl.BlockSpec((CONV_WIDTH, LANE), lambda b, h, t: (0, cb + h))
    vec = pl.BlockSpec((1, LANE), lambda b, h, t: (0, 0))
    kern = functools.partial(_gdn_kernel, tb=tb, nt=nt, a_lane0=a_lane0, b_lane0=b_lane0,
                             q_scale=float(LANE) ** -0.5)
    return pl.pallas_call(
        kern,
        grid=(batch, heads, nt + 1),
        in_specs=[zspec(cb_q), zspec(cb_q + heads), zspec(cb_q + 2 * heads), zspec(cb_gz),
                  pl.BlockSpec((tb, LANE), lambda b, h, t: (row_in(b, t), cb_sm)),
                  cwspec(0), cwspec(heads), cwspec(2 * heads), vec, vec, vec],
        out_specs=pl.BlockSpec((tb, LANE), lambda b, h, t: (row_out(b, t), h)),
        out_shape=jax.ShapeDtypeStruct((n, heads * LANE), BF16),
        scratch_shapes=[pltpu.VMEM((LANE, LANE), F32), pltpu.VMEM((3, tb + 8, LANE), F32),
                        pltpu.VMEM((2, 3, tb, LANE), BF16), pltpu.VMEM((2, 3, tb, LANE), F32),
                        pltpu.VMEM((2, tb, tb), BF16)],
        compiler_params=_cparams(("parallel", "parallel", "arbitrary")),
        name="gdn",
    )(za, za, za, za, zb, conv_w, conv_w, conv_w, alog_pad, dtb_pad, onorm_g)


_RW_ROWS = ("w0", "a0", "k_k", "k_a", "r_k", "ln_w", "ln_b", "mu_r", "mu_k", "mu_v")


def _rwkv_kernel(zr_ref, zk_ref, zv_ref, sm0_ref, sm1_ref, sm2_ref, sm3_ref, pv_ref, musm_ref,
                 wup_ref, aup_ref, gup_ref, o_ref, h_ref, c_ref, pwx_ref, pbs_ref, puo_ref, phk_ref,
                 parb_ref, pvec_ref, *, tb, nt, hd):
    C = CHUNK
    G2, SG = 2 * C, 4 * C
    n_groups = tb // G2
    t = pl.program_id(2)

    @pl.when(t == 0)
    def _():
        h_ref[...] = jnp.zeros_like(h_ref)
        c_ref[...] = jnp.zeros_like(c_ref)

    pv = pv_ref[...]
    prow = lambda name: pv[_RW_ROWS.index(name):_RW_ROWS.index(name) + 1, :]
    rr = lax.broadcasted_iota(jnp.int32, (LANE, LANE), 0)
    cc = lax.broadcasted_iota(jnp.int32, (LANE, LANE), 1)
    bd_ones = ((rr // hd) == (cc // hd)).astype(BF16)
    lane = lax.broadcasted_iota(jnp.int32, (1, LANE), 1)
    m0 = (lane < hd).astype(F32)
    m1 = 1.0 - m0

    def phase_a(a, out):
        musm = musm_ref[...]
        row0 = lax.broadcasted_iota(jnp.int32, (tb, LANE), 0) == 0

        def shift(idx, ref, mu):
            x = ref[...]
            prev = jnp.where(row0, c_ref[idx, 0:1, :], pltpu.roll(x, 1, 0))
            c_ref[idx, 0:1, :] = x[tb - 1:tb, :]
            return x + (prev - x) * mu

        r_all = shift(0, zr_ref, prow("mu_r"))
        k_all = shift(1, zk_ref, prow("mu_k"))
        v_all = shift(2, zv_ref, prow("mu_v"))
        yield
        wd = shift(3, sm0_ref, musm[0:1, 0:LANE])
        ad = shift(4, sm1_ref, musm[0:1, LANE:2 * LANE])
        gd0 = shift(5, sm2_ref, musm[0:1, 2 * LANE:3 * LANE])
        gd1 = shift(6, sm3_ref, musm[0:1, 3 * LANE:4 * LANE])
        w_lin = prow("w0") + _dot(jnp.tanh(wd), wup_ref[...])
        logw = -jnp.exp(-_softplus(-w_lin) - 0.5)
        yield
        alr = _sigmoid(prow("a0") + _dot(ad, aup_ref[...]))
        gate = _dot(_sigmoid(gd0), gup_ref[0:LANE, :]) + _dot(_sigmoid(gd1), gup_ref[LANE:2 * LANE, :])
        yield
        kraw = k_all * prow("k_k")
        kk = kraw * lax.rsqrt(_dot_sel(bd_ones, kraw * kraw, 2, left=False) + 1e-6)
        keff = k_all * (1.0 + (alr - 1.0) * prow("k_a"))
        bonus = _dot_sel(bd_ones, r_all * keff * prow("r_k"), 2, left=False) * v_all
        yield
        rt_i = lax.broadcasted_iota(jnp.int32, (tb, tb), 0)
        ct_i = lax.broadcasted_iota(jnp.int32, (tb, tb), 1)
        same_chunk = (rt_i // C) == (ct_i // C)
        cum = _dot_sel(jnp.logical_and(same_chunk, rt_i >= ct_i).astype(BF16), logw, 3)
        yield
        ctot = _dot_sel(same_chunk.astype(BF16), logw, 3)
        yield
        e_neg = jnp.exp(-cum)
        e_end = jnp.exp(ctot - cum)
        b_all = kk * alr
        a.update(v=v_all, rt=r_all * jnp.exp(cum), at=-kk * jnp.exp(cum - logw),
                 bt=b_all * e_neg, kt=keff * e_neg, bh=b_all * e_end, kh=keff * e_end)
        out.update(e_tot=jnp.exp(ctot), bonus=bonus, gate=gate)
        yield [phase_b(a, out, g) for g in range(n_groups)]

    def phase_b(a, out, g):
        r4 = lax.broadcasted_iota(jnp.int32, (SG, SG), 0)
        c4 = lax.broadcasted_iota(jnp.int32, (SG, SG), 1)
        same_unit = (r4 // C) == (c4 // C)
        strict = jnp.logical_and(same_unit, r4 > c4)
        incl = jnp.logical_and(same_unit, r4 >= c4)

        def halves(x):
            return x[g * G2:g * G2 + C], x[g * G2 + C:(g + 1) * G2]

        def stack(x):
            x0, x1 = halves(x)
            return jnp.concatenate([x0 * m0, x0 * m1, x1 * m0, x1 * m1], axis=0)

        def dup(x):
            x0, x1 = halves(x)
            return jnp.concatenate([x0, x0, x1, x1], axis=0)

        xa, xr = stack(a["at"]), stack(a["rt"])
        v_st, b_st, k_st = stack(a["v"]), stack(a["bh"]), stack(a["kh"])
        sc = _dot_nt(jnp.concatenate([xa, xr], axis=0),
                     jnp.concatenate([dup(a["bt"]), dup(a["kt"])], axis=0))
        yield
        T = yield from _inv_unit_lower(jnp.where(strict, -sc[:SG, :SG], 0.0), SG, hd)
        av = _dot(jnp.where(strict, sc[:SG, SG:], 0.0), v_st)
        wt = _dot(T, xa)
        yield
        ut = _dot(T, av)
        ork = _dot(jnp.where(incl, sc[SG:, SG:], 0.0), v_st)
        yield
        hk = [_dot_tn(v_st[i * G2:(i + 1) * G2], k_st[i * G2:(i + 1) * G2]) for i in range(2)]
        out[g] = dict(wt=wt.astype(BF16), xr=xr.astype(BF16), b_st=b_st.astype(BF16), ut=ut, ork=ork,
                      hk=hk, a_rb=jnp.where(incl, sc[SG:, :SG], 0.0).astype(BF16))

    ws = lax.rem(t, 2)
    rs = 1 - ws

    def store(out):
        for g in range(n_groups):
            o = out[g]
            pwx_ref[ws, g, 0], pwx_ref[ws, g, 1], pbs_ref[ws, g] = o["wt"], o["xr"], o["b_st"]
            puo_ref[ws, g, 0], puo_ref[ws, g, 1] = o["ut"], o["ork"]
            phk_ref[ws, 2 * g], phk_ref[ws, 2 * g + 1] = o["hk"]
            parb_ref[ws, g] = o["a_rb"]
        pvec_ref[ws, 0], pvec_ref[ws, 1], pvec_ref[ws, 2] = out["e_tot"], out["bonus"], out["gate"]

    def load():
        p = {g: dict(wt=pwx_ref[rs, g, 0], xr=pwx_ref[rs, g, 1], b_st=pbs_ref[rs, g],
                     ut=puo_ref[rs, g, 0], ork=puo_ref[rs, g, 1],
                     hk=[phk_ref[rs, 2 * g], phk_ref[rs, 2 * g + 1]], a_rb=parb_ref[rs, g])
             for g in range(n_groups)}
        p.update(e_tot=pvec_ref[rs, 0], bonus=pvec_ref[rs, 1], gate=pvec_ref[rs, 2])
        return p

    def phase_c(p):
        H = h_ref[...]
        us, xrs = [], []
        for ci in range(tb // C):
            gr = p[ci // 2]
            rows = slice((ci % 2) * G2, (ci % 2 + 1) * G2)
            xh = _dot_nt(jnp.concatenate([gr["wt"][rows], gr["xr"][rows]], axis=0), H)
            yield
            u_c = xh[:G2] + gr["ut"][rows]
            H = (H * p["e_tot"][ci * C:ci * C + 1, :] + _dot_tn(u_c, gr["b_st"][rows])
                 + gr["hk"][ci % 2])
            us.append(u_c)
            xrs.append(xh[G2:])
            yield
        h_ref[...] = H
        ys = []
        for g in range(n_groups):
            o_st = (jnp.concatenate(xrs[2 * g:2 * g + 2], axis=0)
                    + _dot(p[g]["a_rb"], jnp.concatenate(us[2 * g:2 * g + 2], axis=0)) + p[g]["ork"])
            ys += [o_st[0:C] + o_st[C:2 * C], o_st[2 * C:3 * C] + o_st[3 * C:4 * C]]
            yield
        y = jnp.concatenate(ys, axis=0)
        mu = _dot_sel(bd_ones, y, 2, left=False) * (1.0 / hd)
        yield
        yc = y - mu
        var = _dot_sel(bd_ones, yc * yc, 2, left=False) * (1.0 / hd)
        yield
        y = yc * lax.rsqrt(var + RWKV_GN_EPS) * prow("ln_w") + prow("ln_b") + p["bonus"]
        o_ref[...] = (y * p["gate"]).astype(o_ref.dtype)

    @pl.when(t == 0)
    def _():
        out = {}
        _round_robin([phase_a({}, out)])
        store(out)

    @pl.when(jnp.logical_and(t > 0, t < nt))
    def _():
        prev, out = load(), {}
        _round_robin([phase_c(prev), phase_a({}, out)])
        store(out)

    @pl.when(t == nt)
    def _():
        _round_robin([phase_c(load())])


def _rwkv_branch(z, pv, musm, wup, aup, gup, *, batch, seq, width, hd, cb_r, cb_sm, tb):
    n = batch * seq
    nt = seq // tb
    hp = width // LANE
    row_in = lambda b, t: b * nt + jnp.minimum(t, nt - 1)
    row_out = lambda b, t: b * nt + jnp.maximum(t - 1, 0)
    zspec = lambda cb: pl.BlockSpec((tb, LANE), lambda b, p, t: (row_in(b, t), cb + p))
    smspec = lambda i: pl.BlockSpec((tb, LANE), lambda b, p, t: (row_in(b, t), cb_sm + i))
    colblk = lambda rows: pl.BlockSpec((rows, LANE), lambda b, p, t: (0, p))
    kern = functools.partial(_rwkv_kernel, tb=tb, nt=nt, hd=hd)
    ng, sg = tb // (2 * CHUNK), 4 * CHUNK
    return pl.pallas_call(
        kern,
        grid=(batch, hp, nt + 1),
        in_specs=[zspec(cb_r), zspec(cb_r + hp), zspec(cb_r + 2 * hp),
                  smspec(0), smspec(1), smspec(2), smspec(3),
                  colblk(pv.shape[0]),
                  pl.BlockSpec(musm.shape, lambda b, p, t: (0, 0)),
                  colblk(LANE), colblk(LANE), colblk(2 * LANE)],
        out_specs=pl.BlockSpec((tb, LANE), lambda b, p, t: (row_out(b, t), p)),
        out_shape=jax.ShapeDtypeStruct((n, width), BF16),
        scratch_shapes=[pltpu.VMEM((LANE, LANE), F32), pltpu.VMEM((7, 8, LANE), F32),
                        pltpu.VMEM((2, ng, 2, sg, LANE), BF16), pltpu.VMEM((2, ng, sg, LANE), BF16),
                        pltpu.VMEM((2, ng, 2, sg, LANE), F32), pltpu.VMEM((2, 2 * ng, LANE, LANE), F32),
                        pltpu.VMEM((2, ng, sg, sg), BF16), pltpu.VMEM((2, 3, tb, LANE), F32)],
        compiler_params=_cparams(("parallel", "parallel", "arbitrary")),
        name="rwkv7",
    )(z, z, z, z, z, z, z, pv, musm, wup, aup, gup)


class _Layout:
    def __init__(self, d_model, gdn_heads, rwkv_width, decay_lora, aaa_lora, gate_lora, tn):
        self.d_model, self.gh, self.rw = d_model, gdn_heads, rwkv_width
        self.dl, self.al, self.gl = decay_lora, aaa_lora, gate_lora
        self.gw = gdn_heads * LANE
        assert decay_lora + 2 * gdn_heads <= LANE and aaa_lora <= LANE and gate_lora == 2 * LANE
        self.a_lane0 = decay_lora
        self.b_lane0 = decay_lora + gdn_heads
        self.cb_q = 0
        self.cb_gz = 3 * self.gw // LANE
        self.cb_r = 0
        self.cb_sm = 3 * rwkv_width // LANE
        self.off_z = 3 * self.gw
        self.off_a = self.off_z + self.gw
        self.off_b = self.off_a + gdn_heads
        self.off_rwkv = self.off_b + gdn_heads
        self.off_gates = self.off_rwkv + 3 * rwkv_width + decay_lora + aaa_lora + gate_lora

    def split_cols(self, w):
        lead = w.shape[:-1]
        zeros = lambda n: jnp.zeros(lead + (n,), w.dtype)
        rw, dl, al, gl, gh = self.rw, self.dl, self.al, self.gl, self.gh
        o = self.off_rwkv + 3 * rw
        piece_b = jnp.concatenate(
            [w[..., self.off_rwkv:o],
             w[..., o:o + dl], w[..., self.off_a:self.off_b], w[..., self.off_b:self.off_rwkv],
             zeros(LANE - dl - 2 * gh),
             w[..., o + dl:o + dl + al], zeros(LANE - al),
             w[..., o + dl + al:o + dl + al + gl]], axis=-1)
        return w[..., :self.off_a], piece_b, w[..., self.off_gates:]


def _pad_rows(w, rows):
    return jnp.concatenate([w, jnp.zeros((rows - w.shape[0],) + w.shape[1:], w.dtype)], axis=0)


def _gdn_from_packed(za, zb, p, lay, batch, seq, tb=256):
    lane_vec = lambda v, lane0: jnp.zeros((1, LANE), F32).at[0, lane0:lane0 + v.shape[0]].set(v)
    return _gdn_branch(za, zb, p["conv_w"], lane_vec(p["gdn_a_log"], lay.a_lane0),
                       lane_vec(p["gdn_dt_bias"], lay.a_lane0), p["gdn_onorm_g"].reshape(1, LANE),
                       batch=batch, seq=seq, heads=lay.gh, cb_q=lay.cb_q, cb_gz=lay.cb_gz,
                       cb_sm=lay.cb_sm, a_lane0=lay.a_lane0, b_lane0=lay.b_lane0, tb=min(tb, seq))


def _rwkv_from_packed(z, p, lay, batch, seq, tb=256):
    rw = lay.rw
    hd = p["rwkv_r_k"].shape[-1]
    assert LANE % hd == 0 and hd * 2 == LANE
    mu = p["rwkv_mu"]
    rows = {"w0": p["rwkv_w0"], "a0": p["rwkv_a0"], "k_k": p["rwkv_k_k"], "k_a": p["rwkv_k_a"],
            "r_k": p["rwkv_r_k"].reshape(rw), "ln_w": p["rwkv_ln_w"], "ln_b": p["rwkv_ln_b"],
            "mu_r": mu[:rw], "mu_k": mu[rw:2 * rw], "mu_v": mu[2 * rw:3 * rw]}
    pv = _pad_rows(jnp.stack([rows[k] for k in _RW_ROWS]).astype(F32), 16)
    o = 3 * rw
    zl = lambda n: jnp.zeros((n,), F32)
    musm = jnp.concatenate([mu[o:o + lay.dl], zl(LANE - lay.dl), mu[o + lay.dl:o + lay.dl + lay.al],
                            zl(LANE - lay.al), mu[o + lay.dl + lay.al:]])
    musm = _pad_rows(musm.reshape(1, 4 * LANE), 8)
    wup = _pad_rows(p["rwkv_w_up"], LANE).astype(BF16)
    aup = _pad_rows(p["rwkv_a_up"], LANE).astype(BF16)
    gup = p["rwkv_g_up"].astype(BF16)
    return _rwkv_branch(z, pv, musm, wup, aup, gup, batch=batch, seq=seq, width=rw, hd=hd,
                        cb_r=lay.cb_r, cb_sm=lay.cb_sm, tb=min(tb, seq))


def _make_layout(p, tn):
    return _Layout(p["norm1_g"].shape[-1], p["gdn_a_log"].shape[-1], p["rwkv_w0"].shape[-1],
                   p["rwkv_w_up"].shape[0], p["rwkv_a_up"].shape[0], p["rwkv_g_up"].shape[0], tn)


def _ada_kernel(c_ref, w_ref, b_ref, o_ref):
    o_ref[...] = _dot(_silu(c_ref[...]), w_ref[...]) + b_ref[...]


def _ada_mod(c_pad, w_ada, b_ada, tn):
    d, cols = w_ada.shape
    return pl.pallas_call(
        _ada_kernel,
        grid=(cols // tn,),
        in_specs=[pl.BlockSpec((8, d), lambda j: (0, 0)),
                  pl.BlockSpec((d, tn), lambda j: (0, j)),
                  pl.BlockSpec((1, tn), lambda j: (0, j))],
        out_specs=pl.BlockSpec((8, tn), lambda j: (0, j)),
        out_shape=jax.ShapeDtypeStruct((8, cols), F32),
        compiler_params=_cparams(("arbitrary",)),
        name="ada_mod",
    )(c_pad, w_ada, b_ada.reshape(1, cols))


def _modnorm(x, g, sc, sh):
    y = x * lax.rsqrt(jnp.mean(x * x, axis=-1, keepdims=True) + RMS_EPS)
    return y * g * (1.0 + sc) + sh


def _norm1_kernel(x_ref, g_ref, sh_ref, sc_ref, o_ref):
    b = pl.program_id(0)
    h = _modnorm(x_ref[...], g_ref[...], sc_ref[pl.ds(b, 1), :], sh_ref[pl.ds(b, 1), :])
    o_ref[...] = h.astype(o_ref.dtype)


def _norm1(x2, g, mod, *, batch, seq, tm):
    n, d = x2.shape
    nt = seq // tm
    return pl.pallas_call(
        _norm1_kernel,
        grid=(batch, nt),
        in_specs=[pl.BlockSpec((tm, d), lambda b, t: (b * nt + t, 0)),
                  pl.BlockSpec((1, d), lambda b, t: (0, 0)),
                  pl.BlockSpec((8, d), lambda b, t: (0, 0)),
                  pl.BlockSpec((8, d), lambda b, t: (0, 1))],
        out_specs=pl.BlockSpec((tm, d), lambda b, t: (b * nt + t, 0)),
        out_shape=jax.ShapeDtypeStruct((n, d), BF16),
        compiler_params=_cparams(("parallel", "arbitrary")),
        name="norm1",
    )(x2, g, mod, mod)


def _merge_kernel(o_ref, y_ref, wa_ref, wb_ref, ga_ref, gb_ref, m_ref):
    ha = jnp.dot(o_ref[...], wa_ref[...], preferred_element_type=F32)
    hb = jnp.dot(y_ref[...], wb_ref[...], preferred_element_type=F32)
    m_ref[...] = (ga_ref[...].astype(F32) * ha + gb_ref[...].astype(F32) * hb).astype(m_ref.dtype)


def _merge(o, y, wa, wb, gates, *, tm, tn):
    n, d = o.shape[0], wa.shape[1]
    tn = math.gcd(tn, d)
    gb0 = 0
    return pl.pallas_call(
        _merge_kernel,
        grid=(n // tm, d // tn),
        in_specs=[pl.BlockSpec((tm, o.shape[1]), lambda i, j: (i, 0)),
                  pl.BlockSpec((tm, y.shape[1]), lambda i, j: (i, 0)),
                  pl.BlockSpec((wa.shape[0], tn), lambda i, j: (0, j)),
                  pl.BlockSpec((wb.shape[0], tn), lambda i, j: (0, j)),
                  pl.BlockSpec((tm, tn), lambda i, j: (i, gb0 + j)),
                  pl.BlockSpec((tm, tn), lambda i, j: (i, gb0 + d // tn + j))],
        out_specs=pl.BlockSpec((tm, tn), lambda i, j: (i, j)),
        out_shape=jax.ShapeDtypeStruct((n, d), BF16),
        compiler_params=_cparams(("parallel", "arbitrary")),
        name="merge",
    )(o, y, wa, wb, gates, gates)


def _outproj_kernel(m_ref, w_ref, x_ref, g_ref, o_ref, *, tiles_per_batch):
    b = pl.program_id(0) // tiles_per_batch
    y = jnp.dot(m_ref[...], w_ref[...], preferred_element_type=F32)
    o_ref[...] = x_ref[...] + g_ref[pl.ds(b, 1), :] * y


def _outproj(m, w, x2, mod, *, seq, tm, tn):
    n, d = x2.shape
    gcol = 2 * d // tn
    return pl.pallas_call(
        functools.partial(_outproj_kernel, tiles_per_batch=seq // tm),
        grid=(n // tm, d // tn),
        in_specs=[pl.BlockSpec((tm, d), lambda i, j: (i, 0)),
                  pl.BlockSpec((d, tn), lambda i, j: (0, j)),
                  pl.BlockSpec((tm, tn), lambda i, j: (i, j)),
                  pl.BlockSpec((8, tn), lambda i, j: (0, gcol + j))],
        out_specs=pl.BlockSpec((tm, tn), lambda i, j: (i, j)),
        out_shape=jax.ShapeDtypeStruct((n, d), F32),
        compiler_params=_cparams(("parallel", "arbitrary")),
        name="outproj",
    )(m, w, x2, mod)


def _router_kernel(x_ref, g_ref, sh_ref, sc_ref, wr_ref, br_ref, h_ref, id_ref, wt_ref, *,
                   n_groups, per_group):
    b = pl.program_id(0)
    h = _modnorm(x_ref[...], g_ref[...], sc_ref[pl.ds(b, 1), :], sh_ref[pl.ds(b, 1), :])
    h_ref[...] = h
    logits = _dot(h, wr_ref[...]) + br_ref[...]
    tm = logits.shape[0]
    lane = lax.broadcasted_iota(jnp.int32, (tm, LANE), 1)
    neg = jnp.float32(-jnp.inf)
    big = jnp.int32(LANE)

    def first_argmax(vals, mask):
        vm = jnp.where(mask, vals, neg)
        mx = jnp.max(vm, axis=1, keepdims=True)
        idx = jnp.min(jnp.where(jnp.logical_and(mask, vm == mx), lane, big), axis=1, keepdims=True)
        return mx, idx

    gmask = lane < n_groups
    gmax, g_sel = first_argmax(logits, gmask)
    g_prob = 1.0 / jnp.sum(jnp.where(gmask, jnp.exp(logits - gmax), 0.0), axis=1, keepdims=True)
    e_lane = lane - n_groups
    emask = jnp.logical_and(e_lane >= g_sel * per_group, e_lane < (g_sel + 1) * per_group)
    emax = jnp.max(jnp.where(emask, logits, neg), axis=1, keepdims=True)
    ex = jnp.where(emask, jnp.exp(logits - emax), 0.0)
    probs = ex / jnp.sum(ex, axis=1, keepdims=True)
    p0, i0 = first_argmax(probs, emask)
    p1, i1 = first_argmax(probs, jnp.logical_and(emask, lane != i0))
    den = p0 + p1
    id_ref[...] = jnp.where(lane == 0, i0 - n_groups, jnp.where(lane == 1, i1 - n_groups, 0))
    wt_ref[...] = jnp.where(lane == 0, g_prob * (p0 / den), jnp.where(lane == 1, g_prob * (p1 / den), 0.0))


def _router(x1, g, mod, wr, br, *, batch, seq, tm, n_groups, per_group):
    n, d = x1.shape
    nt = seq // tm
    rowblk = lambda w: pl.BlockSpec((tm, w), lambda b, t: (b * nt + t, 0))
    return pl.pallas_call(
        functools.partial(_router_kernel, n_groups=n_groups, per_group=per_group),
        grid=(batch, nt),
        in_specs=[rowblk(d),
                  pl.BlockSpec((1, d), lambda b, t: (0, 0)),
                  pl.BlockSpec((8, d), lambda b, t: (0, 3)),
                  pl.BlockSpec((8, d), lambda b, t: (0, 4)),
                  pl.BlockSpec((d, LANE), lambda b, t: (0, 0)),
                  pl.BlockSpec((1, LANE), lambda b, t: (0, 0))],
        out_specs=[rowblk(d), rowblk(LANE), rowblk(LANE)],
        out_shape=[jax.ShapeDtypeStruct((n, d), F32), jax.ShapeDtypeStruct((n, LANE), jnp.int32),
                   jax.ShapeDtypeStruct((n, LANE), F32)],
        compiler_params=_cparams(("parallel", "arbitrary")),
        name="router",
    )(x1, g, mod, mod, wr, br)


def _expert_kernel(be_ref, tok_ref, h_hbm, w1_ref, w3_ref, w2_ref, bw_ref, o_ref, xbuf, sem):
    i = pl.program_id(0)
    nb = pl.num_programs(0)

    def row_copy(blk, slot, r):
        tok = tok_ref[blk * MOE_BLOCK + r]
        return pltpu.make_async_copy(h_hbm.at[pl.ds(tok, 1), :], xbuf.at[slot, pl.ds(r, 1), :],
                                     sem.at[slot])

    def start_block(blk, slot):
        def body(r, carry):
            row_copy(blk, slot, r).start()
            return carry
        lax.fori_loop(0, MOE_BLOCK, body, 0, unroll=DMA_UNROLL)

    @pl.when(i == 0)
    def _():
        start_block(0, 0)

    @pl.when(i + 1 < nb)
    def _():
        start_block(i + 1, (i + 1) % 2)

    slot = i % 2

    def wait_body(r, carry):
        row_copy(i, slot, r).wait()
        return carry
    lax.fori_loop(0, MOE_BLOCK, wait_body, 0, unroll=DMA_UNROLL)

    x = xbuf[slot]
    a = _dot(x, w1_ref[...])
    g = _dot(x, w3_ref[...])
    y = _dot(_silu(a) * g, w2_ref[...])
    o_ref[...] = y * bw_ref[...]


def _experts(h2, w1, w3, w2, block_expert, buf_tok, buf_w):
    d = h2.shape[1]
    de = w1.shape[2]
    p_rows = buf_tok.shape[0]
    wspec = lambda shape: pl.BlockSpec((None,) + shape, lambda i, be, tok: (be[i], 0, 0))
    grid_spec = pltpu.PrefetchScalarGridSpec(
        num_scalar_prefetch=2,
        grid=(p_rows // MOE_BLOCK,),
        in_specs=[pl.BlockSpec(memory_space=pl.ANY),
                  wspec((d, de)), wspec((d, de)), wspec((de, d)),
                  pl.BlockSpec((MOE_BLOCK, 1), lambda i, be, tok: (i, 0))],
        out_specs=pl.BlockSpec((MOE_BLOCK, d), lambda i, be, tok: (i, 0)),
        scratch_shapes=[pltpu.VMEM((2, MOE_BLOCK, d), F32), pltpu.SemaphoreType.DMA((2,))],
    )
    return pl.pallas_call(
        _expert_kernel,
        grid_spec=grid_spec,
        out_shape=jax.ShapeDtypeStruct((p_rows, d), F32),
        compiler_params=_cparams(("arbitrary",)),
        name="experts",
    )(block_expert, buf_tok, h2, w1, w3, w2, buf_w)


def _combine_kernel(pos_ref, x_ref, g_ref, gf_ref, y_hbm, o_ref, ybuf, sem, *, tm, tiles_per_batch):
    i = pl.program_id(0)
    nb = pl.num_programs(0)
    rows = TOP_K * tm

    def row_copy(blk, slot, r):
        return pltpu.make_async_copy(y_hbm.at[pl.ds(pos_ref[blk * rows + r], 1), :],
                                     ybuf.at[slot, pl.ds(r, 1), :], sem.at[slot])

    def start_block(blk, slot):
        def body(r, carry):
            row_copy(blk, slot, r).start()
            return carry
        lax.fori_loop(0, rows, body, 0, unroll=DMA_UNROLL)

    @pl.when(i == 0)
    def _():
        start_block(0, 0)

    @pl.when(i + 1 < nb)
    def _():
        start_block(i + 1, (i + 1) % 2)

    slot = i % 2

    def wait_body(r, carry):
        row_copy(i, slot, r).wait()
        return carry
    lax.fori_loop(0, rows, wait_body, 0, unroll=DMA_UNROLL)

    b = i // tiles_per_batch
    moe = ybuf[slot, 0:tm, :] + ybuf[slot, tm:rows, :]
    x = x_ref[...] + g_ref[pl.ds(b, 1), :] * moe
    o_ref[...] = x * lax.rsqrt(jnp.mean(x * x, axis=-1, keepdims=True) + RMS_EPS) * gf_ref[...]


def _combine(pos, x1, mod, gf, yb, *, seq, tm):
    n, d = x1.shape
    grid_spec = pltpu.PrefetchScalarGridSpec(
        num_scalar_prefetch=1,
        grid=(n // tm,),
        in_specs=[pl.BlockSpec((tm, d), lambda i, pos: (i, 0)),
                  pl.BlockSpec((8, d), lambda i, pos: (0, 5)),
                  pl.BlockSpec((1, d), lambda i, pos: (0, 0)),
                  pl.BlockSpec(memory_space=pl.ANY)],
        out_specs=pl.BlockSpec((tm, d), lambda i, pos: (i, 0)),
        scratch_shapes=[pltpu.VMEM((2, TOP_K * tm, d), F32), pltpu.SemaphoreType.DMA((2,))],
    )
    return pl.pallas_call(
        functools.partial(_combine_kernel, tm=tm, tiles_per_batch=seq // tm),
        grid_spec=grid_spec,
        out_shape=jax.ShapeDtypeStruct((n, d), F32),
        compiler_params=_cparams(("arbitrary",)),
        name="combine",
    )(pos, x1, mod, gf, yb)


def _dispatch(ids, wts, n_experts):
    n = ids.shape[0]
    a = n * TOP_K
    expert_id = ids.reshape(a)
    w_flat = wts.reshape(a)
    order = jnp.argsort(expert_id).astype(jnp.int32)
    rank_of = jnp.argsort(order).astype(jnp.int32)
    experts = jnp.arange(n_experts, dtype=jnp.int32)
    counts = jnp.sum((expert_id[:, None] == experts[None, :]).astype(jnp.int32), axis=0)
    c_start = jnp.cumsum(counts) - counts
    padded = (counts + MOE_BLOCK - 1) // MOE_BLOCK * MOE_BLOCK
    p_end = jnp.cumsum(padded)
    p_start = p_end - padded
    pos = rank_of + (p_start - c_start)[expert_id]
    p_rows = (a + n_experts * (MOE_BLOCK - 1) + MOE_BLOCK - 1) // MOE_BLOCK * MOE_BLOCK
    n_blocks = p_rows // MOE_BLOCK
    starts = jnp.arange(n_blocks, dtype=jnp.int32) * MOE_BLOCK
    block_expert = jnp.minimum(jnp.sum((p_end[None, :] <= starts[:, None]).astype(jnp.int32), axis=1),
                               n_experts - 1)
    slot = jnp.arange(p_rows, dtype=jnp.int32)
    slot_expert = jnp.repeat(block_expert, MOE_BLOCK)
    within = slot - p_start[slot_expert]
    valid = within < counts[slot_expert]
    src = order[jnp.clip(within + c_start[slot_expert], 0, a - 1)]
    buf_tok = jnp.where(valid, src // TOP_K, 0)
    buf_w = jnp.where(valid, w_flat[src], 0.0)
    return pos, buf_tok, buf_w, block_expert


def _pick(n, pref):
    t = min(n, pref)
    while n % t:
        t //= 2
    return t


def kernel(x, c, w_ada, b_ada, norm1_g, w_in, conv_w, gdn_a_log, gdn_dt_bias, gdn_onorm_g, rwkv_mu,
           rwkv_w0, rwkv_w_up, rwkv_a0, rwkv_a_up, rwkv_g_up, rwkv_k_k, rwkv_k_a, rwkv_r_k, rwkv_ln_w,
           rwkv_ln_b, w_gdn_o, w_rwkv_o, w_out, norm2_g, w_group, b_group, w_expert, b_expert, w1, w3,
           w2, norm_f_g):
    batch, seq, d = x.shape
    n = batch * seq
    depth = w_ada.shape[0]
    assert depth == 1, "the final norm is fused into the last layer's combine"
    assert batch <= 8 and seq % CHUNK == 0 and d % 512 == 0
    x2 = x.reshape(n, d)
    c_pad = _pad_rows(c, 8)
    tm_big = _pick(seq, 1024)
    tm_row = _pick(seq, 256)
    tn = 512
    for l in range(depth):
        p = {"norm1_g": norm1_g[l], "conv_w": conv_w[l], "gdn_a_log": gdn_a_log[l],
             "gdn_dt_bias": gdn_dt_bias[l], "gdn_onorm_g": gdn_onorm_g[l], "rwkv_mu": rwkv_mu[l],
             "rwkv_w0": rwkv_w0[l], "rwkv_w_up": rwkv_w_up[l], "rwkv_a0": rwkv_a0[l],
             "rwkv_a_up": rwkv_a_up[l], "rwkv_g_up": rwkv_g_up[l], "rwkv_k_k": rwkv_k_k[l],
             "rwkv_k_a": rwkv_k_a[l], "rwkv_r_k": rwkv_r_k[l], "rwkv_ln_w": rwkv_ln_w[l],
             "rwkv_ln_b": rwkv_ln_b[l]}
        lay = _make_layout(p, tn)
        mod = _ada_mod(c_pad, w_ada[l], b_ada[l], tn)

        h1 = _norm1(x2, norm1_g[l].reshape(1, d), mod, batch=batch, seq=seq, tm=tm_row)
        w_a, w_b, w_c = lay.split_cols(w_in[l])
        za = _matmul(h1, w_a.astype(BF16), tm_big, tn, F32, name="in_proj_gdn")
        zb = _matmul(h1, w_b.astype(BF16), tm_big, tn, F32, name="in_proj_rwkv")
        gates = _matmul(h1, w_c.astype(BF16), tm_big, tn, BF16, act=_sigmoid, name="in_proj_gates")
        o_gdn = _gdn_from_packed(za, zb, p, lay, batch, seq)
        y_rwkv = _rwkv_from_packed(zb, p, lay, batch, seq)
        m = _merge(o_gdn, y_rwkv, w_gdn_o[l].astype(BF16), w_rwkv_o[l].astype(BF16), gates,
                   tm=tm_big, tn=tn)
        x1 = _outproj(m, w_out[l].astype(BF16), x2, mod, seq=seq, tm=tm_big, tn=tn)

        n_groups, n_experts = w_group.shape[-1], w_expert.shape[-1]
        assert n_groups + n_experts <= LANE
        wr = jnp.concatenate([w_group[l], w_expert[l],
                              jnp.zeros((d, LANE - n_groups - n_experts), F32)], axis=1).astype(BF16)
        br = jnp.concatenate([b_group[l], b_expert[l],
                              jnp.zeros((LANE - n_groups - n_experts,), F32)]).reshape(1, LANE)
        h2, ids, wts = _router(x1, norm2_g[l].reshape(1, d), mod, wr, br, batch=batch, seq=seq,
                               tm=tm_row, n_groups=n_groups, per_group=n_experts // n_groups)
        pos, buf_tok, buf_w, block_expert = _dispatch(ids[:, :TOP_K], wts[:, :TOP_K], n_experts)
        yb = _experts(h2, w1[l].astype(BF16), w3[l].astype(BF16), w2[l].astype(BF16), block_expert,
                      buf_tok, buf_w.reshape(-1, 1))
        tm_c = _pick(seq, 128)
        pos_tiles = pos.reshape(n // tm_c, tm_c, TOP_K).transpose(0, 2, 1).reshape(n * TOP_K)
        x2 = _combine(pos_tiles, x1, mod, norm_f_g.reshape(1, d), yb, seq=seq, tm=tm_c)
    return x2.reshape(batch, seq, d)
```

```python
import functools
import math

import jax
import jax.numpy as jnp
from jax import lax
from jax.experimental import pallas as pl
from jax.experimental.pallas import tpu as pltpu

F32 = jnp.float32
BF16 = jnp.bfloat16

LANE = 128
CHUNK = 64
CONV_WIDTH = 4
RMS_EPS = 1e-6
RWKV_GN_EPS = 64e-5
MOE_BLOCK = 128
TOP_K = 2
DMA_UNROLL = 8
SEQS_PER_STEP = 2
VMEM_LIMIT = 56 * 1024 * 1024


def _cparams(sem):
    return pltpu.CompilerParams(dimension_semantics=sem, vmem_limit_bytes=VMEM_LIMIT)


def _dot(a, b):
    return jnp.dot(a.astype(BF16), b.astype(BF16), preferred_element_type=F32)


def _dot_nt(a, b):
    return lax.dot_general(a.astype(BF16), b.astype(BF16), (((1,), (1,)), ((), ())),
                           preferred_element_type=F32)


def _dot_tn(a, b):
    return lax.dot_general(a.astype(BF16), b.astype(BF16), (((0,), (0,)), ((), ())),
                           preferred_element_type=F32)


def _split(x, terms):
    pieces = []
    for _ in range(terms - 1):
        hi = x.astype(BF16)
        pieces.append(hi)
        x = x - hi.astype(F32)
    pieces.append(x.astype(BF16))
    return pieces


def _dot_sel(sel, x, terms, left=True):
    out = None
    for piece in _split(x, terms):
        ops = (sel, piece) if left else (piece, sel)
        d = jnp.dot(*ops, preferred_element_type=F32)
        out = d if out is None else out + d
    return out


def _sigmoid(x):
    return 1.0 / (1.0 + jnp.exp(-x))


def _silu(x):
    return x * _sigmoid(x)


def _softplus(x):
    return jnp.maximum(x, 0.0) + jnp.log(1.0 + jnp.exp(-jnp.abs(x)))


def _inv_unit_lower(L, n, top):
    r = lax.broadcasted_iota(jnp.int32, (n, n), 0)
    c = lax.broadcasted_iota(jnp.int32, (n, n), 1)
    eye = (r == c).astype(F32)
    same16 = (r // 16) == (c // 16)
    Ld = jnp.where(same16, L, 0.0)
    X = eye - Ld
    P = _dot(Ld, Ld)
    yield
    for _ in range(2):
        X = X + _dot(X, P)
        P = _dot(P, P)
        yield
    X = X + _dot(X, P)
    yield
    bs = 32
    while bs <= top:
        inner = (r // (bs // 2)) == (c // (bs // 2))
        outer = (r // bs) == (c // bs)
        Lo = jnp.where(jnp.logical_and(outer, jnp.logical_not(inner)), L, 0.0)
        Y = _dot(Lo, X)
        yield
        X = X - _dot(X, Y)
        yield
        bs *= 2
    return X


def _round_robin(gens):
    gens = list(gens)
    while gens:
        for g in list(gens):
            try:
                more = next(g)
            except StopIteration:
                gens.remove(g)
                continue
            if more:
                gens.extend(more)


def _mm_kernel(a_ref, w_ref, o_ref, *, act):
    y = jnp.dot(a_ref[...], w_ref[...], preferred_element_type=F32)
    o_ref[...] = (act(y) if act else y).astype(o_ref.dtype)


def _matmul(a, w, tm, tn, out_dtype, act=None, name="matmul"):
    M, K = a.shape
    Nn = w.shape[1]
    tn = math.gcd(tn, Nn)
    return pl.pallas_call(
        functools.partial(_mm_kernel, act=act),
        grid=(M // tm, Nn // tn),
        in_specs=[pl.BlockSpec((tm, K), lambda i, j: (i, 0)),
                  pl.BlockSpec((K, tn), lambda i, j: (0, j))],
        out_specs=pl.BlockSpec((tm, tn), lambda i, j: (i, j)),
        out_shape=jax.ShapeDtypeStruct((M, Nn), out_dtype),
        compiler_params=_cparams(("parallel", "arbitrary")),
        name=name,
    )(a, w)


def _run_pipelined(t, nt, heads):
    def prepare():
        outs = [{} for _ in heads]
        return outs, [hd[0](out) for hd, out in zip(heads, outs)]

    def finish(outs):
        for hd, out in zip(heads, outs):
            hd[3](out)

    @pl.when(t == 0)
    def _():
        outs, gens = prepare()
        _round_robin(gens)
        finish(outs)

    @pl.when(jnp.logical_and(t > 0, t < nt))
    def _():
        prevs = [hd[2]() for hd in heads]
        outs, gens = prepare()
        _round_robin([hd[1](p) for hd, p in zip(heads, prevs)] + gens)
        finish(outs)

    @pl.when(t == nt)
    def _():
        _round_robin([hd[1](hd[2]()) for hd in heads])


def _gdn_kernel(zq_ref, zk_ref, zv_ref, gz_ref, sm_ref, cwq_ref, cwk_ref, cwv_ref, alog_ref, dtb_ref,
                on_ref, o_ref, s_ref, xs_ref, pw_ref, pu_ref, pattn_ref, *, nsub, nt, **kw):
    t = pl.program_id(2)

    @pl.when(t == 0)
    def _():
        s_ref[...] = jnp.zeros_like(s_ref)
        xs_ref[:, :, 0:8, :] = jnp.zeros((nsub, 3, 8, LANE), F32)

    lanes = lambda s: slice(s * LANE, (s + 1) * LANE)
    heads = [_gdn_head(pl.program_id(1) * nsub + s, t,
                       zq_ref.at[:, lanes(s)], zk_ref.at[:, lanes(s)], zv_ref.at[:, lanes(s)],
                       gz_ref.at[:, lanes(s)], sm_ref, cwq_ref.at[:, lanes(s)], cwk_ref.at[:, lanes(s)],
                       cwv_ref.at[:, lanes(s)], alog_ref, dtb_ref, on_ref, o_ref.at[:, lanes(s)],
                       s_ref.at[s], xs_ref.at[s], pw_ref.at[s], pu_ref.at[s], pattn_ref.at[s], **kw)
             for s in range(nsub)]
    _run_pipelined(t, nt, heads)


def _gdn_head(h, t, zq_ref, zk_ref, zv_ref, gz_ref, sm_ref, cwq_ref, cwk_ref, cwv_ref, alog_ref, dtb_ref,
              on_ref, o_ref, s_ref, xs_ref, pw_ref, pu_ref, pattn_ref, *, tb, a_lane0, b_lane0, q_scale):
    C = CHUNK

    def conv_silu(idx, z_ref, cw_ref):
        xs_ref[idx, 8:8 + tb, :] = z_ref[...]
        cw = cw_ref[...]
        acc = cw[CONV_WIDTH - 1:CONV_WIDTH, :] * xs_ref[idx, 8:8 + tb, :]
        for i in range(1, CONV_WIDTH):
            acc = acc + cw[CONV_WIDTH - 1 - i:CONV_WIDTH - i, :] * xs_ref[idx, 8 - i:8 - i + tb, :]
        xs_ref[idx, 0:8, :] = xs_ref[idx, tb:tb + 8, :]
        return _silu(acc)

    def l2n(x):
        return x * lax.rsqrt(jnp.sum(x * x, axis=-1, keepdims=True) + 1e-6)

    def phase_ab(out):
        q_all = l2n(conv_silu(0, zq_ref, cwq_ref)) * q_scale
        yield
        k_all = l2n(conv_silu(1, zk_ref, cwk_ref))
        yield
        v_all = conv_silu(2, zv_ref, cwv_ref)
        sm = sm_ref[...]
        lane = lax.broadcasted_iota(jnp.int32, (1, LANE), 1)
        g_full = -jnp.exp(alog_ref[...]) * _softplus(sm + dtb_ref[...])
        g_all = jnp.sum(jnp.where(lane == a_lane0 + h, g_full, 0.0), axis=1, keepdims=True)
        beta_all = jnp.sum(jnp.where(lane == b_lane0 + h, _sigmoid(sm), 0.0), axis=1, keepdims=True)
        yield
        r = lax.broadcasted_iota(jnp.int32, (tb, tb), 0)
        c = lax.broadcasted_iota(jnp.int32, (tb, tb), 1)
        same = (r // C) == (c // C)
        causal = jnp.logical_and(same, r >= c)
        strict = jnp.logical_and(same, r > c)
        gb = jnp.broadcast_to(g_all, (tb, LANE))
        gc_lanes = _dot_sel(causal.astype(BF16), gb, 2)
        yield
        g_tot = _dot_sel(same.astype(BF16), gb, 2)
        yield
        gc_col = jnp.concatenate([gc_lanes] * (tb // LANE), axis=1)
        gc_row = gc_col.T
        decay = jnp.where(causal, jnp.exp(jnp.minimum(gc_col - gc_row, 0.0)), 0.0)
        gc = gc_col[:, 0:1]
        e_gc = jnp.exp(gc)
        kb = k_all * beta_all
        sc = _dot_nt(jnp.concatenate([kb, q_all], axis=0), k_all)
        yield
        L = sc[:tb] * jnp.where(strict, decay, 0.0)
        T = yield from _inv_unit_lower(L, tb, C)
        uw = _dot(T, jnp.concatenate([v_all * beta_all, kb * e_gc], axis=1))
        yield
        out["w"] = uw[:, LANE:].astype(BF16)
        out["q_dec"] = (q_all * e_gc).astype(BF16)
        out["k_dec"] = (k_all * jnp.exp(g_tot[:, 0:1] - gc)).astype(BF16)
        out["u"] = uw[:, :LANE]
        out["e_gl"] = jnp.exp(g_tot[:, :LANE])
        out["gzn"] = on_ref[...] * _silu(gz_ref[...])
        out["attn"] = (sc[tb:] * decay).astype(BF16)

    wslot = lax.rem(t, 2)
    rslot = 1 - wslot

    def store(out):
        pw_ref[wslot, 0], pw_ref[wslot, 1], pw_ref[wslot, 2] = out["w"], out["q_dec"], out["k_dec"]
        pu_ref[wslot, 0], pu_ref[wslot, 1], pu_ref[wslot, 2] = out["u"], out["e_gl"], out["gzn"]
        pattn_ref[wslot] = out["attn"]

    def load():
        return dict(w=pw_ref[rslot, 0], q_dec=pw_ref[rslot, 1], k_dec=pw_ref[rslot, 2],
                    u=pu_ref[rslot, 0], e_gl=pu_ref[rslot, 1], gzn=pu_ref[rslot, 2],
                    attn=pattn_ref[rslot])

    def phase_c(p):
        S = s_ref[...]
        v_news, o_qs = [], []
        for ci in range(tb // C):
            sl = slice(ci * C, (ci + 1) * C)
            ws = _dot(jnp.concatenate([p["w"][sl], p["q_dec"][sl]], axis=0), S)
            yield
            v_new = p["u"][sl] - ws[:C]
            v_news.append(v_new)
            o_qs.append(ws[C:])
            S = S * p["e_gl"][ci * C:ci * C + 1, :] + _dot_tn(p["k_dec"][sl], v_new)
            yield
        s_ref[...] = S
        o = jnp.concatenate(o_qs, axis=0) + _dot(p["attn"], jnp.concatenate(v_news, axis=0))
        yield
        o = o * lax.rsqrt(jnp.mean(o * o, axis=-1, keepdims=True) + RMS_EPS) * p["gzn"]
        o_ref[...] = o.astype(o_ref.dtype)

    return phase_ab, phase_c, load, store


def _gdn_branch(za, zb, conv_w, alog_pad, dtb_pad, onorm_g, *, batch, seq, heads, cb_q, cb_gz, cb_sm,
                a_lane0, b_lane0, tb):
    n = batch * seq
    nt = seq // tb
    nsub = SEQS_PER_STEP if heads % SEQS_PER_STEP == 0 else 1
    wide = nsub * LANE
    row_in = lambda b, t: b * nt + jnp.minimum(t, nt - 1)
    row_out = lambda b, t: b * nt + jnp.maximum(t - 1, 0)
    zspec = lambda cb: pl.BlockSpec((tb, wide), lambda b, h, t: (row_in(b, t), cb // nsub + h))
    cwspec = lambda cb: pl.BlockSpec((CONV_WIDTH, wide), lambda b, h, t: (0, cb // nsub + h))
    vec = pl.BlockSpec((1, LANE), lambda b, h, t: (0, 0))
    kern = functools.partial(_gdn_kernel, tb=tb, nt=nt, nsub=nsub, a_lane0=a_lane0, b_lane0=b_lane0,
                             q_scale=float(LANE) ** -0.5)
    return pl.pallas_call(
        kern,
        grid=(batch, heads // nsub, nt + 1),
        in_specs=[zspec(cb_q), zspec(cb_q + heads), zspec(cb_q + 2 * heads), zspec(cb_gz),
                  pl.BlockSpec((tb, LANE), lambda b, h, t: (row_in(b, t), cb_sm)),
                  cwspec(0), cwspec(heads), cwspec(2 * heads), vec, vec, vec],
        out_specs=pl.BlockSpec((tb, wide), lambda b, h, t: (row_out(b, t), h)),
        out_shape=jax.ShapeDtypeStruct((n, heads * LANE), BF16),
        scratch_shapes=[pltpu.VMEM((nsub, LANE, LANE), F32), pltpu.VMEM((nsub, 3, tb + 8, LANE), F32),
                        pltpu.VMEM((nsub, 2, 3, tb, LANE), BF16), pltpu.VMEM((nsub, 2, 3, tb, LANE), F32),
                        pltpu.VMEM((nsub, 2, tb, tb), BF16)],
        compiler_params=_cparams(("parallel", "parallel", "arbitrary")),
        name="gdn",
    )(za, za, za, za, zb, conv_w, conv_w, conv_w, alog_pad, dtb_pad, onorm_g)


_RW_ROWS = ("w0", "a0", "k_k", "k_a", "r_k", "ln_w", "ln_b", "mu_r", "mu_k", "mu_v")


def _rwkv_kernel(zr_ref, zk_ref, zv_ref, sm0_ref, sm1_ref, sm2_ref, sm3_ref, pv_ref, musm_ref,
                 wup_ref, aup_ref, gup_ref, o_ref, h_ref, c_ref, pwx_ref, pbs_ref, puo_ref, phk_ref,
                 parb_ref, pvec_ref, *, nsub, nt, **kw):
    t = pl.program_id(2)

    @pl.when(t == 0)
    def _():
        h_ref[...] = jnp.zeros_like(h_ref)
        c_ref[...] = jnp.zeros_like(c_ref)

    lanes = lambda s: slice(s * LANE, (s + 1) * LANE)
    heads = [_rwkv_head_pair(t, zr_ref.at[:, lanes(s)], zk_ref.at[:, lanes(s)], zv_ref.at[:, lanes(s)],
                             sm0_ref, sm1_ref, sm2_ref, sm3_ref, pv_ref.at[:, lanes(s)], musm_ref,
                             wup_ref.at[:, lanes(s)], aup_ref.at[:, lanes(s)], gup_ref.at[:, lanes(s)],
                             o_ref.at[:, lanes(s)], h_ref.at[s], c_ref.at[s], pwx_ref.at[s], pbs_ref.at[s],
                             puo_ref.at[s], phk_ref.at[s], parb_ref.at[s], pvec_ref.at[s], **kw)
             for s in range(nsub)]
    _run_pipelined(t, nt, heads)


def _rwkv_head_pair(t, zr_ref, zk_ref, zv_ref, sm0_ref, sm1_ref, sm2_ref, sm3_ref, pv_ref, musm_ref,
                    wup_ref, aup_ref, gup_ref, o_ref, h_ref, c_ref, pwx_ref, pbs_ref, puo_ref, phk_ref,
                    parb_ref, pvec_ref, *, tb, hd):
    C = CHUNK
    G2, SG = 2 * C, 4 * C
    n_groups = tb // G2

    pv = pv_ref[...]
    prow = lambda name: pv[_RW_ROWS.index(name):_RW_ROWS.index(name) + 1, :]
    rr = lax.broadcasted_iota(jnp.int32, (LANE, LANE), 0)
    cc = lax.broadcasted_iota(jnp.int32, (LANE, LANE), 1)
    bd_ones = ((rr // hd) == (cc // hd)).astype(BF16)
    lane = lax.broadcasted_iota(jnp.int32, (1, LANE), 1)
    m0 = (lane < hd).astype(F32)
    m1 = 1.0 - m0

    def phase_a(a, out):
        musm = musm_ref[...]
        row0 = lax.broadcasted_iota(jnp.int32, (tb, LANE), 0) == 0

        def shift(idx, ref, mu):
            x = ref[...]
            prev = jnp.where(row0, c_ref[idx, 0:1, :], pltpu.roll(x, 1, 0))
            c_ref[idx, 0:1, :] = x[tb - 1:tb, :]
            return x + (prev - x) * mu

        r_all = shift(0, zr_ref, prow("mu_r"))
        k_all = shift(1, zk_ref, prow("mu_k"))
        v_all = shift(2, zv_ref, prow("mu_v"))
        yield
        wd = shift(3, sm0_ref, musm[0:1, 0:LANE])
        ad = shift(4, sm1_ref, musm[0:1, LANE:2 * LANE])
        gd0 = shift(5, sm2_ref, musm[0:1, 2 * LANE:3 * LANE])
        gd1 = shift(6, sm3_ref, musm[0:1, 3 * LANE:4 * LANE])
        w_lin = prow("w0") + _dot(jnp.tanh(wd), wup_ref[...])
        logw = -jnp.exp(-_softplus(-w_lin) - 0.5)
        yield
        alr = _sigmoid(prow("a0") + _dot(ad, aup_ref[...]))
        gate = _dot(_sigmoid(gd0), gup_ref[0:LANE, :]) + _dot(_sigmoid(gd1), gup_ref[LANE:2 * LANE, :])
        yield
        kraw = k_all * prow("k_k")
        kk = kraw * lax.rsqrt(_dot_sel(bd_ones, kraw * kraw, 2, left=False) + 1e-6)
        keff = k_all * (1.0 + (alr - 1.0) * prow("k_a"))
        bonus = _dot_sel(bd_ones, r_all * keff * prow("r_k"), 2, left=False) * v_all
        yield
        rt_i = lax.broadcasted_iota(jnp.int32, (tb, tb), 0)
        ct_i = lax.broadcasted_iota(jnp.int32, (tb, tb), 1)
        same_chunk = (rt_i // C) == (ct_i // C)
        cum = _dot_sel(jnp.logical_and(same_chunk, rt_i >= ct_i).astype(BF16), logw, 3)
        yield
        ctot = _dot_sel(same_chunk.astype(BF16), logw, 3)
        yield
        e_neg = jnp.exp(-cum)
        e_end = jnp.exp(ctot - cum)
        b_all = kk * alr
        a.update(v=v_all, rt=r_all * jnp.exp(cum), at=-kk * jnp.exp(cum - logw),
                 bt=b_all * e_neg, kt=keff * e_neg, bh=b_all * e_end, kh=keff * e_end)
        out.update(e_tot=jnp.exp(ctot), bonus=bonus, gate=gate)
        yield [phase_b(a, out, g) for g in range(n_groups)]

    def phase_b(a, out, g):
        r4 = lax.broadcasted_iota(jnp.int32, (SG, SG), 0)
        c4 = lax.broadcasted_iota(jnp.int32, (SG, SG), 1)
        same_unit = (r4 // C) == (c4 // C)
        strict = jnp.logical_and(same_unit, r4 > c4)
        incl = jnp.logical_and(same_unit, r4 >= c4)

        def halves(x):
            return x[g * G2:g * G2 + C], x[g * G2 + C:(g + 1) * G2]

        def stack(x):
            x0, x1 = halves(x)
            return jnp.concatenate([x0 * m0, x0 * m1, x1 * m0, x1 * m1], axis=0)

        def dup(x):
            x0, x1 = halves(x)
            return jnp.concatenate([x0, x0, x1, x1], axis=0)

        xa, xr = stack(a["at"]), stack(a["rt"])
        v_st, b_st, k_st = stack(a["v"]), stack(a["bh"]), stack(a["kh"])
        sc = _dot_nt(jnp.concatenate([xa, xr], axis=0),
                     jnp.concatenate([dup(a["bt"]), dup(a["kt"])], axis=0))
        yield
        T = yield from _inv_unit_lower(jnp.where(strict, -sc[:SG, :SG], 0.0), SG, hd)
        av = _dot(jnp.where(strict, sc[:SG, SG:], 0.0), v_st)
        wt = _dot(T, xa)
        yield
        ut = _dot(T, av)
        ork = _dot(jnp.where(incl, sc[SG:, SG:], 0.0), v_st)
        yield
        hk = [_dot_tn(v_st[i * G2:(i + 1) * G2], k_st[i * G2:(i + 1) * G2]) for i in range(2)]
        out[g] = dict(wt=wt.astype(BF16), xr=xr.astype(BF16), b_st=b_st.astype(BF16), ut=ut, ork=ork,
                      hk=hk, a_rb=jnp.where(incl, sc[SG:, :SG], 0.0).astype(BF16))

    ws = lax.rem(t, 2)
    rs = 1 - ws

    def store(out):
        for g in range(n_groups):
            o = out[g]
            pwx_ref[ws, g, 0], pwx_ref[ws, g, 1], pbs_ref[ws, g] = o["wt"], o["xr"], o["b_st"]
            puo_ref[ws, g, 0], puo_ref[ws, g, 1] = o["ut"], o["ork"]
            phk_ref[ws, 2 * g], phk_ref[ws, 2 * g + 1] = o["hk"]
            parb_ref[ws, g] = o["a_rb"]
        pvec_ref[ws, 0], pvec_ref[ws, 1], pvec_ref[ws, 2] = out["e_tot"], out["bonus"], out["gate"]

    def load():
        p = {g: dict(wt=pwx_ref[rs, g, 0], xr=pwx_ref[rs, g, 1], b_st=pbs_ref[rs, g],
                     ut=puo_ref[rs, g, 0], ork=puo_ref[rs, g, 1],
                     hk=[phk_ref[rs, 2 * g], phk_ref[rs, 2 * g + 1]], a_rb=parb_ref[rs, g])
             for g in range(n_groups)}
        p.update(e_tot=pvec_ref[rs, 0], bonus=pvec_ref[rs, 1], gate=pvec_ref[rs, 2])
        return p

    def phase_c(p):
        H = h_ref[...]
        us, xrs = [], []
        for ci in range(tb // C):
            gr = p[ci // 2]
            rows = slice((ci % 2) * G2, (ci % 2 + 1) * G2)
            xh = _dot_nt(jnp.concatenate([gr["wt"][rows], gr["xr"][rows]], axis=0), H)
            yield
            u_c = xh[:G2] + gr["ut"][rows]
            H = (H * p["e_tot"][ci * C:ci * C + 1, :] + _dot_tn(u_c, gr["b_st"][rows])
                 + gr["hk"][ci % 2])
            us.append(u_c)
            xrs.append(xh[G2:])
            yield
        h_ref[...] = H
        ys = []
        for g in range(n_groups):
            o_st = (jnp.concatenate(xrs[2 * g:2 * g + 2], axis=0)
                    + _dot(p[g]["a_rb"], jnp.concatenate(us[2 * g:2 * g + 2], axis=0)) + p[g]["ork"])
            ys += [o_st[0:C] + o_st[C:2 * C], o_st[2 * C:3 * C] + o_st[3 * C:4 * C]]
            yield
        y = jnp.concatenate(ys, axis=0)
        mu = _dot_sel(bd_ones, y, 2, left=False) * (1.0 / hd)
        yield
        yc = y - mu
        var = _dot_sel(bd_ones, yc * yc, 2, left=False) * (1.0 / hd)
        yield
        y = yc * lax.rsqrt(var + RWKV_GN_EPS) * prow("ln_w") + prow("ln_b") + p["bonus"]
        o_ref[...] = (y * p["gate"]).astype(o_ref.dtype)

    return (lambda out: phase_a({}, out)), phase_c, load, store


def _rwkv_branch(z, pv, musm, wup, aup, gup, *, batch, seq, width, hd, cb_r, cb_sm, tb):
    n = batch * seq
    nt = seq // tb
    hp = width // LANE
    nsub = SEQS_PER_STEP if hp % SEQS_PER_STEP == 0 else 1
    wide = nsub * LANE
    row_in = lambda b, t: b * nt + jnp.minimum(t, nt - 1)
    row_out = lambda b, t: b * nt + jnp.maximum(t - 1, 0)
    zspec = lambda cb: pl.BlockSpec((tb, wide), lambda b, p, t: (row_in(b, t), cb // nsub + p))
    smspec = lambda i: pl.BlockSpec((tb, LANE), lambda b, p, t: (row_in(b, t), cb_sm + i))
    colblk = lambda rows: pl.BlockSpec((rows, wide), lambda b, p, t: (0, p))
    kern = functools.partial(_rwkv_kernel, tb=tb, nt=nt, nsub=nsub, hd=hd)
    ng, sg = tb // (2 * CHUNK), 4 * CHUNK
    return pl.pallas_call(
        kern,
        grid=(batch, hp // nsub, nt + 1),
        in_specs=[zspec(cb_r), zspec(cb_r + hp), zspec(cb_r + 2 * hp),
                  smspec(0), smspec(1), smspec(2), smspec(3),
                  colblk(pv.shape[0]),
                  pl.BlockSpec(musm.shape, lambda b, p, t: (0, 0)),
                  colblk(LANE), colblk(LANE), colblk(2 * LANE)],
        out_specs=pl.BlockSpec((tb, wide), lambda b, p, t: (row_out(b, t), p)),
        out_shape=jax.ShapeDtypeStruct((n, width), BF16),
        scratch_shapes=[pltpu.VMEM((nsub, LANE, LANE), F32), pltpu.VMEM((nsub, 7, 8, LANE), F32),
                        pltpu.VMEM((nsub, 2, ng, 2, sg, LANE), BF16),
                        pltpu.VMEM((nsub, 2, ng, sg, LANE), BF16),
                        pltpu.VMEM((nsub, 2, ng, 2, sg, LANE), F32),
                        pltpu.VMEM((nsub, 2, 2 * ng, LANE, LANE), F32),
                        pltpu.VMEM((nsub, 2, ng, sg, sg), BF16), pltpu.VMEM((nsub, 2, 3, tb, LANE), F32)],
        compiler_params=_cparams(("parallel", "parallel", "arbitrary")),
        name="rwkv7",
    )(z, z, z, z, z, z, z, pv, musm, wup, aup, gup)


class _Layout:
    def __init__(self, d_model, gdn_heads, rwkv_width, decay_lora, aaa_lora, gate_lora, tn):
        self.d_model, self.gh, self.rw = d_model, gdn_heads, rwkv_width
        self.dl, self.al, self.gl = decay_lora, aaa_lora, gate_lora
        self.gw = gdn_heads * LANE
        assert decay_lora + 2 * gdn_heads <= LANE and aaa_lora <= LANE and gate_lora == 2 * LANE
        self.a_lane0 = decay_lora
        self.b_lane0 = decay_lora + gdn_heads
        self.cb_q = 0
        self.cb_gz = 3 * self.gw // LANE
        self.cb_r = 0
        self.cb_sm = 3 * rwkv_width // LANE
        self.off_z = 3 * self.gw
        self.off_a = self.off_z + self.gw
        self.off_b = self.off_a + gdn_heads
        self.off_rwkv = self.off_b + gdn_heads
        self.off_gates = self.off_rwkv + 3 * rwkv_width + decay_lora + aaa_lora + gate_lora

    def split_cols(self, w):
        lead = w.shape[:-1]
        zeros = lambda n: jnp.zeros(lead + (n,), w.dtype)
        rw, dl, al, gl, gh = self.rw, self.dl, self.al, self.gl, self.gh
        o = self.off_rwkv + 3 * rw
        piece_b = jnp.concatenate(
            [w[..., self.off_rwkv:o],
             w[..., o:o + dl], w[..., self.off_a:self.off_b], w[..., self.off_b:self.off_rwkv],
             zeros(LANE - dl - 2 * gh),
             w[..., o + dl:o + dl + al], zeros(LANE - al),
             w[..., o + dl + al:o + dl + al + gl]], axis=-1)
        return w[..., :self.off_a], piece_b, w[..., self.off_gates:]


def _pad_rows(w, rows):
    return jnp.concatenate([w, jnp.zeros((rows - w.shape[0],) + w.shape[1:], w.dtype)], axis=0)


def _gdn_from_packed(za, zb, p, lay, batch, seq, tb=256):
    lane_vec = lambda v, lane0: jnp.zeros((1, LANE), F32).at[0, lane0:lane0 + v.shape[0]].set(v)
    return _gdn_branch(za, zb, p["conv_w"], lane_vec(p["gdn_a_log"], lay.a_lane0),
                       lane_vec(p["gdn_dt_bias"], lay.a_lane0), p["gdn_onorm_g"].reshape(1, LANE),
                       batch=batch, seq=seq, heads=lay.gh, cb_q=lay.cb_q, cb_gz=lay.cb_gz,
                       cb_sm=lay.cb_sm, a_lane0=lay.a_lane0, b_lane0=lay.b_lane0, tb=min(tb, seq))


def _rwkv_from_packed(z, p, lay, batch, seq, tb=256):
    rw = lay.rw
    hd = p["rwkv_r_k"].shape[-1]
    assert LANE % hd == 0 and hd * 2 == LANE
    mu = p["rwkv_mu"]
    rows = {"w0": p["rwkv_w0"], "a0": p["rwkv_a0"], "k_k": p["rwkv_k_k"], "k_a": p["rwkv_k_a"],
            "r_k": p["rwkv_r_k"].reshape(rw), "ln_w": p["rwkv_ln_w"], "ln_b": p["rwkv_ln_b"],
            "mu_r": mu[:rw], "mu_k": mu[rw:2 * rw], "mu_v": mu[2 * rw:3 * rw]}
    pv = _pad_rows(jnp.stack([rows[k] for k in _RW_ROWS]).astype(F32), 16)
    o = 3 * rw
    zl = lambda n: jnp.zeros((n,), F32)
    musm = jnp.concatenate([mu[o:o + lay.dl], zl(LANE - lay.dl), mu[o + lay.dl:o + lay.dl + lay.al],
                            zl(LANE - lay.al), mu[o + lay.dl + lay.al:]])
    musm = _pad_rows(musm.reshape(1, 4 * LANE), 8)
    wup = _pad_rows(p["rwkv_w_up"], LANE).astype(BF16)
    aup = _pad_rows(p["rwkv_a_up"], LANE).astype(BF16)
    gup = p["rwkv_g_up"].astype(BF16)
    return _rwkv_branch(z, pv, musm, wup, aup, gup, batch=batch, seq=seq, width=rw, hd=hd,
                        cb_r=lay.cb_r, cb_sm=lay.cb_sm, tb=min(tb, seq))


def _make_layout(p, tn):
    return _Layout(p["norm1_g"].shape[-1], p["gdn_a_log"].shape[-1], p["rwkv_w0"].shape[-1],
                   p["rwkv_w_up"].shape[0], p["rwkv_a_up"].shape[0], p["rwkv_g_up"].shape[0], tn)


def _ada_kernel(c_ref, w_ref, b_ref, o_ref):
    o_ref[...] = _dot(_silu(c_ref[...]), w_ref[...]) + b_ref[...]


def _ada_mod(c_pad, w_ada, b_ada, tn):
    d, cols = w_ada.shape
    return pl.pallas_call(
        _ada_kernel,
        grid=(cols // tn,),
        in_specs=[pl.BlockSpec((8, d), lambda j: (0, 0)),
                  pl.BlockSpec((d, tn), lambda j: (0, j)),
                  pl.BlockSpec((1, tn), lambda j: (0, j))],
        out_specs=pl.BlockSpec((8, tn), lambda j: (0, j)),
        out_shape=jax.ShapeDtypeStruct((8, cols), F32),
        compiler_params=_cparams(("arbitrary",)),
        name="ada_mod",
    )(c_pad, w_ada, b_ada.reshape(1, cols))


def _modnorm(x, g, sc, sh):
    y = x * lax.rsqrt(jnp.mean(x * x, axis=-1, keepdims=True) + RMS_EPS)
    return y * g * (1.0 + sc) + sh


def _norm1_kernel(x_ref, g_ref, sh_ref, sc_ref, o_ref):
    b = pl.program_id(0)
    h = _modnorm(x_ref[...], g_ref[...], sc_ref[pl.ds(b, 1), :], sh_ref[pl.ds(b, 1), :])
    o_ref[...] = h.astype(o_ref.dtype)


def _norm1(x2, g, mod, *, batch, seq, tm):
    n, d = x2.shape
    nt = seq // tm
    return pl.pallas_call(
        _norm1_kernel,
        grid=(batch, nt),
        in_specs=[pl.BlockSpec((tm, d), lambda b, t: (b * nt + t, 0)),
                  pl.BlockSpec((1, d), lambda b, t: (0, 0)),
                  pl.BlockSpec((8, d), lambda b, t: (0, 0)),
                  pl.BlockSpec((8, d), lambda b, t: (0, 1))],
        out_specs=pl.BlockSpec((tm, d), lambda b, t: (b * nt + t, 0)),
        out_shape=jax.ShapeDtypeStruct((n, d), BF16),
        compiler_params=_cparams(("parallel", "arbitrary")),
        name="norm1",
    )(x2, g, mod, mod)


def _merge_kernel(o_ref, y_ref, wa_ref, wb_ref, ga_ref, gb_ref, m_ref):
    ha = jnp.dot(o_ref[...], wa_ref[...], preferred_element_type=F32)
    hb = jnp.dot(y_ref[...], wb_ref[...], preferred_element_type=F32)
    m_ref[...] = (ga_ref[...].astype(F32) * ha + gb_ref[...].astype(F32) * hb).astype(m_ref.dtype)


def _merge(o, y, wa, wb, gates, *, tm, tn):
    n, d = o.shape[0], wa.shape[1]
    tn = math.gcd(tn, d)
    gb0 = 0
    return pl.pallas_call(
        _merge_kernel,
        grid=(n // tm, d // tn),
        in_specs=[pl.BlockSpec((tm, o.shape[1]), lambda i, j: (i, 0)),
                  pl.BlockSpec((tm, y.shape[1]), lambda i, j: (i, 0)),
                  pl.BlockSpec((wa.shape[0], tn), lambda i, j: (0, j)),
                  pl.BlockSpec((wb.shape[0], tn), lambda i, j: (0, j)),
                  pl.BlockSpec((tm, tn), lambda i, j: (i, gb0 + j)),
                  pl.BlockSpec((tm, tn), lambda i, j: (i, gb0 + d // tn + j))],
        out_specs=pl.BlockSpec((tm, tn), lambda i, j: (i, j)),
        out_shape=jax.ShapeDtypeStruct((n, d), BF16),
        compiler_params=_cparams(("parallel", "arbitrary")),
        name="merge",
    )(o, y, wa, wb, gates, gates)


def _outproj_kernel(m_ref, w_ref, x_ref, g_ref, o_ref, *, tiles_per_batch):
    b = pl.program_id(0) // tiles_per_batch
    y = jnp.dot(m_ref[...], w_ref[...], preferred_element_type=F32)
    o_ref[...] = x_ref[...] + g_ref[pl.ds(b, 1), :] * y


def _outproj(m, w, x2, mod, *, seq, tm, tn):
    n, d = x2.shape
    gcol = 2 * d // tn
    return pl.pallas_call(
        functools.partial(_outproj_kernel, tiles_per_batch=seq // tm),
        grid=(n // tm, d // tn),
        in_specs=[pl.BlockSpec((tm, d), lambda i, j: (i, 0)),
                  pl.BlockSpec((d, tn), lambda i, j: (0, j)),
                  pl.BlockSpec((tm, tn), lambda i, j: (i, j)),
                  pl.BlockSpec((8, tn), lambda i, j: (0, gcol + j))],
        out_specs=pl.BlockSpec((tm, tn), lambda i, j: (i, j)),
        out_shape=jax.ShapeDtypeStruct((n, d), F32),
        compiler_params=_cparams(("parallel", "arbitrary")),
        name="outproj",
    )(m, w, x2, mod)


def _router_kernel(x_ref, g_ref, sh_ref, sc_ref, wr_ref, br_ref, h_ref, id_ref, wt_ref, cnt_out_ref,
                   cnt_ref, *, n_groups, per_group):
    b = pl.program_id(0)
    h = _modnorm(x_ref[...], g_ref[...], sc_ref[pl.ds(b, 1), :], sh_ref[pl.ds(b, 1), :])
    h_ref[...] = h
    logits = _dot(h, wr_ref[...]) + br_ref[...]
    tm = logits.shape[0]
    lane = lax.broadcasted_iota(jnp.int32, (tm, LANE), 1)
    neg = jnp.float32(-jnp.inf)
    big = jnp.int32(LANE)

    def first_argmax(vals, mask):
        vm = jnp.where(mask, vals, neg)
        mx = jnp.max(vm, axis=1, keepdims=True)
        idx = jnp.min(jnp.where(jnp.logical_and(mask, vm == mx), lane, big), axis=1, keepdims=True)
        return mx, idx

    gmask = lane < n_groups
    gmax, g_sel = first_argmax(logits, gmask)
    g_prob = 1.0 / jnp.sum(jnp.where(gmask, jnp.exp(logits - gmax), 0.0), axis=1, keepdims=True)
    e_lane = lane - n_groups
    emask = jnp.logical_and(e_lane >= g_sel * per_group, e_lane < (g_sel + 1) * per_group)
    emax = jnp.max(jnp.where(emask, logits, neg), axis=1, keepdims=True)
    ex = jnp.where(emask, jnp.exp(logits - emax), 0.0)
    probs = ex / jnp.sum(ex, axis=1, keepdims=True)
    p0, i0 = first_argmax(probs, emask)
    p1, i1 = first_argmax(probs, jnp.logical_and(emask, lane != i0))
    den = p0 + p1
    wt_ref[...] = jnp.where(lane == 0, g_prob * (p0 / den), jnp.where(lane == 1, g_prob * (p1 / den), 0.0))

    @pl.when(jnp.logical_and(b == 0, pl.program_id(1) == 0))
    def _():
        cnt_ref[...] = jnp.zeros_like(cnt_ref)

    e0, e1 = i0 - n_groups, i1 - n_groups
    hit0, hit1 = lane == e0, lane == e1
    hits = jnp.logical_or(hit0, hit1).astype(BF16)
    rr = lax.broadcasted_iota(jnp.int32, (tm, tm), 0)
    cc = lax.broadcasted_iota(jnp.int32, (tm, tm), 1)
    before = jnp.dot((rr > cc).astype(BF16), hits, preferred_element_type=F32) + cnt_ref[0:1, :]
    rank0 = jnp.sum(jnp.where(hit0, before, 0.0), axis=1, keepdims=True).astype(jnp.int32)
    rank1 = jnp.sum(jnp.where(hit1, before, 0.0), axis=1, keepdims=True).astype(jnp.int32)
    cnt_ref[0:1, :] = cnt_ref[0:1, :] + jnp.sum(hits.astype(F32), axis=0, keepdims=True)
    id_ref[...] = jnp.where(lane == 0, e0, jnp.where(lane == 1, e1,
                            jnp.where(lane == 2, rank0, jnp.where(lane == 3, rank1, 0))))
    cnt_out_ref[...] = cnt_ref[...].astype(jnp.int32)


def _router(x1, g, mod, wr, br, *, batch, seq, tm, n_groups, per_group):
    n, d = x1.shape
    nt = seq // tm
    rowblk = lambda w: pl.BlockSpec((tm, w), lambda b, t: (b * nt + t, 0))
    return pl.pallas_call(
        functools.partial(_router_kernel, n_groups=n_groups, per_group=per_group),
        grid=(batch, nt),
        in_specs=[rowblk(d),
                  pl.BlockSpec((1, d), lambda b, t: (0, 0)),
                  pl.BlockSpec((8, d), lambda b, t: (0, 3)),
                  pl.BlockSpec((8, d), lambda b, t: (0, 4)),
                  pl.BlockSpec((d, LANE), lambda b, t: (0, 0)),
                  pl.BlockSpec((1, LANE), lambda b, t: (0, 0))],
        out_specs=[rowblk(d), rowblk(LANE), rowblk(LANE), pl.BlockSpec((8, LANE), lambda b, t: (0, 0))],
        out_shape=[jax.ShapeDtypeStruct((n, d), F32), jax.ShapeDtypeStruct((n, LANE), jnp.int32),
                   jax.ShapeDtypeStruct((n, LANE), F32), jax.ShapeDtypeStruct((8, LANE), jnp.int32)],
        scratch_shapes=[pltpu.VMEM((8, LANE), F32)],
        compiler_params=_cparams(("arbitrary", "arbitrary")),
        name="router",
    )(x1, g, mod, mod, wr, br)


def _expert_kernel(be_ref, tok_ref, h_hbm, w1_ref, w3_ref, w2_ref, bw_ref, o_ref, xbuf, sem):
    i = pl.program_id(0)
    nb = pl.num_programs(0)

    def row_copy(blk, slot, r):
        tok = tok_ref[blk * MOE_BLOCK + r]
        return pltpu.make_async_copy(h_hbm.at[pl.ds(tok, 1), :], xbuf.at[slot, pl.ds(r, 1), :],
                                     sem.at[slot])

    def start_block(blk, slot):
        def body(r, carry):
            row_copy(blk, slot, r).start()
            return carry
        lax.fori_loop(0, MOE_BLOCK, body, 0, unroll=DMA_UNROLL)

    @pl.when(i == 0)
    def _():
        start_block(0, 0)

    @pl.when(i + 1 < nb)
    def _():
        start_block(i + 1, (i + 1) % 2)

    slot = i % 2

    def wait_body(r, carry):
        row_copy(i, slot, r).wait()
        return carry
    lax.fori_loop(0, MOE_BLOCK, wait_body, 0, unroll=DMA_UNROLL)

    x = xbuf[slot]
    a = _dot(x, w1_ref[...])
    g = _dot(x, w3_ref[...])
    y = _dot(_silu(a) * g, w2_ref[...])
    o_ref[...] = y * bw_ref[...]


def _experts(h2, w1, w3, w2, block_expert, buf_tok, buf_w):
    d = h2.shape[1]
    de = w1.shape[2]
    p_rows = buf_tok.shape[0]
    wspec = lambda shape: pl.BlockSpec((None,) + shape, lambda i, be, tok: (be[i], 0, 0))
    grid_spec = pltpu.PrefetchScalarGridSpec(
        num_scalar_prefetch=2,
        grid=(p_rows // MOE_BLOCK,),
        in_specs=[pl.BlockSpec(memory_space=pl.ANY),
                  wspec((d, de)), wspec((d, de)), wspec((de, d)),
                  pl.BlockSpec((MOE_BLOCK, 1), lambda i, be, tok: (i, 0))],
        out_specs=pl.BlockSpec((MOE_BLOCK, d), lambda i, be, tok: (i, 0)),
        scratch_shapes=[pltpu.VMEM((2, MOE_BLOCK, d), F32), pltpu.SemaphoreType.DMA((2,))],
    )
    return pl.pallas_call(
        _expert_kernel,
        grid_spec=grid_spec,
        out_shape=jax.ShapeDtypeStruct((p_rows, d), F32),
        compiler_params=_cparams(("arbitrary",)),
        name="experts",
    )(block_expert, buf_tok, h2, w1, w3, w2, buf_w)


def _combine_kernel(pos_ref, x_ref, g_ref, gf_ref, y_hbm, o_ref, ybuf, sem, *, tm, tiles_per_batch):
    i = pl.program_id(0)
    nb = pl.num_programs(0)
    rows = TOP_K * tm

    def row_copy(blk, slot, r):
        return pltpu.make_async_copy(y_hbm.at[pl.ds(pos_ref[blk * rows + r], 1), :],
                                     ybuf.at[slot, pl.ds(r, 1), :], sem.at[slot])

    def start_block(blk, slot):
        def body(r, carry):
            row_copy(blk, slot, r).start()
            return carry
        lax.fori_loop(0, rows, body, 0, unroll=DMA_UNROLL)

    @pl.when(i == 0)
    def _():
        start_block(0, 0)

    @pl.when(i + 1 < nb)
    def _():
        start_block(i + 1, (i + 1) % 2)

    slot = i % 2

    def wait_body(r, carry):
        row_copy(i, slot, r).wait()
        return carry
    lax.fori_loop(0, rows, wait_body, 0, unroll=DMA_UNROLL)

    b = i // tiles_per_batch
    moe = ybuf[slot, 0:tm, :] + ybuf[slot, tm:rows, :]
    x = x_ref[...] + g_ref[pl.ds(b, 1), :] * moe
    o_ref[...] = x * lax.rsqrt(jnp.mean(x * x, axis=-1, keepdims=True) + RMS_EPS) * gf_ref[...]


def _combine(pos, x1, mod, gf, yb, *, seq, tm):
    n, d = x1.shape
    grid_spec = pltpu.PrefetchScalarGridSpec(
        num_scalar_prefetch=1,
        grid=(n // tm,),
        in_specs=[pl.BlockSpec((tm, d), lambda i, pos: (i, 0)),
                  pl.BlockSpec((8, d), lambda i, pos: (0, 5)),
                  pl.BlockSpec((1, d), lambda i, pos: (0, 0)),
                  pl.BlockSpec(memory_space=pl.ANY)],
        out_specs=pl.BlockSpec((tm, d), lambda i, pos: (i, 0)),
        scratch_shapes=[pltpu.VMEM((2, TOP_K * tm, d), F32), pltpu.SemaphoreType.DMA((2,))],
    )
    return pl.pallas_call(
        functools.partial(_combine_kernel, tm=tm, tiles_per_batch=seq // tm),
        grid_spec=grid_spec,
        out_shape=jax.ShapeDtypeStruct((n, d), F32),
        compiler_params=_cparams(("arbitrary",)),
        name="combine",
    )(pos, x1, mod, gf, yb)


def _dispatch(ids, rank, wts, counts):
    n = ids.shape[0]
    n_experts = counts.shape[0]
    a = n * TOP_K
    expert_id = ids.reshape(a)
    w_flat = wts.reshape(a)
    order = jnp.argsort(expert_id).astype(jnp.int32)
    ei = jnp.arange(n_experts, dtype=jnp.int32)
    upto = ei[None, :] <= ei[:, None]
    padded = (counts + MOE_BLOCK - 1) // MOE_BLOCK * MOE_BLOCK
    c_start = jnp.sum(jnp.where(upto, counts[None, :], 0), axis=1) - counts
    p_end = jnp.sum(jnp.where(upto, padded[None, :], 0), axis=1)
    p_start = p_end - padded
    pos = rank.reshape(a) + p_start[expert_id]
    p_rows = (a + n_experts * (MOE_BLOCK - 1) + MOE_BLOCK - 1) // MOE_BLOCK * MOE_BLOCK
    n_blocks = p_rows // MOE_BLOCK
    starts = jnp.arange(n_blocks, dtype=jnp.int32) * MOE_BLOCK
    block_expert = jnp.minimum(jnp.sum((p_end[None, :] <= starts[:, None]).astype(jnp.int32), axis=1),
                               n_experts - 1)
    slot = jnp.arange(p_rows, dtype=jnp.int32)
    slot_expert = jnp.repeat(block_expert, MOE_BLOCK)
    within = slot - p_start[slot_expert]
    valid = within < counts[slot_expert]
    src = order[jnp.clip(within + c_start[slot_expert], 0, a - 1)]
    buf_tok = jnp.where(valid, src // TOP_K, 0)
    buf_w = jnp.where(valid, w_flat[src], 0.0)
    return pos, buf_tok, buf_w, block_expert


def _pick(n, pref):
    t = min(n, pref)
    while n % t:
        t //= 2
    return t


def kernel(x, c, w_ada, b_ada, norm1_g, w_in, conv_w, gdn_a_log, gdn_dt_bias, gdn_onorm_g, rwkv_mu,
           rwkv_w0, rwkv_w_up, rwkv_a0, rwkv_a_up, rwkv_g_up, rwkv_k_k, rwkv_k_a, rwkv_r_k, rwkv_ln_w,
           rwkv_ln_b, w_gdn_o, w_rwkv_o, w_out, norm2_g, w_group, b_group, w_expert, b_expert, w1, w3,
           w2, norm_f_g):
    batch, seq, d = x.shape
    n = batch * seq
    depth = w_ada.shape[0]
    assert depth == 1, "the final norm is fused into the last layer's combine"
    assert batch <= 8 and seq % CHUNK == 0 and d % 512 == 0
    x2 = x.reshape(n, d)
    c_pad = _pad_rows(c, 8)
    tm_big = _pick(seq, 1024)
    tm_row = _pick(seq, 256)
    tn = 512
    for l in range(depth):
        p = {"norm1_g": norm1_g[l], "conv_w": conv_w[l], "gdn_a_log": gdn_a_log[l],
             "gdn_dt_bias": gdn_dt_bias[l], "gdn_onorm_g": gdn_onorm_g[l], "rwkv_mu": rwkv_mu[l],
             "rwkv_w0": rwkv_w0[l], "rwkv_w_up": rwkv_w_up[l], "rwkv_a0": rwkv_a0[l],
             "rwkv_a_up": rwkv_a_up[l], "rwkv_g_up": rwkv_g_up[l], "rwkv_k_k": rwkv_k_k[l],
             "rwkv_k_a": rwkv_k_a[l], "rwkv_r_k": rwkv_r_k[l], "rwkv_ln_w": rwkv_ln_w[l],
             "rwkv_ln_b": rwkv_ln_b[l]}
        lay = _make_layout(p, tn)
        mod = _ada_mod(c_pad, w_ada[l], b_ada[l], tn)

        h1 = _norm1(x2, norm1_g[l].reshape(1, d), mod, batch=batch, seq=seq, tm=tm_row)
        w_a, w_b, w_c = lay.split_cols(w_in[l])
        za = _matmul(h1, w_a.astype(BF16), tm_big, tn, F32, name="in_proj_gdn")
        zb = _matmul(h1, w_b.astype(BF16), tm_big, tn, F32, name="in_proj_rwkv")
        gates = _matmul(h1, w_c.astype(BF16), tm_big, tn, BF16, act=_sigmoid, name="in_proj_gates")
        o_gdn = _gdn_from_packed(za, zb, p, lay, batch, seq)
        y_rwkv = _rwkv_from_packed(zb, p, lay, batch, seq)
        m = _merge(o_gdn, y_rwkv, w_gdn_o[l].astype(BF16), w_rwkv_o[l].astype(BF16), gates,
                   tm=tm_big, tn=tn)
        x1 = _outproj(m, w_out[l].astype(BF16), x2, mod, seq=seq, tm=tm_big, tn=tn)

        n_groups, n_experts = w_group.shape[-1], w_expert.shape[-1]
        assert n_groups + n_experts <= LANE
        wr = jnp.concatenate([w_group[l], w_expert[l],
                              jnp.zeros((d, LANE - n_groups - n_experts), F32)], axis=1).astype(BF16)
        br = jnp.concatenate([b_group[l], b_expert[l],
                              jnp.zeros((LANE - n_groups - n_experts,), F32)]).reshape(1, LANE)
        h2, ids, wts, cnt = _router(x1, norm2_g[l].reshape(1, d), mod, wr, br, batch=batch, seq=seq,
                                    tm=tm_row, n_groups=n_groups, per_group=n_experts // n_groups)
        pos, buf_tok, buf_w, block_expert = _dispatch(ids[:, :TOP_K], ids[:, TOP_K:2 * TOP_K],
                                                      wts[:, :TOP_K], cnt[0, :n_experts])
        yb = _experts(h2, w1[l].astype(BF16), w3[l].astype(BF16), w2[l].astype(BF16), block_expert,
                      buf_tok, buf_w.reshape(-1, 1))
        tm_c = _pick(seq, 128)
        pos_tiles = pos.reshape(n // tm_c, tm_c, TOP_K).transpose(0, 2, 1).reshape(n * TOP_K)
        x2 = _combine(pos_tiles, x1, mod, norm_f_g.reshape(1, d), yb, seq=seq, tm=tm_c)
    return x2.reshape(batch, seq, d)
```

```python
import functools
import math

import jax
import jax.numpy as jnp
from jax import lax
from jax.experimental import pallas as pl
from jax.experimental.pallas import tpu as pltpu

F32 = jnp.float32
BF16 = jnp.bfloat16

LANE = 128
CHUNK = 64
CONV_WIDTH = 4
RMS_EPS = 1e-6
RWKV_GN_EPS = 64e-5
MOE_BLOCK = 128
TOP_K = 2
DMA_UNROLL = 8
SEQS_PER_STEP = 4
VMEM_LIMIT = 56 * 1024 * 1024


def _cparams(sem):
    return pltpu.CompilerParams(dimension_semantics=sem, vmem_limit_bytes=VMEM_LIMIT)


def _dot(a, b):
    return jnp.dot(a.astype(BF16), b.astype(BF16), preferred_element_type=F32)


def _dot_nt(a, b):
    return lax.dot_general(a.astype(BF16), b.astype(BF16), (((1,), (1,)), ((), ())),
                           preferred_element_type=F32)


def _dot_tn(a, b):
    return lax.dot_general(a.astype(BF16), b.astype(BF16), (((0,), (0,)), ((), ())),
                           preferred_element_type=F32)


def _split(x, terms):
    pieces = []
    for _ in range(terms - 1):
        hi = x.astype(BF16)
        pieces.append(hi)
        x = x - hi.astype(F32)
    pieces.append(x.astype(BF16))
    return pieces


def _dot_sel(sel, x, terms, left=True):
    out = None
    for piece in _split(x, terms):
        ops = (sel, piece) if left else (piece, sel)
        d = jnp.dot(*ops, preferred_element_type=F32)
        out = d if out is None else out + d
    return out


def _sigmoid(x):
    return 1.0 / (1.0 + jnp.exp(-x))


def _silu(x):
    return x * _sigmoid(x)


def _softplus(x):
    return jnp.maximum(x, 0.0) + jnp.log(1.0 + jnp.exp(-jnp.abs(x)))


def _inv_unit_lower(L, n, top):
    r = lax.broadcasted_iota(jnp.int32, (n, n), 0)
    c = lax.broadcasted_iota(jnp.int32, (n, n), 1)
    eye = (r == c).astype(F32)
    same16 = (r // 16) == (c // 16)
    Ld = jnp.where(same16, L, 0.0)
    X = eye - Ld
    P = _dot(Ld, Ld)
    yield
    for _ in range(2):
        X = X + _dot(X, P)
        P = _dot(P, P)
        yield
    X = X + _dot(X, P)
    yield
    bs = 32
    while bs <= top:
        inner = (r // (bs // 2)) == (c // (bs // 2))
        outer = (r // bs) == (c // bs)
        Lo = jnp.where(jnp.logical_and(outer, jnp.logical_not(inner)), L, 0.0)
        Y = _dot(Lo, X)
        yield
        X = X - _dot(X, Y)
        yield
        bs *= 2
    return X


def _round_robin(gens):
    gens = list(gens)
    while gens:
        for g in list(gens):
            try:
                more = next(g)
            except StopIteration:
                gens.remove(g)
                continue
            if more:
                gens.extend(more)


def _mm_kernel(a_ref, w_ref, *refs, act, n_cast):
    cast_in, o_ref, cast_out = refs[:n_cast], refs[n_cast], refs[n_cast + 1:]
    y = jnp.dot(a_ref[...], w_ref[...], preferred_element_type=F32)
    o_ref[...] = (act(y) if act else y).astype(o_ref.dtype)
    for src, dst in zip(cast_in, cast_out):
        dst[...] = src[...].astype(dst.dtype)


def _matmul(a, w, tm, tn, out_dtype, act=None, name="matmul", cast=()):
    M, K = a.shape
    Nn = w.shape[1]
    tn = math.gcd(tn, Nn)
    nj = Nn // tn
    steps = (M // tm) * nj
    cast = [c for c in cast]
    assert all(c.shape[0] % (16 * steps) == 0 for c in cast)
    slab = lambda c: pl.BlockSpec((c.shape[0] // steps, c.shape[1]), lambda i, j: (i * nj + j, 0))
    outs = pl.pallas_call(
        functools.partial(_mm_kernel, act=act, n_cast=len(cast)),
        grid=(M // tm, nj),
        in_specs=[pl.BlockSpec((tm, K), lambda i, j: (i, 0)),
                  pl.BlockSpec((K, tn), lambda i, j: (0, j))] + [slab(c) for c in cast],
        out_specs=[pl.BlockSpec((tm, tn), lambda i, j: (i, j))] + [slab(c) for c in cast],
        out_shape=[jax.ShapeDtypeStruct((M, Nn), out_dtype)]
                  + [jax.ShapeDtypeStruct(c.shape, BF16) for c in cast],
        compiler_params=_cparams(("parallel", "arbitrary")),
        name=name,
    )(a, w, *cast)
    return outs if cast else outs[0]


def _run_pipelined(t, nt, heads):
    def prepare():
        outs = [{} for _ in heads]
        return outs, [hd[0](out) for hd, out in zip(heads, outs)]

    def finish(outs):
        for hd, out in zip(heads, outs):
            hd[3](out)

    @pl.when(t == 0)
    def _():
        outs, gens = prepare()
        _round_robin(gens)
        finish(outs)

    @pl.when(jnp.logical_and(t > 0, t < nt))
    def _():
        prevs = [hd[2]() for hd in heads]
        outs, gens = prepare()
        _round_robin([hd[1](p) for hd, p in zip(heads, prevs)] + gens)
        finish(outs)

    @pl.when(t == nt)
    def _():
        _round_robin([hd[1](hd[2]()) for hd in heads])


def _gdn_kernel(zq_ref, zk_ref, zv_ref, gz_ref, sm_ref, cwq_ref, cwk_ref, cwv_ref, alog_ref, dtb_ref,
                on_ref, o_ref, s_ref, xs_ref, pw_ref, pu_ref, pattn_ref, *, nsub, nt, **kw):
    t = pl.program_id(2)

    @pl.when(t == 0)
    def _():
        s_ref[...] = jnp.zeros_like(s_ref)
        xs_ref[:, :, 0:8, :] = jnp.zeros((nsub, 3, 8, LANE), F32)

    lanes = lambda s: slice(s * LANE, (s + 1) * LANE)
    heads = [_gdn_head(pl.program_id(1) * nsub + s, t,
                       zq_ref.at[:, lanes(s)], zk_ref.at[:, lanes(s)], zv_ref.at[:, lanes(s)],
                       gz_ref.at[:, lanes(s)], sm_ref, cwq_ref.at[:, lanes(s)], cwk_ref.at[:, lanes(s)],
                       cwv_ref.at[:, lanes(s)], alog_ref, dtb_ref, on_ref, o_ref.at[:, lanes(s)],
                       s_ref.at[s], xs_ref.at[s], pw_ref.at[s], pu_ref.at[s], pattn_ref.at[s], **kw)
             for s in range(nsub)]
    _run_pipelined(t, nt, heads)


def _gdn_head(h, t, zq_ref, zk_ref, zv_ref, gz_ref, sm_ref, cwq_ref, cwk_ref, cwv_ref, alog_ref, dtb_ref,
              on_ref, o_ref, s_ref, xs_ref, pw_ref, pu_ref, pattn_ref, *, tb, a_lane0, b_lane0, q_scale):
    C = CHUNK

    def conv_silu(idx, z_ref, cw_ref):
        xs_ref[idx, 8:8 + tb, :] = z_ref[...]
        cw = cw_ref[...]
        acc = cw[CONV_WIDTH - 1:CONV_WIDTH, :] * xs_ref[idx, 8:8 + tb, :]
        for i in range(1, CONV_WIDTH):
            acc = acc + cw[CONV_WIDTH - 1 - i:CONV_WIDTH - i, :] * xs_ref[idx, 8 - i:8 - i + tb, :]
        xs_ref[idx, 0:8, :] = xs_ref[idx, tb:tb + 8, :]
        return _silu(acc)

    def l2n(x):
        return x * lax.rsqrt(jnp.sum(x * x, axis=-1, keepdims=True) + 1e-6)

    def phase_ab(out):
        q_all = l2n(conv_silu(0, zq_ref, cwq_ref)) * q_scale
        yield
        k_all = l2n(conv_silu(1, zk_ref, cwk_ref))
        yield
        v_all = conv_silu(2, zv_ref, cwv_ref)
        sm = sm_ref[...]
        lane = lax.broadcasted_iota(jnp.int32, (1, LANE), 1)
        g_full = -jnp.exp(alog_ref[...]) * _softplus(sm + dtb_ref[...])
        g_all = jnp.sum(jnp.where(lane == a_lane0 + h, g_full, 0.0), axis=1, keepdims=True)
        beta_all = jnp.sum(jnp.where(lane == b_lane0 + h, _sigmoid(sm), 0.0), axis=1, keepdims=True)
        yield
        r = lax.broadcasted_iota(jnp.int32, (tb, tb), 0)
        c = lax.broadcasted_iota(jnp.int32, (tb, tb), 1)
        same = (r // C) == (c // C)
        causal = jnp.logical_and(same, r >= c)
        strict = jnp.logical_and(same, r > c)
        gb = jnp.broadcast_to(g_all, (tb, LANE))
        gc_lanes = _dot_sel(causal.astype(BF16), gb, 2)
        yield
        g_tot = _dot_sel(same.astype(BF16), gb, 2)
        yield
        gc_col = jnp.concatenate([gc_lanes] * (tb // LANE), axis=1)
        gc_row = gc_col.T
        decay = jnp.where(causal, jnp.exp(jnp.minimum(gc_col - gc_row, 0.0)), 0.0)
        gc = gc_col[:, 0:1]
        e_gc = jnp.exp(gc)
        kb = k_all * beta_all
        sc = _dot_nt(jnp.concatenate([kb, q_all], axis=0), k_all)
        yield
        L = sc[:tb] * jnp.where(strict, decay, 0.0)
        T = yield from _inv_unit_lower(L, tb, C)
        uw = _dot(T, jnp.concatenate([v_all * beta_all, kb * e_gc], axis=1))
        yield
        out["w"] = uw[:, LANE:].astype(BF16)
        out["q_dec"] = (q_all * e_gc).astype(BF16)
        out["k_dec"] = (k_all * jnp.exp(g_tot[:, 0:1] - gc)).astype(BF16)
        out["u"] = uw[:, :LANE]
        out["e_gl"] = jnp.exp(g_tot[:, :LANE])
        out["gzn"] = on_ref[...] * _silu(gz_ref[...])
        out["attn"] = (sc[tb:] * decay).astype(BF16)

    wslot = lax.rem(t, 2)
    rslot = 1 - wslot

    def store(out):
        pw_ref[wslot, 0], pw_ref[wslot, 1], pw_ref[wslot, 2] = out["w"], out["q_dec"], out["k_dec"]
        pu_ref[wslot, 0], pu_ref[wslot, 1], pu_ref[wslot, 2] = out["u"], out["e_gl"], out["gzn"]
        pattn_ref[wslot] = out["attn"]

    def load():
        return dict(w=pw_ref[rslot, 0], q_dec=pw_ref[rslot, 1], k_dec=pw_ref[rslot, 2],
                    u=pu_ref[rslot, 0], e_gl=pu_ref[rslot, 1], gzn=pu_ref[rslot, 2],
                    attn=pattn_ref[rslot])

    def phase_c(p):
        S = s_ref[...]
        v_news, o_qs = [], []
        for ci in range(tb // C):
            sl = slice(ci * C, (ci + 1) * C)
            ws = _dot(jnp.concatenate([p["w"][sl], p["q_dec"][sl]], axis=0), S)
            yield
            v_new = p["u"][sl] - ws[:C]
            v_news.append(v_new)
            o_qs.append(ws[C:])
            S = S * p["e_gl"][ci * C:ci * C + 1, :] + _dot_tn(p["k_dec"][sl], v_new)
            yield
        s_ref[...] = S
        o = jnp.concatenate(o_qs, axis=0) + _dot(p["attn"], jnp.concatenate(v_news, axis=0))
        yield
        o = o * lax.rsqrt(jnp.mean(o * o, axis=-1, keepdims=True) + RMS_EPS) * p["gzn"]
        o_ref[...] = o.astype(o_ref.dtype)

    return phase_ab, phase_c, load, store


def _gdn_branch(za, zb, conv_w, alog_pad, dtb_pad, onorm_g, *, batch, seq, heads, cb_q, cb_gz, cb_sm,
                a_lane0, b_lane0, tb):
    n = batch * seq
    nt = seq // tb
    nsub = SEQS_PER_STEP if heads % SEQS_PER_STEP == 0 else 1
    wide = nsub * LANE
    row_in = lambda b, t: b * nt + jnp.minimum(t, nt - 1)
    row_out = lambda b, t: b * nt + jnp.maximum(t - 1, 0)
    zspec = lambda cb: pl.BlockSpec((tb, wide), lambda b, h, t: (row_in(b, t), cb // nsub + h))
    cwspec = lambda cb: pl.BlockSpec((CONV_WIDTH, wide), lambda b, h, t: (0, cb // nsub + h))
    vec = pl.BlockSpec((1, LANE), lambda b, h, t: (0, 0))
    kern = functools.partial(_gdn_kernel, tb=tb, nt=nt, nsub=nsub, a_lane0=a_lane0, b_lane0=b_lane0,
                             q_scale=float(LANE) ** -0.5)
    return pl.pallas_call(
        kern,
        grid=(batch, heads // nsub, nt + 1),
        in_specs=[zspec(cb_q), zspec(cb_q + heads), zspec(cb_q + 2 * heads), zspec(cb_gz),
                  pl.BlockSpec((tb, LANE), lambda b, h, t: (row_in(b, t), cb_sm)),
                  cwspec(0), cwspec(heads), cwspec(2 * heads), vec, vec, vec],
        out_specs=pl.BlockSpec((tb, wide), lambda b, h, t: (row_out(b, t), h)),
        out_shape=jax.ShapeDtypeStruct((n, heads * LANE), BF16),
        scratch_shapes=[pltpu.VMEM((nsub, LANE, LANE), F32), pltpu.VMEM((nsub, 3, tb + 8, LANE), F32),
                        pltpu.VMEM((nsub, 2, 3, tb, LANE), BF16), pltpu.VMEM((nsub, 2, 3, tb, LANE), F32),
                        pltpu.VMEM((nsub, 2, tb, tb), BF16)],
        compiler_params=_cparams(("parallel", "parallel", "arbitrary")),
        name="gdn",
    )(za, za, za, za, zb, conv_w, conv_w, conv_w, alog_pad, dtb_pad, onorm_g)


_RW_ROWS = ("w0", "a0", "k_k", "k_a", "r_k", "ln_w", "ln_b", "mu_r", "mu_k", "mu_v")


def _rwkv_kernel(zr_ref, zk_ref, zv_ref, sm0_ref, sm1_ref, sm2_ref, sm3_ref, pv_ref, musm_ref,
                 wup_ref, aup_ref, gup_ref, o_ref, h_ref, c_ref, pwx_ref, pbs_ref, puo_ref, phk_ref,
                 parb_ref, pvec_ref, *, nsub, nt, **kw):
    t = pl.program_id(2)

    @pl.when(t == 0)
    def _():
        h_ref[...] = jnp.zeros_like(h_ref)
        c_ref[...] = jnp.zeros_like(c_ref)

    lanes = lambda s: slice(s * LANE, (s + 1) * LANE)
    heads = [_rwkv_head_pair(t, zr_ref.at[:, lanes(s)], zk_ref.at[:, lanes(s)], zv_ref.at[:, lanes(s)],
                             sm0_ref, sm1_ref, sm2_ref, sm3_ref, pv_ref.at[:, lanes(s)], musm_ref,
                             wup_ref.at[:, lanes(s)], aup_ref.at[:, lanes(s)], gup_ref.at[:, lanes(s)],
                             o_ref.at[:, lanes(s)], h_ref.at[s], c_ref.at[s], pwx_ref.at[s], pbs_ref.at[s],
                             puo_ref.at[s], phk_ref.at[s], parb_ref.at[s], pvec_ref.at[s], **kw)
             for s in range(nsub)]
    _run_pipelined(t, nt, heads)


def _rwkv_head_pair(t, zr_ref, zk_ref, zv_ref, sm0_ref, sm1_ref, sm2_ref, sm3_ref, pv_ref, musm_ref,
                    wup_ref, aup_ref, gup_ref, o_ref, h_ref, c_ref, pwx_ref, pbs_ref, puo_ref, phk_ref,
                    parb_ref, pvec_ref, *, tb, hd):
    C = CHUNK
    G2, SG = 2 * C, 4 * C
    n_groups = tb // G2

    pv = pv_ref[...]
    prow = lambda name: pv[_RW_ROWS.index(name):_RW_ROWS.index(name) + 1, :]
    rr = lax.broadcasted_iota(jnp.int32, (LANE, LANE), 0)
    cc = lax.broadcasted_iota(jnp.int32, (LANE, LANE), 1)
    bd_ones = ((rr // hd) == (cc // hd)).astype(BF16)
    lane = lax.broadcasted_iota(jnp.int32, (1, LANE), 1)
    m0 = (lane < hd).astype(F32)
    m1 = 1.0 - m0

    def phase_a(a, out):
        musm = musm_ref[...]
        row0 = lax.broadcasted_iota(jnp.int32, (tb, LANE), 0) == 0

        def shift(idx, ref, mu):
            x = ref[...]
            prev = jnp.where(row0, c_ref[idx, 0:1, :], pltpu.roll(x, 1, 0))
            c_ref[idx, 0:1, :] = x[tb - 1:tb, :]
            return x + (prev - x) * mu

        r_all = shift(0, zr_ref, prow("mu_r"))
        k_all = shift(1, zk_ref, prow("mu_k"))
        v_all = shift(2, zv_ref, prow("mu_v"))
        yield
        wd = shift(3, sm0_ref, musm[0:1, 0:LANE])
        ad = shift(4, sm1_ref, musm[0:1, LANE:2 * LANE])
        gd0 = shift(5, sm2_ref, musm[0:1, 2 * LANE:3 * LANE])
        gd1 = shift(6, sm3_ref, musm[0:1, 3 * LANE:4 * LANE])
        w_lin = prow("w0") + _dot(jnp.tanh(wd), wup_ref[...])
        logw = -jnp.exp(-_softplus(-w_lin) - 0.5)
        yield
        alr = _sigmoid(prow("a0") + _dot(ad, aup_ref[...]))
        gate = _dot(_sigmoid(gd0), gup_ref[0:LANE, :]) + _dot(_sigmoid(gd1), gup_ref[LANE:2 * LANE, :])
        yield
        kraw = k_all * prow("k_k")
        kk = kraw * lax.rsqrt(_dot_sel(bd_ones, kraw * kraw, 2, left=False) + 1e-6)
        keff = k_all * (1.0 + (alr - 1.0) * prow("k_a"))
        bonus = _dot_sel(bd_ones, r_all * keff * prow("r_k"), 2, left=False) * v_all
        yield
        rt_i = lax.broadcasted_iota(jnp.int32, (tb, tb), 0)
        ct_i = lax.broadcasted_iota(jnp.int32, (tb, tb), 1)
        same_chunk = (rt_i // C) == (ct_i // C)
        cum = _dot_sel(jnp.logical_and(same_chunk, rt_i >= ct_i).astype(BF16), logw, 3)
        yield
        ctot = _dot_sel(same_chunk.astype(BF16), logw, 3)
        yield
        e_neg = jnp.exp(-cum)
        e_end = jnp.exp(ctot - cum)
        b_all = kk * alr
        a.update(v=v_all, rt=r_all * jnp.exp(cum), at=-kk * jnp.exp(cum - logw),
                 bt=b_all * e_neg, kt=keff * e_neg, bh=b_all * e_end, kh=keff * e_end)
        out.update(e_tot=jnp.exp(ctot), bonus=bonus, gate=gate)
        yield [phase_b(a, out, g) for g in range(n_groups)]

    def phase_b(a, out, g):
        r4 = lax.broadcasted_iota(jnp.int32, (SG, SG), 0)
        c4 = lax.broadcasted_iota(jnp.int32, (SG, SG), 1)
        same_unit = (r4 // C) == (c4 // C)
        strict = jnp.logical_and(same_unit, r4 > c4)
        incl = jnp.logical_and(same_unit, r4 >= c4)

        def halves(x):
            return x[g * G2:g * G2 + C], x[g * G2 + C:(g + 1) * G2]

        def stack(x):
            x0, x1 = halves(x)
            return jnp.concatenate([x0 * m0, x0 * m1, x1 * m0, x1 * m1], axis=0)

        def dup(x):
            x0, x1 = halves(x)
            return jnp.concatenate([x0, x0, x1, x1], axis=0)

        xa, xr = stack(a["at"]), stack(a["rt"])
        v_st, b_st, k_st = stack(a["v"]), stack(a["bh"]), stack(a["kh"])
        sc = _dot_nt(jnp.concatenate([xa, xr], axis=0),
                     jnp.concatenate([dup(a["bt"]), dup(a["kt"])], axis=0))
        yield
        T = yield from _inv_unit_lower(jnp.where(strict, -sc[:SG, :SG], 0.0), SG, hd)
        av = _dot(jnp.where(strict, sc[:SG, SG:], 0.0), v_st)
        wt = _dot(T, xa)
        yield
        ut = _dot(T, av)
        ork = _dot(jnp.where(incl, sc[SG:, SG:], 0.0), v_st)
        yield
        hk = [_dot_tn(v_st[i * G2:(i + 1) * G2], k_st[i * G2:(i + 1) * G2]) for i in range(2)]
        out[g] = dict(wt=wt.astype(BF16), xr=xr.astype(BF16), b_st=b_st.astype(BF16), ut=ut, ork=ork,
                      hk=hk, a_rb=jnp.where(incl, sc[SG:, :SG], 0.0).astype(BF16))

    ws = lax.rem(t, 2)
    rs = 1 - ws

    def store(out):
        for g in range(n_groups):
            o = out[g]
            pwx_ref[ws, g, 0], pwx_ref[ws, g, 1], pbs_ref[ws, g] = o["wt"], o["xr"], o["b_st"]
            puo_ref[ws, g, 0], puo_ref[ws, g, 1] = o["ut"], o["ork"]
            phk_ref[ws, 2 * g], phk_ref[ws, 2 * g + 1] = o["hk"]
            parb_ref[ws, g] = o["a_rb"]
        pvec_ref[ws, 0], pvec_ref[ws, 1], pvec_ref[ws, 2] = out["e_tot"], out["bonus"], out["gate"]

    def load():
        p = {g: dict(wt=pwx_ref[rs, g, 0], xr=pwx_ref[rs, g, 1], b_st=pbs_ref[rs, g],
                     ut=puo_ref[rs, g, 0], ork=puo_ref[rs, g, 1],
                     hk=[phk_ref[rs, 2 * g], phk_ref[rs, 2 * g + 1]], a_rb=parb_ref[rs, g])
             for g in range(n_groups)}
        p.update(e_tot=pvec_ref[rs, 0], bonus=pvec_ref[rs, 1], gate=pvec_ref[rs, 2])
        return p

    def phase_c(p):
        H = h_ref[...]
        us, xrs = [], []
        for ci in range(tb // C):
            gr = p[ci // 2]
            rows = slice((ci % 2) * G2, (ci % 2 + 1) * G2)
            xh = _dot_nt(jnp.concatenate([gr["wt"][rows], gr["xr"][rows]], axis=0), H)
            yield
            u_c = xh[:G2] + gr["ut"][rows]
            H = (H * p["e_tot"][ci * C:ci * C + 1, :] + _dot_tn(u_c, gr["b_st"][rows])
                 + gr["hk"][ci % 2])
            us.append(u_c)
            xrs.append(xh[G2:])
            yield
        h_ref[...] = H
        ys = []
        for g in range(n_groups):
            o_st = (jnp.concatenate(xrs[2 * g:2 * g + 2], axis=0)
                    + _dot(p[g]["a_rb"], jnp.concatenate(us[2 * g:2 * g + 2], axis=0)) + p[g]["ork"])
            ys += [o_st[0:C] + o_st[C:2 * C], o_st[2 * C:3 * C] + o_st[3 * C:4 * C]]
            yield
        y = jnp.concatenate(ys, axis=0)
        mu = _dot_sel(bd_ones, y, 2, left=False) * (1.0 / hd)
        yield
        yc = y - mu
        var = _dot_sel(bd_ones, yc * yc, 2, left=False) * (1.0 / hd)
        yield
        y = yc * lax.rsqrt(var + RWKV_GN_EPS) * prow("ln_w") + prow("ln_b") + p["bonus"]
        o_ref[...] = (y * p["gate"]).astype(o_ref.dtype)

    return (lambda out: phase_a({}, out)), phase_c, load, store


def _rwkv_branch(z, pv, musm, wup, aup, gup, *, batch, seq, width, hd, cb_r, cb_sm, tb):
    n = batch * seq
    nt = seq // tb
    hp = width // LANE
    nsub = SEQS_PER_STEP if hp % SEQS_PER_STEP == 0 else 1
    wide = nsub * LANE
    row_in = lambda b, t: b * nt + jnp.minimum(t, nt - 1)
    row_out = lambda b, t: b * nt + jnp.maximum(t - 1, 0)
    zspec = lambda cb: pl.BlockSpec((tb, wide), lambda b, p, t: (row_in(b, t), cb // nsub + p))
    smspec = lambda i: pl.BlockSpec((tb, LANE), lambda b, p, t: (row_in(b, t), cb_sm + i))
    colblk = lambda rows: pl.BlockSpec((rows, wide), lambda b, p, t: (0, p))
    kern = functools.partial(_rwkv_kernel, tb=tb, nt=nt, nsub=nsub, hd=hd)
    ng, sg = tb // (2 * CHUNK), 4 * CHUNK
    return pl.pallas_call(
        kern,
        grid=(batch, hp // nsub, nt + 1),
        in_specs=[zspec(cb_r), zspec(cb_r + hp), zspec(cb_r + 2 * hp),
                  smspec(0), smspec(1), smspec(2), smspec(3),
                  colblk(pv.shape[0]),
                  pl.BlockSpec(musm.shape, lambda b, p, t: (0, 0)),
                  colblk(LANE), colblk(LANE), colblk(2 * LANE)],
        out_specs=pl.BlockSpec((tb, wide), lambda b, p, t: (row_out(b, t), p)),
        out_shape=jax.ShapeDtypeStruct((n, width), BF16),
        scratch_shapes=[pltpu.VMEM((nsub, LANE, LANE), F32), pltpu.VMEM((nsub, 7, 8, LANE), F32),
                        pltpu.VMEM((nsub, 2, ng, 2, sg, LANE), BF16),
                        pltpu.VMEM((nsub, 2, ng, sg, LANE), BF16),
                        pltpu.VMEM((nsub, 2, ng, 2, sg, LANE), F32),
                        pltpu.VMEM((nsub, 2, 2 * ng, LANE, LANE), F32),
                        pltpu.VMEM((nsub, 2, ng, sg, sg), BF16), pltpu.VMEM((nsub, 2, 3, tb, LANE), F32)],
        compiler_params=_cparams(("parallel", "parallel", "arbitrary")),
        name="rwkv7",
    )(z, z, z, z, z, z, z, pv, musm, wup, aup, gup)


class _Layout:
    def __init__(self, d_model, gdn_heads, rwkv_width, decay_lora, aaa_lora, gate_lora, tn):
        self.d_model, self.gh, self.rw = d_model, gdn_heads, rwkv_width
        self.dl, self.al, self.gl = decay_lora, aaa_lora, gate_lora
        self.gw = gdn_heads * LANE
        assert decay_lora + 2 * gdn_heads <= LANE and aaa_lora <= LANE and gate_lora == 2 * LANE
        self.a_lane0 = decay_lora
        self.b_lane0 = decay_lora + gdn_heads
        self.cb_q = 0
        self.cb_gz = 3 * self.gw // LANE
        self.cb_r = 0
        self.cb_sm = 3 * rwkv_width // LANE
        self.off_z = 3 * self.gw
        self.off_a = self.off_z + self.gw
        self.off_b = self.off_a + gdn_heads
        self.off_rwkv = self.off_b + gdn_heads
        self.off_gates = self.off_rwkv + 3 * rwkv_width + decay_lora + aaa_lora + gate_lora

    def split_cols(self, w):
        lead = w.shape[:-1]
        zeros = lambda n: jnp.zeros(lead + (n,), w.dtype)
        rw, dl, al, gl, gh = self.rw, self.dl, self.al, self.gl, self.gh
        o = self.off_rwkv + 3 * rw
        piece_b = jnp.concatenate(
            [w[..., self.off_rwkv:o],
             w[..., o:o + dl], w[..., self.off_a:self.off_b], w[..., self.off_b:self.off_rwkv],
             zeros(LANE - dl - 2 * gh),
             w[..., o + dl:o + dl + al], zeros(LANE - al),
             w[..., o + dl + al:o + dl + al + gl]], axis=-1)
        return w[..., :self.off_a], piece_b, w[..., self.off_gates:]


def _pad_rows(w, rows):
    return jnp.concatenate([w, jnp.zeros((rows - w.shape[0],) + w.shape[1:], w.dtype)], axis=0)


def _gdn_from_packed(za, zb, p, lay, batch, seq, tb=256):
    lane_vec = lambda v, lane0: jnp.zeros((1, LANE), F32).at[0, lane0:lane0 + v.shape[0]].set(v)
    return _gdn_branch(za, zb, p["conv_w"], lane_vec(p["gdn_a_log"], lay.a_lane0),
                       lane_vec(p["gdn_dt_bias"], lay.a_lane0), p["gdn_onorm_g"].reshape(1, LANE),
                       batch=batch, seq=seq, heads=lay.gh, cb_q=lay.cb_q, cb_gz=lay.cb_gz,
                       cb_sm=lay.cb_sm, a_lane0=lay.a_lane0, b_lane0=lay.b_lane0, tb=min(tb, seq))


def _rwkv_from_packed(z, p, lay, batch, seq, tb=256):
    rw = lay.rw
    hd = p["rwkv_r_k"].shape[-1]
    assert LANE % hd == 0 and hd * 2 == LANE
    mu = p["rwkv_mu"]
    rows = {"w0": p["rwkv_w0"], "a0": p["rwkv_a0"], "k_k": p["rwkv_k_k"], "k_a": p["rwkv_k_a"],
            "r_k": p["rwkv_r_k"].reshape(rw), "ln_w": p["rwkv_ln_w"], "ln_b": p["rwkv_ln_b"],
            "mu_r": mu[:rw], "mu_k": mu[rw:2 * rw], "mu_v": mu[2 * rw:3 * rw]}
    pv = _pad_rows(jnp.stack([rows[k] for k in _RW_ROWS]).astype(F32), 16)
    o = 3 * rw
    zl = lambda n: jnp.zeros((n,), F32)
    musm = jnp.concatenate([mu[o:o + lay.dl], zl(LANE - lay.dl), mu[o + lay.dl:o + lay.dl + lay.al],
                            zl(LANE - lay.al), mu[o + lay.dl + lay.al:]])
    musm = _pad_rows(musm.reshape(1, 4 * LANE), 8)
    wup = _pad_rows(p["rwkv_w_up"], LANE).astype(BF16)
    aup = _pad_rows(p["rwkv_a_up"], LANE).astype(BF16)
    gup = p["rwkv_g_up"].astype(BF16)
    return _rwkv_branch(z, pv, musm, wup, aup, gup, batch=batch, seq=seq, width=rw, hd=hd,
                        cb_r=lay.cb_r, cb_sm=lay.cb_sm, tb=min(tb, seq))


def _make_layout(p, tn):
    return _Layout(p["norm1_g"].shape[-1], p["gdn_a_log"].shape[-1], p["rwkv_w0"].shape[-1],
                   p["rwkv_w_up"].shape[0], p["rwkv_a_up"].shape[0], p["rwkv_g_up"].shape[0], tn)


def _ada_kernel(c_ref, w_ref, b_ref, o_ref):
    o_ref[...] = _dot(_silu(c_ref[...]), w_ref[...]) + b_ref[...]


def _ada_mod(c_pad, w_ada, b_ada, tn):
    d, cols = w_ada.shape
    return pl.pallas_call(
        _ada_kernel,
        grid=(cols // tn,),
        in_specs=[pl.BlockSpec((8, d), lambda j: (0, 0)),
                  pl.BlockSpec((d, tn), lambda j: (0, j)),
                  pl.BlockSpec((1, tn), lambda j: (0, j))],
        out_specs=pl.BlockSpec((8, tn), lambda j: (0, j)),
        out_shape=jax.ShapeDtypeStruct((8, cols), F32),
        compiler_params=_cparams(("arbitrary",)),
        name="ada_mod",
    )(c_pad, w_ada, b_ada.reshape(1, cols))


def _modnorm(x, g, sc, sh):
    y = x * lax.rsqrt(jnp.mean(x * x, axis=-1, keepdims=True) + RMS_EPS)
    return y * g * (1.0 + sc) + sh


def _norm1_kernel(x_ref, g_ref, sh_ref, sc_ref, o_ref):
    b = pl.program_id(0)
    h = _modnorm(x_ref[...], g_ref[...], sc_ref[pl.ds(b, 1), :], sh_ref[pl.ds(b, 1), :])
    o_ref[...] = h.astype(o_ref.dtype)


def _norm1(x2, g, mod, *, batch, seq, tm):
    n, d = x2.shape
    nt = seq // tm
    return pl.pallas_call(
        _norm1_kernel,
        grid=(batch, nt),
        in_specs=[pl.BlockSpec((tm, d), lambda b, t: (b * nt + t, 0)),
                  pl.BlockSpec((1, d), lambda b, t: (0, 0)),
                  pl.BlockSpec((8, d), lambda b, t: (0, 0)),
                  pl.BlockSpec((8, d), lambda b, t: (0, 1))],
        out_specs=pl.BlockSpec((tm, d), lambda b, t: (b * nt + t, 0)),
        out_shape=jax.ShapeDtypeStruct((n, d), BF16),
        compiler_params=_cparams(("parallel", "arbitrary")),
        name="norm1",
    )(x2, g, mod, mod)


def _merge_kernel(o_ref, y_ref, wa_ref, wb_ref, ga_ref, gb_ref, m_ref):
    ha = jnp.dot(o_ref[...], wa_ref[...], preferred_element_type=F32)
    hb = jnp.dot(y_ref[...], wb_ref[...], preferred_element_type=F32)
    m_ref[...] = (ga_ref[...].astype(F32) * ha + gb_ref[...].astype(F32) * hb).astype(m_ref.dtype)


def _merge(o, y, wa, wb, gates, *, tm, tn):
    n, d = o.shape[0], wa.shape[1]
    tn = math.gcd(tn, d)
    gb0 = 0
    return pl.pallas_call(
        _merge_kernel,
        grid=(n // tm, d // tn),
        in_specs=[pl.BlockSpec((tm, o.shape[1]), lambda i, j: (i, 0)),
                  pl.BlockSpec((tm, y.shape[1]), lambda i, j: (i, 0)),
                  pl.BlockSpec((wa.shape[0], tn), lambda i, j: (0, j)),
                  pl.BlockSpec((wb.shape[0], tn), lambda i, j: (0, j)),
                  pl.BlockSpec((tm, tn), lambda i, j: (i, gb0 + j)),
                  pl.BlockSpec((tm, tn), lambda i, j: (i, gb0 + d // tn + j))],
        out_specs=pl.BlockSpec((tm, tn), lambda i, j: (i, j)),
        out_shape=jax.ShapeDtypeStruct((n, d), BF16),
        compiler_params=_cparams(("parallel", "arbitrary")),
        name="merge",
    )(o, y, wa, wb, gates, gates)


def _outproj_kernel(m_ref, w_ref, x_ref, g_ref, o_ref, *, tiles_per_batch):
    b = pl.program_id(0) // tiles_per_batch
    y = jnp.dot(m_ref[...], w_ref[...], preferred_element_type=F32)
    o_ref[...] = x_ref[...] + g_ref[pl.ds(b, 1), :] * y


def _outproj(m, w, x2, mod, *, seq, tm, tn):
    n, d = x2.shape
    gcol = 2 * d // tn
    return pl.pallas_call(
        functools.partial(_outproj_kernel, tiles_per_batch=seq // tm),
        grid=(n // tm, d // tn),
        in_specs=[pl.BlockSpec((tm, d), lambda i, j: (i, 0)),
                  pl.BlockSpec((d, tn), lambda i, j: (0, j)),
                  pl.BlockSpec((tm, tn), lambda i, j: (i, j)),
                  pl.BlockSpec((8, tn), lambda i, j: (0, gcol + j))],
        out_specs=pl.BlockSpec((tm, tn), lambda i, j: (i, j)),
        out_shape=jax.ShapeDtypeStruct((n, d), F32),
        compiler_params=_cparams(("parallel", "arbitrary")),
        name="outproj",
    )(m, w, x2, mod)


def _router_kernel(x_ref, g_ref, sh_ref, sc_ref, wr_ref, br_ref, h_ref, id_ref, wt_ref, cnt_out_ref,
                   cnt_ref, *, n_groups, per_group):
    b = pl.program_id(0)
    h = _modnorm(x_ref[...], g_ref[...], sc_ref[pl.ds(b, 1), :], sh_ref[pl.ds(b, 1), :])
    h_ref[...] = h
    logits = _dot(h, wr_ref[...]) + br_ref[...]
    tm = logits.shape[0]
    lane = lax.broadcasted_iota(jnp.int32, (tm, LANE), 1)
    neg = jnp.float32(-jnp.inf)
    big = jnp.int32(LANE)

    def first_argmax(vals, mask):
        vm = jnp.where(mask, vals, neg)
        mx = jnp.max(vm, axis=1, keepdims=True)
        idx = jnp.min(jnp.where(jnp.logical_and(mask, vm == mx), lane, big), axis=1, keepdims=True)
        return mx, idx

    gmask = lane < n_groups
    gmax, g_sel = first_argmax(logits, gmask)
    g_prob = 1.0 / jnp.sum(jnp.where(gmask, jnp.exp(logits - gmax), 0.0), axis=1, keepdims=True)
    e_lane = lane - n_groups
    emask = jnp.logical_and(e_lane >= g_sel * per_group, e_lane < (g_sel + 1) * per_group)
    emax = jnp.max(jnp.where(emask, logits, neg), axis=1, keepdims=True)
    ex = jnp.where(emask, jnp.exp(logits - emax), 0.0)
    probs = ex / jnp.sum(ex, axis=1, keepdims=True)
    p0, i0 = first_argmax(probs, emask)
    p1, i1 = first_argmax(probs, jnp.logical_and(emask, lane != i0))
    den = p0 + p1
    wt_ref[...] = jnp.where(lane == 0, g_prob * (p0 / den), jnp.where(lane == 1, g_prob * (p1 / den), 0.0))

    @pl.when(jnp.logical_and(b == 0, pl.program_id(1) == 0))
    def _():
        cnt_ref[...] = jnp.zeros_like(cnt_ref)

    e0, e1 = i0 - n_groups, i1 - n_groups
    hit0, hit1 = lane == e0, lane == e1
    hits = jnp.logical_or(hit0, hit1).astype(BF16)
    rr = lax.broadcasted_iota(jnp.int32, (tm, tm), 0)
    cc = lax.broadcasted_iota(jnp.int32, (tm, tm), 1)
    before = jnp.dot((rr > cc).astype(BF16), hits, preferred_element_type=F32) + cnt_ref[0:1, :]
    rank0 = jnp.sum(jnp.where(hit0, before, 0.0), axis=1, keepdims=True).astype(jnp.int32)
    rank1 = jnp.sum(jnp.where(hit1, before, 0.0), axis=1, keepdims=True).astype(jnp.int32)
    cnt_ref[0:1, :] = cnt_ref[0:1, :] + jnp.sum(hits.astype(F32), axis=0, keepdims=True)
    id_ref[...] = jnp.where(lane == 0, e0, jnp.where(lane == 1, e1,
                            jnp.where(lane == 2, rank0, jnp.where(lane == 3, rank1, 0))))
    cnt_out_ref[...] = cnt_ref[...].astype(jnp.int32)


def _router(x1, g, mod, wr, br, *, batch, seq, tm, n_groups, per_group):
    n, d = x1.shape
    nt = seq // tm
    rowblk = lambda w: pl.BlockSpec((tm, w), lambda b, t: (b * nt + t, 0))
    return pl.pallas_call(
        functools.partial(_router_kernel, n_groups=n_groups, per_group=per_group),
        grid=(batch, nt),
        in_specs=[rowblk(d),
                  pl.BlockSpec((1, d), lambda b, t: (0, 0)),
                  pl.BlockSpec((8, d), lambda b, t: (0, 3)),
                  pl.BlockSpec((8, d), lambda b, t: (0, 4)),
                  pl.BlockSpec((d, LANE), lambda b, t: (0, 0)),
                  pl.BlockSpec((1, LANE), lambda b, t: (0, 0))],
        out_specs=[rowblk(d), rowblk(LANE), rowblk(LANE), pl.BlockSpec((8, LANE), lambda b, t: (0, 0))],
        out_shape=[jax.ShapeDtypeStruct((n, d), F32), jax.ShapeDtypeStruct((n, LANE), jnp.int32),
                   jax.ShapeDtypeStruct((n, LANE), F32), jax.ShapeDtypeStruct((8, LANE), jnp.int32)],
        scratch_shapes=[pltpu.VMEM((8, LANE), F32)],
        compiler_params=_cparams(("arbitrary", "arbitrary")),
        name="router",
    )(x1, g, mod, mod, wr, br)


def _expert_kernel(be_ref, tok_ref, h_hbm, w1_ref, w3_ref, w2_ref, bw_ref, o_ref, xbuf, sem):
    i = pl.program_id(0)
    nb = pl.num_programs(0)

    def row_copy(blk, slot, r):
        tok = tok_ref[blk * MOE_BLOCK + r]
        return pltpu.make_async_copy(h_hbm.at[pl.ds(tok, 1), :], xbuf.at[slot, pl.ds(r, 1), :],
                                     sem.at[slot])

    def start_block(blk, slot):
        def body(r, carry):
            row_copy(blk, slot, r).start()
            return carry
        lax.fori_loop(0, MOE_BLOCK, body, 0, unroll=DMA_UNROLL)

    @pl.when(i == 0)
    def _():
        start_block(0, 0)

    @pl.when(i + 1 < nb)
    def _():
        start_block(i + 1, (i + 1) % 2)

    slot = i % 2

    def wait_body(r, carry):
        row_copy(i, slot, r).wait()
        return carry
    lax.fori_loop(0, MOE_BLOCK, wait_body, 0, unroll=DMA_UNROLL)

    x = xbuf[slot]
    a = _dot(x, w1_ref[...])
    g = _dot(x, w3_ref[...])
    y = _dot(_silu(a) * g, w2_ref[...])
    o_ref[...] = y * bw_ref[...]


def _experts(h2, w1, w3, w2, block_expert, buf_tok, buf_w):
    d = h2.shape[1]
    de = w1.shape[2]
    p_rows = buf_tok.shape[0]
    wspec = lambda shape: pl.BlockSpec((None,) + shape, lambda i, be, tok: (be[i], 0, 0))
    grid_spec = pltpu.PrefetchScalarGridSpec(
        num_scalar_prefetch=2,
        grid=(p_rows // MOE_BLOCK,),
        in_specs=[pl.BlockSpec(memory_space=pl.ANY),
                  wspec((d, de)), wspec((d, de)), wspec((de, d)),
                  pl.BlockSpec((MOE_BLOCK, 1), lambda i, be, tok: (i, 0))],
        out_specs=pl.BlockSpec((MOE_BLOCK, d), lambda i, be, tok: (i, 0)),
        scratch_shapes=[pltpu.VMEM((2, MOE_BLOCK, d), F32), pltpu.SemaphoreType.DMA((2,))],
    )
    return pl.pallas_call(
        _expert_kernel,
        grid_spec=grid_spec,
        out_shape=jax.ShapeDtypeStruct((p_rows, d), F32),
        compiler_params=_cparams(("arbitrary",)),
        name="experts",
    )(block_expert, buf_tok, h2, w1, w3, w2, buf_w)


def _combine_kernel(pos_ref, x_ref, g_ref, gf_ref, y_hbm, o_ref, ybuf, sem, *, tm, tiles_per_batch):
    i = pl.program_id(0)
    nb = pl.num_programs(0)
    rows = TOP_K * tm

    def row_copy(blk, slot, r):
        return pltpu.make_async_copy(y_hbm.at[pl.ds(pos_ref[blk * rows + r], 1), :],
                                     ybuf.at[slot, pl.ds(r, 1), :], sem.at[slot])

    def start_block(blk, slot):
        def body(r, carry):
            row_copy(blk, slot, r).start()
            return carry
        lax.fori_loop(0, rows, body, 0, unroll=DMA_UNROLL)

    @pl.when(i == 0)
    def _():
        start_block(0, 0)

    @pl.when(i + 1 < nb)
    def _():
        start_block(i + 1, (i + 1) % 2)

    slot = i % 2

    def wait_body(r, carry):
        row_copy(i, slot, r).wait()
        return carry
    lax.fori_loop(0, rows, wait_body, 0, unroll=DMA_UNROLL)

    b = i // tiles_per_batch
    moe = ybuf[slot, 0:tm, :] + ybuf[slot, tm:rows, :]
    x = x_ref[...] + g_ref[pl.ds(b, 1), :] * moe
    o_ref[...] = x * lax.rsqrt(jnp.mean(x * x, axis=-1, keepdims=True) + RMS_EPS) * gf_ref[...]


def _combine(pos, x1, mod, gf, yb, *, seq, tm):
    n, d = x1.shape
    grid_spec = pltpu.PrefetchScalarGridSpec(
        num_scalar_prefetch=1,
        grid=(n // tm,),
        in_specs=[pl.BlockSpec((tm, d), lambda i, pos: (i, 0)),
                  pl.BlockSpec((8, d), lambda i, pos: (0, 5)),
                  pl.BlockSpec((1, d), lambda i, pos: (0, 0)),
                  pl.BlockSpec(memory_space=pl.ANY)],
        out_specs=pl.BlockSpec((tm, d), lambda i, pos: (i, 0)),
        scratch_shapes=[pltpu.VMEM((2, TOP_K * tm, d), F32), pltpu.SemaphoreType.DMA((2,))],
    )
    return pl.pallas_call(
        functools.partial(_combine_kernel, tm=tm, tiles_per_batch=seq // tm),
        grid_spec=grid_spec,
        out_shape=jax.ShapeDtypeStruct((n, d), F32),
        compiler_params=_cparams(("arbitrary",)),
        name="combine",
    )(pos, x1, mod, gf, yb)


def _dispatch(ids, rank, wts, counts):
    n = ids.shape[0]
    n_experts = counts.shape[0]
    a = n * TOP_K
    expert_id = ids.reshape(a)
    w_flat = wts.reshape(a)
    order = jnp.argsort(expert_id).astype(jnp.int32)
    ei = jnp.arange(n_experts, dtype=jnp.int32)
    upto = ei[None, :] <= ei[:, None]
    padded = (counts + MOE_BLOCK - 1) // MOE_BLOCK * MOE_BLOCK
    c_start = jnp.sum(jnp.where(upto, counts[None, :], 0), axis=1) - counts
    p_end = jnp.sum(jnp.where(upto, padded[None, :], 0), axis=1)
    p_start = p_end - padded
    pos = rank.reshape(a) + p_start[expert_id]
    p_rows = (a + n_experts * (MOE_BLOCK - 1) + MOE_BLOCK - 1) // MOE_BLOCK * MOE_BLOCK
    n_blocks = p_rows // MOE_BLOCK
    starts = jnp.arange(n_blocks, dtype=jnp.int32) * MOE_BLOCK
    block_expert = jnp.minimum(jnp.sum((p_end[None, :] <= starts[:, None]).astype(jnp.int32), axis=1),
                               n_experts - 1)
    slot = jnp.arange(p_rows, dtype=jnp.int32)
    per_slot = lambda table: jnp.repeat(table[block_expert], MOE_BLOCK)
    within = slot - per_slot(p_start)
    valid = within < per_slot(counts)
    src = order[jnp.clip(within + per_slot(c_start), 0, a - 1)]
    buf_tok = jnp.where(valid, src // TOP_K, 0)
    buf_w = jnp.where(valid, w_flat[src], 0.0)
    return pos, buf_tok, buf_w, block_expert


def _pick(n, pref):
    t = min(n, pref)
    while n % t:
        t //= 2
    return t


def kernel(x, c, w_ada, b_ada, norm1_g, w_in, conv_w, gdn_a_log, gdn_dt_bias, gdn_onorm_g, rwkv_mu,
           rwkv_w0, rwkv_w_up, rwkv_a0, rwkv_a_up, rwkv_g_up, rwkv_k_k, rwkv_k_a, rwkv_r_k, rwkv_ln_w,
           rwkv_ln_b, w_gdn_o, w_rwkv_o, w_out, norm2_g, w_group, b_group, w_expert, b_expert, w1, w3,
           w2, norm_f_g):
    batch, seq, d = x.shape
    n = batch * seq
    depth = w_ada.shape[0]
    assert depth == 1, "the final norm is fused into the last layer's combine"
    assert batch <= 8 and seq % CHUNK == 0 and d % 512 == 0
    x2 = x.reshape(n, d)
    c_pad = _pad_rows(c, 8)
    tm_big = _pick(seq, 1024)
    tm_row = _pick(seq, 256)
    tn = 512
    for l in range(depth):
        p = {"norm1_g": norm1_g[l], "conv_w": conv_w[l], "gdn_a_log": gdn_a_log[l],
             "gdn_dt_bias": gdn_dt_bias[l], "gdn_onorm_g": gdn_onorm_g[l], "rwkv_mu": rwkv_mu[l],
             "rwkv_w0": rwkv_w0[l], "rwkv_w_up": rwkv_w_up[l], "rwkv_a0": rwkv_a0[l],
             "rwkv_a_up": rwkv_a_up[l], "rwkv_g_up": rwkv_g_up[l], "rwkv_k_k": rwkv_k_k[l],
             "rwkv_k_a": rwkv_k_a[l], "rwkv_r_k": rwkv_r_k[l], "rwkv_ln_w": rwkv_ln_w[l],
             "rwkv_ln_b": rwkv_ln_b[l]}
        lay = _make_layout(p, tn)
        mod = _ada_mod(c_pad, w_ada[l], b_ada[l], tn)

        h1 = _norm1(x2, norm1_g[l].reshape(1, d), mod, batch=batch, seq=seq, tm=tm_row)
        w_a, w_b, w_c = lay.split_cols(w_in[l])
        n_exp, _, d_exp = w1.shape[1:]
        flat = lambda w: w.reshape(-1, w.shape[-1])
        za, w1b, w2b = _matmul(h1, w_a.astype(BF16), tm_big, tn, F32, name="in_proj_gdn",
                               cast=[flat(w1[l]), flat(w2[l])])
        zb = _matmul(h1, w_b.astype(BF16), tm_big, tn, F32, name="in_proj_rwkv")
        gates, w3b = _matmul(h1, w_c.astype(BF16), tm_big, tn, BF16, act=_sigmoid,
                             name="in_proj_gates", cast=[flat(w3[l])])
        w1b, w3b = w1b.reshape(n_exp, d, d_exp), w3b.reshape(n_exp, d, d_exp)
        w2b = w2b.reshape(n_exp, d_exp, d)
        o_gdn = _gdn_from_packed(za, zb, p, lay, batch, seq)
        y_rwkv = _rwkv_from_packed(zb, p, lay, batch, seq)
        m = _merge(o_gdn, y_rwkv, w_gdn_o[l].astype(BF16), w_rwkv_o[l].astype(BF16), gates,
                   tm=tm_big, tn=tn)
        x1 = _outproj(m, w_out[l].astype(BF16), x2, mod, seq=seq, tm=tm_big, tn=tn)

        n_groups, n_experts = w_group.shape[-1], w_expert.shape[-1]
        assert n_groups + n_experts <= LANE
        wr = jnp.concatenate([w_group[l], w_expert[l],
                              jnp.zeros((d, LANE - n_groups - n_experts), F32)], axis=1).astype(BF16)
        br = jnp.concatenate([b_group[l], b_expert[l],
                              jnp.zeros((LANE - n_groups - n_experts,), F32)]).reshape(1, LANE)
        h2, ids, wts, cnt = _router(x1, norm2_g[l].reshape(1, d), mod, wr, br, batch=batch, seq=seq,
                                    tm=tm_row, n_groups=n_groups, per_group=n_experts // n_groups)
        pos, buf_tok, buf_w, block_expert = _dispatch(ids[:, :TOP_K], ids[:, TOP_K:2 * TOP_K],
                                                      wts[:, :TOP_K], cnt[0, :n_experts])
        yb = _experts(h2, w1b, w3b, w2b, block_expert, buf_tok, buf_w.reshape(-1, 1))
        tm_c = _pick(seq, 128)
        pos_tiles = pos.reshape(n // tm_c, tm_c, TOP_K).transpose(0, 2, 1).reshape(n * TOP_K)
        x2 = _combine(pos_tiles, x1, mod, norm_f_g.reshape(1, d), yb, seq=seq, tm=tm_c)
    return x2.reshape(batch, seq, d)
```

```python
import functools
import math

import jax
import jax.numpy as jnp
from jax import lax
from jax.experimental import pallas as pl
from jax.experimental.pallas import tpu as pltpu

F32 = jnp.float32
BF16 = jnp.bfloat16

LANE = 128
CHUNK = 64
CONV_WIDTH = 4
RMS_EPS = 1e-6
RWKV_GN_EPS = 64e-5
MOE_BLOCK = 128
TOP_K = 2
DMA_UNROLL = 8
GATHER_DEPTH = 3
SEQS_PER_STEP = 4
VMEM_LIMIT = 56 * 1024 * 1024


def _cparams(sem):
    return pltpu.CompilerParams(dimension_semantics=sem, vmem_limit_bytes=VMEM_LIMIT)


def _dot(a, b):
    return jnp.dot(a.astype(BF16), b.astype(BF16), preferred_element_type=F32)


def _dot_nt(a, b):
    return lax.dot_general(a.astype(BF16), b.astype(BF16), (((1,), (1,)), ((), ())),
                           preferred_element_type=F32)


def _dot_tn(a, b):
    return lax.dot_general(a.astype(BF16), b.astype(BF16), (((0,), (0,)), ((), ())),
                           preferred_element_type=F32)


def _pack_bf16_pairs(x):
    w = x.shape[1] // 2
    bits = lax.bitcast_convert_type(x.astype(BF16).astype(F32), jnp.uint32)
    return (bits[:, w:] & jnp.uint32(0xFFFF0000)) | (bits[:, :w] >> 16)


def _unpack_bf16_pairs(words):
    lo = lax.bitcast_convert_type(words << 16, F32)
    hi = lax.bitcast_convert_type(words & jnp.uint32(0xFFFF0000), F32)
    return lo, hi


def _split(x, terms):
    pieces = []
    for _ in range(terms - 1):
        hi = x.astype(BF16)
        pieces.append(hi)
        x = x - hi.astype(F32)
    pieces.append(x.astype(BF16))
    return pieces


def _dot_sel(sel, x, terms, left=True):
    out = None
    for piece in _split(x, terms):
        ops = (sel, piece) if left else (piece, sel)
        d = jnp.dot(*ops, preferred_element_type=F32)
        out = d if out is None else out + d
    return out


def _sigmoid(x):
    return 1.0 / (1.0 + jnp.exp(-x))


def _silu(x):
    return x * _sigmoid(x)


def _softplus(x):
    return jnp.maximum(x, 0.0) + jnp.log(1.0 + jnp.exp(-jnp.abs(x)))


def _inv_unit_lower(L, n, top):
    r = lax.broadcasted_iota(jnp.int32, (n, n), 0)
    c = lax.broadcasted_iota(jnp.int32, (n, n), 1)
    eye = (r == c).astype(F32)
    same16 = (r // 16) == (c // 16)
    Ld = jnp.where(same16, L, 0.0)
    X = eye - Ld
    P = _dot(Ld, Ld)
    yield
    for _ in range(2):
        X = X + _dot(X, P)
        P = _dot(P, P)
        yield
    X = X + _dot(X, P)
    yield
    bs = 32
    while bs <= top:
        inner = (r // (bs // 2)) == (c // (bs // 2))
        outer = (r // bs) == (c // bs)
        Lo = jnp.where(jnp.logical_and(outer, jnp.logical_not(inner)), L, 0.0)
        Y = _dot(Lo, X)
        yield
        X = X - _dot(X, Y)
        yield
        bs *= 2
    return X


def _round_robin(gens):
    gens = list(gens)
    while gens:
        for g in list(gens):
            try:
                more = next(g)
            except StopIteration:
                gens.remove(g)
                continue
            if more:
                gens.extend(more)


def _mm_kernel(a_ref, w_ref, *refs, act, n_cast):
    cast_in, o_ref, cast_out = refs[:n_cast], refs[n_cast], refs[n_cast + 1:]
    y = jnp.dot(a_ref[...], w_ref[...], preferred_element_type=F32)
    o_ref[...] = (act(y) if act else y).astype(o_ref.dtype)
    for src, dst in zip(cast_in, cast_out):
        dst[...] = src[...].astype(dst.dtype)


def _matmul(a, w, tm, tn, out_dtype, act=None, name="matmul", cast=()):
    M, K = a.shape
    Nn = w.shape[1]
    tn = math.gcd(tn, Nn)
    nj = Nn // tn
    steps = (M // tm) * nj
    cast = [c for c in cast]
    assert all(c.shape[0] % (16 * steps) == 0 for c in cast)
    slab = lambda c: pl.BlockSpec((c.shape[0] // steps, c.shape[1]), lambda i, j: (i * nj + j, 0))
    outs = pl.pallas_call(
        functools.partial(_mm_kernel, act=act, n_cast=len(cast)),
        grid=(M // tm, nj),
        in_specs=[pl.BlockSpec((tm, K), lambda i, j: (i, 0)),
                  pl.BlockSpec((K, tn), lambda i, j: (0, j))] + [slab(c) for c in cast],
        out_specs=[pl.BlockSpec((tm, tn), lambda i, j: (i, j))] + [slab(c) for c in cast],
        out_shape=[jax.ShapeDtypeStruct((M, Nn), out_dtype)]
                  + [jax.ShapeDtypeStruct(c.shape, BF16) for c in cast],
        compiler_params=_cparams(("parallel", "arbitrary")),
        name=name,
    )(a, w, *cast)
    return outs if cast else outs[0]


def _run_pipelined(t, nt, heads):
    def prepare():
        outs = [{} for _ in heads]
        return outs, [hd[0](out) for hd, out in zip(heads, outs)]

    def finish(outs):
        for hd, out in zip(heads, outs):
            hd[3](out)

    @pl.when(t == 0)
    def _():
        outs, gens = prepare()
        _round_robin(gens)
        finish(outs)

    @pl.when(jnp.logical_and(t > 0, t < nt))
    def _():
        prevs = [hd[2]() for hd in heads]
        outs, gens = prepare()
        _round_robin([hd[1](p) for hd, p in zip(heads, prevs)] + gens)
        finish(outs)

    @pl.when(t == nt)
    def _():
        _round_robin([hd[1](hd[2]()) for hd in heads])


def _gdn_setup(t, zq_ref, zk_ref, zv_ref, gz_ref, sm_ref, cwq_ref, cwk_ref, cwv_ref, alog_ref, dtb_ref,
               on_ref, o_ref, s_ref, xs_ref, pw_ref, pu_ref, pattn_ref, *, nsub, **kw):
    @pl.when(t == 0)
    def _():
        s_ref[...] = jnp.zeros_like(s_ref)
        xs_ref[:, :, 0:8, :] = jnp.zeros((nsub, 3, 8, LANE), F32)

    lanes = lambda s: slice(s * LANE, (s + 1) * LANE)
    return [_gdn_head(pl.program_id(1) * nsub + s, t,
                      zq_ref.at[:, lanes(s)], zk_ref.at[:, lanes(s)], zv_ref.at[:, lanes(s)],
                      gz_ref.at[:, lanes(s)], sm_ref, cwq_ref.at[:, lanes(s)], cwk_ref.at[:, lanes(s)],
                      cwv_ref.at[:, lanes(s)], alog_ref, dtb_ref, on_ref, o_ref.at[:, lanes(s)],
                      s_ref.at[s], xs_ref.at[s], pw_ref.at[s], pu_ref.at[s], pattn_ref.at[s], **kw)
            for s in range(nsub)]


def _gdn_head(h, t, zq_ref, zk_ref, zv_ref, gz_ref, sm_ref, cwq_ref, cwk_ref, cwv_ref, alog_ref, dtb_ref,
              on_ref, o_ref, s_ref, xs_ref, pw_ref, pu_ref, pattn_ref, *, tb, a_lane0, b_lane0, q_scale):
    C = CHUNK

    def conv_silu(idx, z_ref, cw_ref):
        xs_ref[idx, 8:8 + tb, :] = z_ref[...]
        cw = cw_ref[...]
        acc = cw[CONV_WIDTH - 1:CONV_WIDTH, :] * xs_ref[idx, 8:8 + tb, :]
        for i in range(1, CONV_WIDTH):
            acc = acc + cw[CONV_WIDTH - 1 - i:CONV_WIDTH - i, :] * xs_ref[idx, 8 - i:8 - i + tb, :]
        xs_ref[idx, 0:8, :] = xs_ref[idx, tb:tb + 8, :]
        return _silu(acc)

    def l2n(x):
        return x * lax.rsqrt(jnp.sum(x * x, axis=-1, keepdims=True) + 1e-6)

    def phase_ab(out):
        q_all = l2n(conv_silu(0, zq_ref, cwq_ref)) * q_scale
        yield
        k_all = l2n(conv_silu(1, zk_ref, cwk_ref))
        yield
        v_all = conv_silu(2, zv_ref, cwv_ref)
        sm = sm_ref[...]
        lane = lax.broadcasted_iota(jnp.int32, (1, LANE), 1)
        g_full = -jnp.exp(alog_ref[...]) * _softplus(sm + dtb_ref[...])
        g_all = jnp.sum(jnp.where(lane == a_lane0 + h, g_full, 0.0), axis=1, keepdims=True)
        beta_all = jnp.sum(jnp.where(lane == b_lane0 + h, _sigmoid(sm), 0.0), axis=1, keepdims=True)
        yield
        r = lax.broadcasted_iota(jnp.int32, (tb, tb), 0)
        c = lax.broadcasted_iota(jnp.int32, (tb, tb), 1)
        same = (r // C) == (c // C)
        causal = jnp.logical_and(same, r >= c)
        strict = jnp.logical_and(same, r > c)
        gb = jnp.broadcast_to(g_all, (tb, LANE))
        sums = _dot_sel(jnp.concatenate([causal, same], axis=0).astype(BF16), gb, 2)
        gc_lanes, g_tot = sums[:tb], sums[tb:]
        yield
        gc_col = jnp.concatenate([gc_lanes] * (tb // LANE), axis=1)
        gc_row = gc_col.T
        decay = jnp.where(causal, jnp.exp(jnp.minimum(gc_col - gc_row, 0.0)), 0.0)
        gc = gc_col[:, 0:1]
        e_gc = jnp.exp(gc)
        kb = k_all * beta_all
        sc = _dot_nt(jnp.concatenate([kb, q_all], axis=0), k_all)
        yield
        L = sc[:tb] * jnp.where(strict, decay, 0.0)
        T = yield from _inv_unit_lower(L, tb, C)
        uw = _dot(T, jnp.concatenate([v_all * beta_all, kb * e_gc], axis=1))
        yield
        out["w"] = uw[:, LANE:].astype(BF16)
        out["q_dec"] = (q_all * e_gc).astype(BF16)
        out["k_dec"] = (k_all * jnp.exp(g_tot[:, 0:1] - gc)).astype(BF16)
        out["u"] = uw[:, :LANE]
        out["e_gl"] = jnp.exp(g_tot[:, :LANE])
        out["gzn"] = on_ref[...] * _silu(gz_ref[...])
        out["attn"] = (sc[tb:] * decay).astype(BF16)

    wslot = lax.rem(t, 2)
    rslot = 1 - wslot

    def store(out):
        pw_ref[wslot, 0], pw_ref[wslot, 1], pw_ref[wslot, 2] = out["w"], out["q_dec"], out["k_dec"]
        pu_ref[wslot, 0], pu_ref[wslot, 1], pu_ref[wslot, 2] = out["u"], out["e_gl"], out["gzn"]
        pattn_ref[wslot] = out["attn"]

    def load():
        return dict(w=pw_ref[rslot, 0], q_dec=pw_ref[rslot, 1], k_dec=pw_ref[rslot, 2],
                    u=pu_ref[rslot, 0], e_gl=pu_ref[rslot, 1], gzn=pu_ref[rslot, 2],
                    attn=pattn_ref[rslot])

    def phase_c(p):
        S = s_ref[...]
        v_news, o_qs = [], []
        for ci in range(tb // C):
            sl = slice(ci * C, (ci + 1) * C)
            ws = _dot(jnp.concatenate([p["w"][sl], p["q_dec"][sl]], axis=0), S)
            yield
            v_new = p["u"][sl] - ws[:C]
            v_news.append(v_new)
            o_qs.append(ws[C:])
            S = S * p["e_gl"][ci * C:ci * C + 1, :] + _dot_tn(p["k_dec"][sl], v_new)
            yield
        s_ref[...] = S
        o = jnp.concatenate(o_qs, axis=0) + _dot(p["attn"], jnp.concatenate(v_news, axis=0))
        yield
        o = o * lax.rsqrt(jnp.mean(o * o, axis=-1, keepdims=True) + RMS_EPS) * p["gzn"]
        o_ref[...] = o.astype(o_ref.dtype)

    return phase_ab, phase_c, load, store


def _gdn_branch(za, zb, conv_w, alog_pad, dtb_pad, onorm_g, *, batch, seq, heads, cb_q, cb_gz, cb_sm,
                a_lane0, b_lane0, tb):
    n = batch * seq
    nt = seq // tb
    nsub = SEQS_PER_STEP if heads % SEQS_PER_STEP == 0 else 1
    wide = nsub * LANE
    row_in = lambda b, t: b * nt + jnp.minimum(t, nt - 1)
    row_out = lambda b, t: b * nt + jnp.maximum(t - 1, 0)
    zspec = lambda cb: pl.BlockSpec((tb, wide), lambda b, h, t: (row_in(b, t), cb // nsub + h))
    cwspec = lambda cb: pl.BlockSpec((CONV_WIDTH, wide), lambda b, h, t: (0, cb // nsub + h))
    vec = pl.BlockSpec((1, LANE), lambda b, h, t: (0, 0))
    return dict(
        setup=functools.partial(_gdn_setup, tb=tb, nsub=nsub, a_lane0=a_lane0, b_lane0=b_lane0,
                                q_scale=float(LANE) ** -0.5),
        grid=(batch, heads // nsub, nt + 1),
        in_specs=[zspec(cb_q), zspec(cb_q + heads), zspec(cb_q + 2 * heads), zspec(cb_gz),
                  pl.BlockSpec((tb, LANE), lambda b, h, t: (row_in(b, t), cb_sm)),
                  cwspec(0), cwspec(heads), cwspec(2 * heads), vec, vec, vec],
        operands=(za, za, za, za, zb, conv_w, conv_w, conv_w, alog_pad, dtb_pad, onorm_g),
        out_spec=pl.BlockSpec((tb, wide), lambda b, h, t: (row_out(b, t), h)),
        out_shape=jax.ShapeDtypeStruct((n, heads * LANE), BF16),
        scratch=[pltpu.VMEM((nsub, LANE, LANE), F32), pltpu.VMEM((nsub, 3, tb + 8, LANE), F32),
                 pltpu.VMEM((nsub, 2, 3, tb, LANE), BF16), pltpu.VMEM((nsub, 2, 3, tb, LANE), F32),
                 pltpu.VMEM((nsub, 2, tb, tb), BF16)])


def _mixers_kernel(*refs, parts, nt):
    t = pl.program_id(2)
    n_in = [len(p["in_specs"]) for p in parts]
    n_sc = [len(p["scratch"]) for p in parts]
    ins, refs = refs[:sum(n_in)], refs[sum(n_in):]
    outs, scr = refs[:len(parts)], refs[len(parts):]
    heads = []
    for k, p in enumerate(parts):
        i0, s0 = sum(n_in[:k]), sum(n_sc[:k])
        heads += p["setup"](t, *ins[i0:i0 + n_in[k]], outs[k], *scr[s0:s0 + n_sc[k]])
    _run_pipelined(t, nt, heads)


def _mixers(parts, nt, name):
    assert all(p["grid"] == parts[0]["grid"] for p in parts)
    meta = [dict(setup=p["setup"], in_specs=p["in_specs"], scratch=p["scratch"]) for p in parts]
    return pl.pallas_call(
        functools.partial(_mixers_kernel, parts=meta, nt=nt),
        grid=parts[0]["grid"],
        in_specs=[s for p in parts for s in p["in_specs"]],
        out_specs=[p["out_spec"] for p in parts],
        out_shape=[p["out_shape"] for p in parts],
        scratch_shapes=[s for p in parts for s in p["scratch"]],
        compiler_params=_cparams(("parallel", "parallel", "arbitrary")),
        name=name,
    )(*[o for p in parts for o in p["operands"]])


_RW_ROWS = ("w0", "a0", "k_k", "k_a", "r_k", "ln_w", "ln_b", "mu_r", "mu_k", "mu_v")


def _rwkv_setup(t, zr_ref, zk_ref, zv_ref, sm0_ref, sm1_ref, sm2_ref, sm3_ref, pv_ref, musm_ref,
                wup_ref, aup_ref, gup_ref, o_ref, h_ref, c_ref, pwx_ref, pbs_ref, puo_ref, phk_ref,
                parb_ref, pvec_ref, *, nsub, **kw):
    @pl.when(t == 0)
    def _():
        h_ref[...] = jnp.zeros_like(h_ref)
        c_ref[...] = jnp.zeros_like(c_ref)

    lanes = lambda s: slice(s * LANE, (s + 1) * LANE)
    return [_rwkv_head_pair(t, zr_ref.at[:, lanes(s)], zk_ref.at[:, lanes(s)], zv_ref.at[:, lanes(s)],
                            sm0_ref, sm1_ref, sm2_ref, sm3_ref, pv_ref.at[:, lanes(s)], musm_ref,
                            wup_ref.at[:, lanes(s)], aup_ref.at[:, lanes(s)], gup_ref.at[:, lanes(s)],
                            o_ref.at[:, lanes(s)], h_ref.at[s], c_ref.at[s], pwx_ref.at[s], pbs_ref.at[s],
                            puo_ref.at[s], phk_ref.at[s], parb_ref.at[s], pvec_ref.at[s], **kw)
            for s in range(nsub)]


def _rwkv_head_pair(t, zr_ref, zk_ref, zv_ref, sm0_ref, sm1_ref, sm2_ref, sm3_ref, pv_ref, musm_ref,
                    wup_ref, aup_ref, gup_ref, o_ref, h_ref, c_ref, pwx_ref, pbs_ref, puo_ref, phk_ref,
                    parb_ref, pvec_ref, *, tb, hd):
    C = CHUNK
    G2, SG = 2 * C, 4 * C
    n_groups = tb // G2

    pv = pv_ref[...]
    prow = lambda name: pv[_RW_ROWS.index(name):_RW_ROWS.index(name) + 1, :]
    rr = lax.broadcasted_iota(jnp.int32, (LANE, LANE), 0)
    cc = lax.broadcasted_iota(jnp.int32, (LANE, LANE), 1)
    bd_ones = ((rr // hd) == (cc // hd)).astype(BF16)
    lane = lax.broadcasted_iota(jnp.int32, (1, LANE), 1)
    m0 = (lane < hd).astype(F32)
    m1 = 1.0 - m0

    def phase_a(a, out):
        musm = musm_ref[...]
        row0 = lax.broadcasted_iota(jnp.int32, (tb, LANE), 0) == 0

        def shift(idx, ref, mu):
            x = ref[...]
            prev = jnp.where(row0, c_ref[idx, 0:1, :], pltpu.roll(x, 1, 0))
            c_ref[idx, 0:1, :] = x[tb - 1:tb, :]
            return x + (prev - x) * mu

        r_all = shift(0, zr_ref, prow("mu_r"))
        k_all = shift(1, zk_ref, prow("mu_k"))
        v_all = shift(2, zv_ref, prow("mu_v"))
        yield
        wd = shift(3, sm0_ref, musm[0:1, 0:LANE])
        ad = shift(4, sm1_ref, musm[0:1, LANE:2 * LANE])
        gd0 = shift(5, sm2_ref, musm[0:1, 2 * LANE:3 * LANE])
        gd1 = shift(6, sm3_ref, musm[0:1, 3 * LANE:4 * LANE])
        w_lin = prow("w0") + _dot(jnp.tanh(wd), wup_ref[...])
        logw = -jnp.exp(-_softplus(-w_lin) - 0.5)
        yield
        alr = _sigmoid(prow("a0") + _dot(ad, aup_ref[...]))
        gate = _dot(_sigmoid(gd0), gup_ref[0:LANE, :]) + _dot(_sigmoid(gd1), gup_ref[LANE:2 * LANE, :])
        yield
        kraw = k_all * prow("k_k")
        kk = kraw * lax.rsqrt(_dot_sel(bd_ones, kraw * kraw, 1, left=False) + 1e-6)
        keff = k_all * (1.0 + (alr - 1.0) * prow("k_a"))
        bonus = _dot_sel(bd_ones, r_all * keff * prow("r_k"), 2, left=False) * v_all
        yield
        rt_i = lax.broadcasted_iota(jnp.int32, (tb, tb), 0)
        ct_i = lax.broadcasted_iota(jnp.int32, (tb, tb), 1)
        same_chunk = (rt_i // C) == (ct_i // C)
        sums = _dot_sel(jnp.concatenate([jnp.logical_and(same_chunk, rt_i >= ct_i),
                                         same_chunk], axis=0).astype(BF16), logw, 2)
        cum, ctot = sums[:tb], sums[tb:]
        yield
        e_neg = jnp.exp(-cum)
        e_end = jnp.exp(ctot - cum)
        b_all = kk * alr
        a.update(v=v_all, rt=r_all * jnp.exp(cum), at=-kk * jnp.exp(cum - logw),
                 bt=b_all * e_neg, kt=keff * e_neg, bh=b_all * e_end, kh=keff * e_end)
        out.update(e_tot=jnp.exp(ctot), bonus=bonus, gate=gate)
        yield [phase_b(a, out, g) for g in range(n_groups)]

    def phase_b(a, out, g):
        r4 = lax.broadcasted_iota(jnp.int32, (SG, SG), 0)
        c4 = lax.broadcasted_iota(jnp.int32, (SG, SG), 1)
        same_unit = (r4 // C) == (c4 // C)
        strict = jnp.logical_and(same_unit, r4 > c4)
        incl = jnp.logical_and(same_unit, r4 >= c4)

        def halves(x):
            return x[g * G2:g * G2 + C], x[g * G2 + C:(g + 1) * G2]

        def stack(x):
            x0, x1 = halves(x)
            return jnp.concatenate([x0 * m0, x0 * m1, x1 * m0, x1 * m1], axis=0)

        def dup(x):
            x0, x1 = halves(x)
            return jnp.concatenate([x0, x0, x1, x1], axis=0)

        xa, xr = stack(a["at"]), stack(a["rt"])
        v_st, b_st, k_st = stack(a["v"]), stack(a["bh"]), stack(a["kh"])
        sc = _dot_nt(jnp.concatenate([xa, xr], axis=0),
                     jnp.concatenate([dup(a["bt"]), dup(a["kt"])], axis=0))
        yield
        sv = _dot(jnp.concatenate([jnp.where(strict, sc[:SG, SG:], 0.0),
                                   jnp.where(incl, sc[SG:, SG:], 0.0)], axis=0), v_st)
        av, ork = sv[:SG], sv[SG:]
        T = yield from _inv_unit_lower(jnp.where(strict, -sc[:SG, :SG], 0.0), SG, hd)
        tw = _dot(T, jnp.concatenate([xa, av], axis=1))
        wt, ut = tw[:, :LANE], tw[:, LANE:]
        yield
        hk = [_dot_tn(v_st[i * G2:(i + 1) * G2], k_st[i * G2:(i + 1) * G2]) for i in range(2)]
        out[g] = dict(wt=wt.astype(BF16), xr=xr.astype(BF16), b_st=b_st.astype(BF16), ut=ut, ork=ork,
                      hk=hk, a_rb=jnp.where(incl, sc[SG:, :SG], 0.0).astype(BF16))

    ws = lax.rem(t, 2)
    rs = 1 - ws

    def store(out):
        for g in range(n_groups):
            o = out[g]
            pwx_ref[ws, g, 0], pwx_ref[ws, g, 1], pbs_ref[ws, g] = o["wt"], o["xr"], o["b_st"]
            puo_ref[ws, g, 0], puo_ref[ws, g, 1] = o["ut"], o["ork"]
            phk_ref[ws, 2 * g], phk_ref[ws, 2 * g + 1] = o["hk"]
            parb_ref[ws, g] = o["a_rb"]
        pvec_ref[ws, 0], pvec_ref[ws, 1], pvec_ref[ws, 2] = out["e_tot"], out["bonus"], out["gate"]

    def load():
        p = {g: dict(wt=pwx_ref[rs, g, 0], xr=pwx_ref[rs, g, 1], b_st=pbs_ref[rs, g],
                     ut=puo_ref[rs, g, 0], ork=puo_ref[rs, g, 1],
                     hk=[phk_ref[rs, 2 * g], phk_ref[rs, 2 * g + 1]], a_rb=parb_ref[rs, g])
             for g in range(n_groups)}
        p.update(e_tot=pvec_ref[rs, 0], bonus=pvec_ref[rs, 1], gate=pvec_ref[rs, 2])
        return p

    def phase_c(p):
        H = h_ref[...]
        us, xrs = [], []
        for ci in range(tb // C):
            gr = p[ci // 2]
            rows = slice((ci % 2) * G2, (ci % 2 + 1) * G2)
            xh = _dot_nt(jnp.concatenate([gr["wt"][rows], gr["xr"][rows]], axis=0), H)
            yield
            u_c = xh[:G2] + gr["ut"][rows]
            H = (H * p["e_tot"][ci * C:ci * C + 1, :] + _dot_tn(u_c, gr["b_st"][rows])
                 + gr["hk"][ci % 2])
            us.append(u_c)
            xrs.append(xh[G2:])
            yield
        h_ref[...] = H
        ys = []
        for g in range(n_groups):
            o_st = (jnp.concatenate(xrs[2 * g:2 * g + 2], axis=0)
                    + _dot(p[g]["a_rb"], jnp.concatenate(us[2 * g:2 * g + 2], axis=0)) + p[g]["ork"])
            ys += [o_st[0:C] + o_st[C:2 * C], o_st[2 * C:3 * C] + o_st[3 * C:4 * C]]
            yield
        y = jnp.concatenate(ys, axis=0)
        mu = _dot_sel(bd_ones, y, 2, left=False) * (1.0 / hd)
        yield
        yc = y - mu
        var = _dot_sel(bd_ones, yc * yc, 1, left=False) * (1.0 / hd)
        yield
        y = yc * lax.rsqrt(var + RWKV_GN_EPS) * prow("ln_w") + prow("ln_b") + p["bonus"]
        o_ref[...] = (y * p["gate"]).astype(o_ref.dtype)

    return (lambda out: phase_a({}, out)), phase_c, load, store


def _rwkv_branch(z, pv, musm, wup, aup, gup, *, batch, seq, width, hd, cb_r, cb_sm, tb):
    n = batch * seq
    nt = seq // tb
    hp = width // LANE
    nsub = SEQS_PER_STEP if hp % SEQS_PER_STEP == 0 else 1
    wide = nsub * LANE
    row_in = lambda b, t: b * nt + jnp.minimum(t, nt - 1)
    row_out = lambda b, t: b * nt + jnp.maximum(t - 1, 0)
    zspec = lambda cb: pl.BlockSpec((tb, wide), lambda b, p, t: (row_in(b, t), cb // nsub + p))
    smspec = lambda i: pl.BlockSpec((tb, LANE), lambda b, p, t: (row_in(b, t), cb_sm + i))
    colblk = lambda rows: pl.BlockSpec((rows, wide), lambda b, p, t: (0, p))
    ng, sg = tb // (2 * CHUNK), 4 * CHUNK
    return dict(
        setup=functools.partial(_rwkv_setup, tb=tb, nsub=nsub, hd=hd),
        grid=(batch, hp // nsub, nt + 1),
        in_specs=[zspec(cb_r), zspec(cb_r + hp), zspec(cb_r + 2 * hp),
                  smspec(0), smspec(1), smspec(2), smspec(3),
                  colblk(pv.shape[0]),
                  pl.BlockSpec(musm.shape, lambda b, p, t: (0, 0)),
                  colblk(LANE), colblk(LANE), colblk(2 * LANE)],
        operands=(z, z, z, z, z, z, z, pv, musm, wup, aup, gup),
        out_spec=pl.BlockSpec((tb, wide), lambda b, p, t: (row_out(b, t), p)),
        out_shape=jax.ShapeDtypeStruct((n, width), BF16),
        scratch=[pltpu.VMEM((nsub, LANE, LANE), F32), pltpu.VMEM((nsub, 7, 8, LANE), F32),
                 pltpu.VMEM((nsub, 2, ng, 2, sg, LANE), BF16),
                 pltpu.VMEM((nsub, 2, ng, sg, LANE), BF16),
                 pltpu.VMEM((nsub, 2, ng, 2, sg, LANE), F32),
                 pltpu.VMEM((nsub, 2, 2 * ng, LANE, LANE), F32),
                 pltpu.VMEM((nsub, 2, ng, sg, sg), BF16), pltpu.VMEM((nsub, 2, 3, tb, LANE), F32)])


class _Layout:
    def __init__(self, d_model, gdn_heads, rwkv_width, decay_lora, aaa_lora, gate_lora, tn):
        self.d_model, self.gh, self.rw = d_model, gdn_heads, rwkv_width
        self.dl, self.al, self.gl = decay_lora, aaa_lora, gate_lora
        self.gw = gdn_heads * LANE
        assert decay_lora + 2 * gdn_heads <= LANE and aaa_lora <= LANE and gate_lora == 2 * LANE
        self.a_lane0 = decay_lora
        self.b_lane0 = decay_lora + gdn_heads
        self.cb_q = 0
        self.cb_gz = 3 * self.gw // LANE
        self.cb_r = 0
        self.cb_sm = 3 * rwkv_width // LANE
        self.off_z = 3 * self.gw
        self.off_a = self.off_z + self.gw
        self.off_b = self.off_a + gdn_heads
        self.off_rwkv = self.off_b + gdn_heads
        self.off_gates = self.off_rwkv + 3 * rwkv_width + decay_lora + aaa_lora + gate_lora

    def split_cols(self, w):
        lead = w.shape[:-1]
        zeros = lambda n: jnp.zeros(lead + (n,), w.dtype)
        rw, dl, al, gl, gh = self.rw, self.dl, self.al, self.gl, self.gh
        o = self.off_rwkv + 3 * rw
        piece_b = jnp.concatenate(
            [w[..., self.off_rwkv:o],
             w[..., o:o + dl], w[..., self.off_a:self.off_b], w[..., self.off_b:self.off_rwkv],
             zeros(LANE - dl - 2 * gh),
             w[..., o + dl:o + dl + al], zeros(LANE - al),
             w[..., o + dl + al:o + dl + al + gl]], axis=-1)
        return w[..., :self.off_a], piece_b, w[..., self.off_gates:]


def _pad_rows(w, rows):
    return jnp.concatenate([w, jnp.zeros((rows - w.shape[0],) + w.shape[1:], w.dtype)], axis=0)


def _gdn_from_packed(za, zb, p, lay, batch, seq, tb=256):
    lane_vec = lambda v, lane0: jnp.zeros((1, LANE), F32).at[0, lane0:lane0 + v.shape[0]].set(v)
    return _gdn_branch(za, zb, p["conv_w"], lane_vec(p["gdn_a_log"], lay.a_lane0),
                       lane_vec(p["gdn_dt_bias"], lay.a_lane0), p["gdn_onorm_g"].reshape(1, LANE),
                       batch=batch, seq=seq, heads=lay.gh, cb_q=lay.cb_q, cb_gz=lay.cb_gz,
                       cb_sm=lay.cb_sm, a_lane0=lay.a_lane0, b_lane0=lay.b_lane0, tb=min(tb, seq))


def _rwkv_from_packed(z, p, lay, batch, seq, tb=256):
    rw = lay.rw
    hd = p["rwkv_r_k"].shape[-1]
    assert LANE % hd == 0 and hd * 2 == LANE
    mu = p["rwkv_mu"]
    rows = {"w0": p["rwkv_w0"], "a0": p["rwkv_a0"], "k_k": p["rwkv_k_k"], "k_a": p["rwkv_k_a"],
            "r_k": p["rwkv_r_k"].reshape(rw), "ln_w": p["rwkv_ln_w"], "ln_b": p["rwkv_ln_b"],
            "mu_r": mu[:rw], "mu_k": mu[rw:2 * rw], "mu_v": mu[2 * rw:3 * rw]}
    pv = _pad_rows(jnp.stack([rows[k] for k in _RW_ROWS]).astype(F32), 16)
    o = 3 * rw
    zl = lambda n: jnp.zeros((n,), F32)
    musm = jnp.concatenate([mu[o:o + lay.dl], zl(LANE - lay.dl), mu[o + lay.dl:o + lay.dl + lay.al],
                            zl(LANE - lay.al), mu[o + lay.dl + lay.al:]])
    musm = _pad_rows(musm.reshape(1, 4 * LANE), 8)
    wup = _pad_rows(p["rwkv_w_up"], LANE).astype(BF16)
    aup = _pad_rows(p["rwkv_a_up"], LANE).astype(BF16)
    gup = p["rwkv_g_up"].astype(BF16)
    return _rwkv_branch(z, pv, musm, wup, aup, gup, batch=batch, seq=seq, width=rw, hd=hd,
                        cb_r=lay.cb_r, cb_sm=lay.cb_sm, tb=min(tb, seq))


def _make_layout(p, tn):
    return _Layout(p["norm1_g"].shape[-1], p["gdn_a_log"].shape[-1], p["rwkv_w0"].shape[-1],
                   p["rwkv_w_up"].shape[0], p["rwkv_a_up"].shape[0], p["rwkv_g_up"].shape[0], tn)


def _ada_kernel(c_ref, w_ref, b_ref, o_ref):
    o_ref[...] = _dot(_silu(c_ref[...]), w_ref[...]) + b_ref[...]


def _ada_mod(c_pad, w_ada, b_ada, tn):
    d, cols = w_ada.shape
    return pl.pallas_call(
        _ada_kernel,
        grid=(cols // tn,),
        in_specs=[pl.BlockSpec((8, d), lambda j: (0, 0)),
                  pl.BlockSpec((d, tn), lambda j: (0, j)),
                  pl.BlockSpec((1, tn), lambda j: (0, j))],
        out_specs=pl.BlockSpec((8, tn), lambda j: (0, j)),
        out_shape=jax.ShapeDtypeStruct((8, cols), F32),
        compiler_params=_cparams(("arbitrary",)),
        name="ada_mod",
    )(c_pad, w_ada, b_ada.reshape(1, cols))


def _modnorm(x, g, sc, sh):
    y = x * lax.rsqrt(jnp.mean(x * x, axis=-1, keepdims=True) + RMS_EPS)
    return y * g * (1.0 + sc) + sh


def _norm1_kernel(x_ref, g_ref, sh_ref, sc_ref, o_ref):
    b = pl.program_id(0)
    h = _modnorm(x_ref[...], g_ref[...], sc_ref[pl.ds(b, 1), :], sh_ref[pl.ds(b, 1), :])
    o_ref[...] = h.astype(o_ref.dtype)


def _norm1(x2, g, mod, *, batch, seq, tm):
    n, d = x2.shape
    nt = seq // tm
    return pl.pallas_call(
        _norm1_kernel,
        grid=(batch, nt),
        in_specs=[pl.BlockSpec((tm, d), lambda b, t: (b * nt + t, 0)),
                  pl.BlockSpec((1, d), lambda b, t: (0, 0)),
                  pl.BlockSpec((8, d), lambda b, t: (0, 0)),
                  pl.BlockSpec((8, d), lambda b, t: (0, 1))],
        out_specs=pl.BlockSpec((tm, d), lambda b, t: (b * nt + t, 0)),
        out_shape=jax.ShapeDtypeStruct((n, d), BF16),
        compiler_params=_cparams(("parallel", "arbitrary")),
        name="norm1",
    )(x2, g, mod, mod)


def _merge_kernel(o_ref, y_ref, wa_ref, wb_ref, ga_ref, gb_ref, m_ref):
    ha = jnp.dot(o_ref[...], wa_ref[...], preferred_element_type=F32)
    hb = jnp.dot(y_ref[...], wb_ref[...], preferred_element_type=F32)
    m_ref[...] = (ga_ref[...].astype(F32) * ha + gb_ref[...].astype(F32) * hb).astype(m_ref.dtype)


def _merge(o, y, wa, wb, gates, *, tm, tn):
    n, d = o.shape[0], wa.shape[1]
    tn = math.gcd(tn, d)
    gb0 = 0
    return pl.pallas_call(
        _merge_kernel,
        grid=(n // tm, d // tn),
        in_specs=[pl.BlockSpec((tm, o.shape[1]), lambda i, j: (i, 0)),
                  pl.BlockSpec((tm, y.shape[1]), lambda i, j: (i, 0)),
                  pl.BlockSpec((wa.shape[0], tn), lambda i, j: (0, j)),
                  pl.BlockSpec((wb.shape[0], tn), lambda i, j: (0, j)),
                  pl.BlockSpec((tm, tn), lambda i, j: (i, gb0 + j)),
                  pl.BlockSpec((tm, tn), lambda i, j: (i, gb0 + d // tn + j))],
        out_specs=pl.BlockSpec((tm, tn), lambda i, j: (i, j)),
        out_shape=jax.ShapeDtypeStruct((n, d), BF16),
        compiler_params=_cparams(("parallel", "arbitrary")),
        name="merge",
    )(o, y, wa, wb, gates, gates)


def _outproj_kernel(m_ref, w_ref, x_ref, g_ref, o_ref, *, tiles_per_batch):
    b = pl.program_id(0) // tiles_per_batch
    y = jnp.dot(m_ref[...], w_ref[...], preferred_element_type=F32)
    o_ref[...] = x_ref[...] + g_ref[pl.ds(b, 1), :] * y


def _outproj(m, w, x2, mod, *, seq, tm, tn):
    n, d = x2.shape
    gcol = 2 * d // tn
    return pl.pallas_call(
        functools.partial(_outproj_kernel, tiles_per_batch=seq // tm),
        grid=(n // tm, d // tn),
        in_specs=[pl.BlockSpec((tm, d), lambda i, j: (i, 0)),
                  pl.BlockSpec((d, tn), lambda i, j: (0, j)),
                  pl.BlockSpec((tm, tn), lambda i, j: (i, j)),
                  pl.BlockSpec((8, tn), lambda i, j: (0, gcol + j))],
        out_specs=pl.BlockSpec((tm, tn), lambda i, j: (i, j)),
        out_shape=jax.ShapeDtypeStruct((n, d), F32),
        compiler_params=_cparams(("parallel", "arbitrary")),
        name="outproj",
    )(m, w, x2, mod)


def _router_kernel(x_ref, g_ref, sh_ref, sc_ref, wr_ref, br_ref, h_ref, id_ref, wt_ref, cnt_out_ref,
                   cnt_ref, *, n_groups, per_group):
    b = pl.program_id(0)
    h = _modnorm(x_ref[...], g_ref[...], sc_ref[pl.ds(b, 1), :], sh_ref[pl.ds(b, 1), :])
    h_ref[...] = _pack_bf16_pairs(h)
    logits = _dot(h, wr_ref[...]) + br_ref[...]
    tm = logits.shape[0]
    lane = lax.broadcasted_iota(jnp.int32, (tm, LANE), 1)
    neg = jnp.float32(-jnp.inf)
    big = jnp.int32(LANE)

    def first_argmax(vals, mask):
        vm = jnp.where(mask, vals, neg)
        mx = jnp.max(vm, axis=1, keepdims=True)
        idx = jnp.min(jnp.where(jnp.logical_and(mask, vm == mx), lane, big), axis=1, keepdims=True)
        return mx, idx

    gmask = lane < n_groups
    gmax, g_sel = first_argmax(logits, gmask)
    g_prob = 1.0 / jnp.sum(jnp.where(gmask, jnp.exp(logits - gmax), 0.0), axis=1, keepdims=True)
    e_lane = lane - n_groups
    emask = jnp.logical_and(e_lane >= g_sel * per_group, e_lane < (g_sel + 1) * per_group)
    emax = jnp.max(jnp.where(emask, logits, neg), axis=1, keepdims=True)
    ex = jnp.where(emask, jnp.exp(logits - emax), 0.0)
    probs = ex / jnp.sum(ex, axis=1, keepdims=True)
    p0, i0 = first_argmax(probs, emask)
    p1, i1 = first_argmax(probs, jnp.logical_and(emask, lane != i0))
    den = p0 + p1
    wt_ref[...] = jnp.where(lane == 0, g_prob * (p0 / den), jnp.where(lane == 1, g_prob * (p1 / den), 0.0))

    @pl.when(jnp.logical_and(b == 0, pl.program_id(1) == 0))
    def _():
        cnt_ref[...] = jnp.zeros_like(cnt_ref)

    e0, e1 = i0 - n_groups, i1 - n_groups
    hit0, hit1 = lane == e0, lane == e1
    hits = jnp.logical_or(hit0, hit1).astype(BF16)
    rr = lax.broadcasted_iota(jnp.int32, (tm, tm), 0)
    cc = lax.broadcasted_iota(jnp.int32, (tm, tm), 1)
    before = jnp.dot((rr > cc).astype(BF16), hits, preferred_element_type=F32) + cnt_ref[0:1, :]
    rank0 = jnp.sum(jnp.where(hit0, before, 0.0), axis=1, keepdims=True).astype(jnp.int32)
    rank1 = jnp.sum(jnp.where(hit1, before, 0.0), axis=1, keepdims=True).astype(jnp.int32)
    cnt_ref[0:1, :] = cnt_ref[0:1, :] + jnp.sum(hits.astype(F32), axis=0, keepdims=True)
    id_ref[...] = jnp.where(lane == 0, e0, jnp.where(lane == 1, e1,
                            jnp.where(lane == 2, rank0, jnp.where(lane == 3, rank1, 0))))
    cnt_out_ref[...] = cnt_ref[...].astype(jnp.int32)


def _router(x1, g, mod, wr, br, *, batch, seq, tm, n_groups, per_group):
    n, d = x1.shape
    nt = seq // tm
    rowblk = lambda w: pl.BlockSpec((tm, w), lambda b, t: (b * nt + t, 0))
    return pl.pallas_call(
        functools.partial(_router_kernel, n_groups=n_groups, per_group=per_group),
        grid=(batch, nt),
        in_specs=[rowblk(d),
                  pl.BlockSpec((1, d), lambda b, t: (0, 0)),
                  pl.BlockSpec((8, d), lambda b, t: (0, 3)),
                  pl.BlockSpec((8, d), lambda b, t: (0, 4)),
                  pl.BlockSpec((d, LANE), lambda b, t: (0, 0)),
                  pl.BlockSpec((1, LANE), lambda b, t: (0, 0))],
        out_specs=[rowblk(d // 2), rowblk(LANE), rowblk(LANE), pl.BlockSpec((8, LANE), lambda b, t: (0, 0))],
        out_shape=[jax.ShapeDtypeStruct((n, d // 2), jnp.uint32), jax.ShapeDtypeStruct((n, LANE), jnp.int32),
                   jax.ShapeDtypeStruct((n, LANE), F32), jax.ShapeDtypeStruct((8, LANE), jnp.int32)],
        scratch_shapes=[pltpu.VMEM((8, LANE), F32)],
        compiler_params=_cparams(("arbitrary", "arbitrary")),
        name="router",
    )(x1, g, mod, mod, wr, br)


def _expert_kernel(be_ref, tok_ref, nused_ref, h_hbm, w1_ref, w3_ref, w2_ref, bw_ref, o_ref, xbuf, sem):
    i = pl.program_id(0)
    nb = nused_ref[0]

    def row_copy(blk, slot, r):
        tok = tok_ref[blk * MOE_BLOCK + r]
        return pltpu.make_async_copy(h_hbm.at[pl.ds(tok, 1), :], xbuf.at[slot, pl.ds(r, 1), :],
                                     sem.at[slot])

    def start_block(blk, slot):
        def body(r, carry):
            row_copy(blk, slot, r).start()
            return carry
        lax.fori_loop(0, MOE_BLOCK, body, 0, unroll=DMA_UNROLL)

    for ahead in range(GATHER_DEPTH - 1):
        @pl.when(jnp.logical_and(i == 0, ahead < nb))
        def _():
            start_block(ahead, ahead)

    @pl.when(i + GATHER_DEPTH - 1 < nb)
    def _():
        start_block(i + GATHER_DEPTH - 1, (i + GATHER_DEPTH - 1) % GATHER_DEPTH)

    slot = i % GATHER_DEPTH

    @pl.when(i < nb)
    def _():
        def wait_body(r, carry):
            row_copy(i, slot, r).wait()
            return carry
        lax.fori_loop(0, MOE_BLOCK, wait_body, 0, unroll=DMA_UNROLL)

        x_lo, x_hi = _unpack_bf16_pairs(xbuf[slot])
        half = x_lo.shape[1]
        a = _dot(x_lo, w1_ref[0:half, :]) + _dot(x_hi, w1_ref[half:, :])
        g = _dot(x_lo, w3_ref[0:half, :]) + _dot(x_hi, w3_ref[half:, :])
        y = _dot(_silu(a) * g, w2_ref[...])
        o_ref[...] = _pack_bf16_pairs(y * bw_ref[...])

    @pl.when(i >= nb)
    def _():
        o_ref[...] = jnp.zeros_like(o_ref)


def _experts(h2, w1, w3, w2, block_expert, buf_tok, n_used, buf_w):
    half = h2.shape[1]
    d = 2 * half
    de = w1.shape[2]
    p_rows = buf_tok.shape[0]
    wspec = lambda shape: pl.BlockSpec((None,) + shape, lambda i, be, tok, nu: (be[i], 0, 0))
    grid_spec = pltpu.PrefetchScalarGridSpec(
        num_scalar_prefetch=3,
        grid=(p_rows // MOE_BLOCK,),
        in_specs=[pl.BlockSpec(memory_space=pl.ANY),
                  wspec((d, de)), wspec((d, de)), wspec((de, d)),
                  pl.BlockSpec((MOE_BLOCK, 1), lambda i, be, tok, nu: (i, 0))],
        out_specs=pl.BlockSpec((MOE_BLOCK, half), lambda i, be, tok, nu: (i, 0)),
        scratch_shapes=[pltpu.VMEM((GATHER_DEPTH, MOE_BLOCK, half), jnp.uint32),
                        pltpu.SemaphoreType.DMA((GATHER_DEPTH,))],
    )
    return pl.pallas_call(
        _expert_kernel,
        grid_spec=grid_spec,
        out_shape=jax.ShapeDtypeStruct((p_rows, half), jnp.uint32),
        compiler_params=_cparams(("arbitrary",)),
        name="experts",
    )(block_expert, buf_tok, n_used, h2, w1, w3, w2, buf_w)


def _combine_kernel(pos_ref, x_ref, g_ref, gf_ref, y_hbm, o_ref, ybuf, sem, *, tm, tiles_per_batch):
    i = pl.program_id(0)
    nb = pl.num_programs(0)
    rows = TOP_K * tm

    def row_copy(blk, slot, r):
        return pltpu.make_async_copy(y_hbm.at[pl.ds(pos_ref[blk * rows + r], 1), :],
                                     ybuf.at[slot, pl.ds(r, 1), :], sem.at[slot])

    def start_block(blk, slot):
        def body(r, carry):
            row_copy(blk, slot, r).start()
            return carry
        lax.fori_loop(0, rows, body, 0, unroll=DMA_UNROLL)

    @pl.when(i == 0)
    def _():
        start_block(0, 0)

    @pl.when(i + 1 < nb)
    def _():
        start_block(i + 1, (i + 1) % 2)

    slot = i % 2

    def wait_body(r, carry):
        row_copy(i, slot, r).wait()
        return carry
    lax.fori_loop(0, rows, wait_body, 0, unroll=DMA_UNROLL)

    b = i // tiles_per_batch
    lo0, hi0 = _unpack_bf16_pairs(ybuf[slot, 0:tm, :])
    lo1, hi1 = _unpack_bf16_pairs(ybuf[slot, tm:rows, :])
    moe = jnp.concatenate([lo0 + lo1, hi0 + hi1], axis=1)
    x = x_ref[...] + g_ref[pl.ds(b, 1), :] * moe
    o_ref[...] = x * lax.rsqrt(jnp.mean(x * x, axis=-1, keepdims=True) + RMS_EPS) * gf_ref[...]


def _combine(pos, x1, mod, gf, yb, *, seq, tm):
    n, d = x1.shape
    grid_spec = pltpu.PrefetchScalarGridSpec(
        num_scalar_prefetch=1,
        grid=(n // tm,),
        in_specs=[pl.BlockSpec((tm, d), lambda i, pos: (i, 0)),
                  pl.BlockSpec((8, d), lambda i, pos: (0, 5)),
                  pl.BlockSpec((1, d), lambda i, pos: (0, 0)),
                  pl.BlockSpec(memory_space=pl.ANY)],
        out_specs=pl.BlockSpec((tm, d), lambda i, pos: (i, 0)),
        scratch_shapes=[pltpu.VMEM((2, TOP_K * tm, d // 2), jnp.uint32), pltpu.SemaphoreType.DMA((2,))],
    )
    return pl.pallas_call(
        functools.partial(_combine_kernel, tm=tm, tiles_per_batch=seq // tm),
        grid_spec=grid_spec,
        out_shape=jax.ShapeDtypeStruct((n, d), F32),
        compiler_params=_cparams(("arbitrary",)),
        name="combine",
    )(pos, x1, mod, gf, yb)


def _dispatch(ids, rank, wts, counts):
    n = ids.shape[0]
    n_experts = counts.shape[0]
    a = n * TOP_K
    expert_id = ids.reshape(a)
    w_flat = wts.reshape(a)
    order = jnp.argsort(expert_id).astype(jnp.int32)
    ei = jnp.arange(n_experts, dtype=jnp.int32)
    upto = ei[None, :] <= ei[:, None]
    padded = (counts + MOE_BLOCK - 1) // MOE_BLOCK * MOE_BLOCK
    c_start = jnp.sum(jnp.where(upto, counts[None, :], 0), axis=1) - counts
    p_end = jnp.sum(jnp.where(upto, padded[None, :], 0), axis=1)
    p_start = p_end - padded
    pos = rank.reshape(a) + p_start[expert_id]
    p_rows = (a + n_experts * (MOE_BLOCK - 1) + MOE_BLOCK - 1) // MOE_BLOCK * MOE_BLOCK
    n_blocks = p_rows // MOE_BLOCK
    starts = jnp.arange(n_blocks, dtype=jnp.int32) * MOE_BLOCK
    block_expert = jnp.minimum(jnp.sum((p_end[None, :] <= starts[:, None]).astype(jnp.int32), axis=1),
                               n_experts - 1)
    slot = jnp.arange(p_rows, dtype=jnp.int32)
    per_slot = lambda table: jnp.repeat(table[block_expert], MOE_BLOCK)
    within = slot - per_slot(p_start)
    valid = within < per_slot(counts)
    src = order[jnp.clip(within + per_slot(c_start), 0, a - 1)]
    buf_tok = jnp.where(valid, src // TOP_K, 0)
    buf_w = jnp.where(valid, w_flat[src], 0.0)
    n_used = (p_end[n_experts - 1:] // MOE_BLOCK).astype(jnp.int32)
    return pos, buf_tok, buf_w, block_expert, n_used


def _pick(n, pref):
    t = min(n, pref)
    while n % t:
        t //= 2
    return t


def kernel(x, c, w_ada, b_ada, norm1_g, w_in, conv_w, gdn_a_log, gdn_dt_bias, gdn_onorm_g, rwkv_mu,
           rwkv_w0, rwkv_w_up, rwkv_a0, rwkv_a_up, rwkv_g_up, rwkv_k_k, rwkv_k_a, rwkv_r_k, rwkv_ln_w,
           rwkv_ln_b, w_gdn_o, w_rwkv_o, w_out, norm2_g, w_group, b_group, w_expert, b_expert, w1, w3,
           w2, norm_f_g):
    batch, seq, d = x.shape
    n = batch * seq
    depth = w_ada.shape[0]
    assert depth == 1, "the final norm is fused into the last layer's combine"
    assert batch <= 8 and seq % CHUNK == 0 and d % 512 == 0
    x2 = x.reshape(n, d)
    c_pad = _pad_rows(c, 8)
    tm_big = _pick(seq, 1024)
    tm_row = _pick(seq, 256)
    tn = 512
    for l in range(depth):
        p = {"norm1_g": norm1_g[l], "conv_w": conv_w[l], "gdn_a_log": gdn_a_log[l],
             "gdn_dt_bias": gdn_dt_bias[l], "gdn_onorm_g": gdn_onorm_g[l], "rwkv_mu": rwkv_mu[l],
             "rwkv_w0": rwkv_w0[l], "rwkv_w_up": rwkv_w_up[l], "rwkv_a0": rwkv_a0[l],
             "rwkv_a_up": rwkv_a_up[l], "rwkv_g_up": rwkv_g_up[l], "rwkv_k_k": rwkv_k_k[l],
             "rwkv_k_a": rwkv_k_a[l], "rwkv_r_k": rwkv_r_k[l], "rwkv_ln_w": rwkv_ln_w[l],
             "rwkv_ln_b": rwkv_ln_b[l]}
        lay = _make_layout(p, tn)
        mod = _ada_mod(c_pad, w_ada[l], b_ada[l], tn)

        h1 = _norm1(x2, norm1_g[l].reshape(1, d), mod, batch=batch, seq=seq, tm=tm_row)
        w_a, w_b, w_c = lay.split_cols(w_in[l])
        n_exp, _, d_exp = w1.shape[1:]
        flat = lambda w: w.reshape(-1, w.shape[-1])
        za, w1b, w2b, w_b, w_c = _matmul(h1, w_a.astype(BF16), tm_big, tn, F32, name="in_proj_gdn",
                                         cast=[flat(w1[l]), flat(w2[l]), w_b, w_c])
        zb = _matmul(h1, w_b, tm_big, tn, F32, name="in_proj_rwkv")
        gates, w3b = _matmul(h1, w_c, tm_big, tn, BF16, act=_sigmoid,
                             name="in_proj_gates", cast=[flat(w3[l])])
        w1b, w3b = w1b.reshape(n_exp, d, d_exp), w3b.reshape(n_exp, d, d_exp)
        w2b = w2b.reshape(n_exp, d_exp, d)
        tb = min(256, seq)
        o_gdn, y_rwkv = _mixers([_gdn_from_packed(za, zb, p, lay, batch, seq, tb),
                                 _rwkv_from_packed(zb, p, lay, batch, seq, tb)], seq // tb, "mixers")
        m = _merge(o_gdn, y_rwkv, w_gdn_o[l].astype(BF16), w_rwkv_o[l].astype(BF16), gates,
                   tm=tm_big, tn=tn)
        x1 = _outproj(m, w_out[l].astype(BF16), x2, mod, seq=seq, tm=tm_big, tn=tn)

        n_groups, n_experts = w_group.shape[-1], w_expert.shape[-1]
        assert n_groups + n_experts <= LANE
        wr = jnp.concatenate([w_group[l], w_expert[l],
                              jnp.zeros((d, LANE - n_groups - n_experts), F32)], axis=1).astype(BF16)
        br = jnp.concatenate([b_group[l], b_expert[l],
                              jnp.zeros((LANE - n_groups - n_experts,), F32)]).reshape(1, LANE)
        h2, ids, wts, cnt = _router(x1, norm2_g[l].reshape(1, d), mod, wr, br, batch=batch, seq=seq,
                                    tm=tm_row, n_groups=n_groups, per_group=n_experts // n_groups)
        pos, buf_tok, buf_w, block_expert, n_used = _dispatch(
            ids[:, :TOP_K], ids[:, TOP_K:2 * TOP_K], wts[:, :TOP_K], cnt[0, :n_experts])
        yb = _experts(h2, w1b, w3b, w2b, block_expert, buf_tok, n_used, buf_w.reshape(-1, 1))
        tm_c = _pick(seq, 128)
        pos_tiles = pos.reshape(n // tm_c, tm_c, TOP_K).transpose(0, 2, 1).reshape(n * TOP_K)
        x2 = _combine(pos_tiles, x1, mod, norm_f_g.reshape(1, d), yb, seq=seq, tm=tm_c)
    return x2.reshape(batch, seq, d)
```

```python
import functools
import math

import jax
import jax.numpy as jnp
from jax import lax
from jax.experimental import pallas as pl
from jax.experimental.pallas import tpu as pltpu

F32 = jnp.float32
BF16 = jnp.bfloat16

LANE = 128
CHUNK = 64
CONV_WIDTH = 4
RMS_EPS = 1e-6
RWKV_GN_EPS = 64e-5
MOE_BLOCK = 128
TOP_K = 2
DMA_UNROLL = 8
GATHER_DEPTH = 3
SEQS_PER_STEP = 4
VMEM_LIMIT = 56 * 1024 * 1024


def _cparams(sem):
    return pltpu.CompilerParams(dimension_semantics=sem, vmem_limit_bytes=VMEM_LIMIT)


def _dot(a, b):
    return jnp.dot(a.astype(BF16), b.astype(BF16), preferred_element_type=F32)


def _dot_nt(a, b):
    return lax.dot_general(a.astype(BF16), b.astype(BF16), (((1,), (1,)), ((), ())),
                           preferred_element_type=F32)


def _dot_tn(a, b):
    return lax.dot_general(a.astype(BF16), b.astype(BF16), (((0,), (0,)), ((), ())),
                           preferred_element_type=F32)


def _pack_bf16_pairs(x):
    w = x.shape[1] // 2
    bits = lax.bitcast_convert_type(x.astype(BF16).astype(F32), jnp.uint32)
    return (bits[:, w:] & jnp.uint32(0xFFFF0000)) | (bits[:, :w] >> 16)


def _unpack_bf16_pairs(words):
    lo = lax.bitcast_convert_type(words << 16, F32)
    hi = lax.bitcast_convert_type(words & jnp.uint32(0xFFFF0000), F32)
    return lo, hi


def _split(x, terms):
    pieces = []
    for _ in range(terms - 1):
        hi = x.astype(BF16)
        pieces.append(hi)
        x = x - hi.astype(F32)
    pieces.append(x.astype(BF16))
    return pieces


def _dot_sel(sel, x, terms, left=True):
    out = None
    for piece in _split(x, terms):
        ops = (sel, piece) if left else (piece, sel)
        d = jnp.dot(*ops, preferred_element_type=F32)
        out = d if out is None else out + d
    return out


def _sigmoid(x):
    return 1.0 / (1.0 + jnp.exp(-x))


def _silu(x):
    return x * _sigmoid(x)


def _softplus(x):
    return jnp.maximum(x, 0.0) + jnp.log(1.0 + jnp.exp(-jnp.abs(x)))


def _inv_unit_lower(L, n, top):
    r = lax.broadcasted_iota(jnp.int32, (n, n), 0)
    c = lax.broadcasted_iota(jnp.int32, (n, n), 1)
    eye = (r == c).astype(F32)
    same16 = (r // 16) == (c // 16)
    Ld = jnp.where(same16, L, 0.0)
    X = eye - Ld
    P = _dot(Ld, Ld)
    yield
    for _ in range(2):
        X = X + _dot(X, P)
        P = _dot(P, P)
        yield
    X = X + _dot(X, P)
    yield
    bs = 32
    while bs <= top:
        inner = (r // (bs // 2)) == (c // (bs // 2))
        outer = (r // bs) == (c // bs)
        Lo = jnp.where(jnp.logical_and(outer, jnp.logical_not(inner)), L, 0.0)
        Y = _dot(Lo, X)
        yield
        X = X - _dot(X, Y)
        yield
        bs *= 2
    return X


def _round_robin(gens):
    gens = list(gens)
    while gens:
        for g in list(gens):
            try:
                more = next(g)
            except StopIteration:
                gens.remove(g)
                continue
            if more:
                gens.extend(more)


def _mm_kernel(a_ref, w_ref, *refs, act, n_cast, w_rows_are_outputs):
    cast_in, o_ref, cast_out = refs[:n_cast], refs[n_cast], refs[n_cast + 1:]
    contract_w = 1 if w_rows_are_outputs else 0
    y = lax.dot_general(a_ref[...], w_ref[...], (((1,), (contract_w,)), ((), ())),
                        preferred_element_type=F32)
    o_ref[...] = (act(y) if act else y).astype(o_ref.dtype)
    for src, dst in zip(cast_in, cast_out):
        dst[...] = src[...].astype(dst.dtype)


def _matmul(a, w, tm, tn, out_dtype, act=None, name="matmul", cast=(), w_rows_are_outputs=False):
    M, K = a.shape
    Nn = w.shape[0] if w_rows_are_outputs else w.shape[1]
    tn = math.gcd(tn, Nn)
    nj = Nn // tn
    steps = (M // tm) * nj
    cast = [c for c in cast]
    assert all(c.shape[0] % (16 * steps) == 0 for c in cast)
    slab = lambda c: pl.BlockSpec((c.shape[0] // steps, c.shape[1]), lambda i, j: (i * nj + j, 0))
    w_spec = (pl.BlockSpec((tn, K), lambda i, j: (j, 0)) if w_rows_are_outputs
              else pl.BlockSpec((K, tn), lambda i, j: (0, j)))
    outs = pl.pallas_call(
        functools.partial(_mm_kernel, act=act, n_cast=len(cast), w_rows_are_outputs=w_rows_are_outputs),
        grid=(M // tm, nj),
        in_specs=[pl.BlockSpec((tm, K), lambda i, j: (i, 0)), w_spec] + [slab(c) for c in cast],
        out_specs=[pl.BlockSpec((tm, tn), lambda i, j: (i, j))] + [slab(c) for c in cast],
        out_shape=[jax.ShapeDtypeStruct((M, Nn), out_dtype)]
                  + [jax.ShapeDtypeStruct(c.shape, BF16) for c in cast],
        compiler_params=_cparams(("parallel", "arbitrary")),
        name=name,
    )(a, w, *cast)
    return outs if cast else outs[0]


def _run_pipelined(t, nt, heads):
    def prepare():
        outs = [{} for _ in heads]
        return outs, [hd[0](out) for hd, out in zip(heads, outs)]

    def finish(outs):
        for hd, out in zip(heads, outs):
            hd[3](out)

    @pl.when(t == 0)
    def _():
        outs, gens = prepare()
        _round_robin(gens)
        finish(outs)

    @pl.when(jnp.logical_and(t > 0, t < nt))
    def _():
        prevs = [hd[2]() for hd in heads]
        outs, gens = prepare()
        _round_robin([hd[1](p) for hd, p in zip(heads, prevs)] + gens)
        finish(outs)

    @pl.when(t == nt)
    def _():
        _round_robin([hd[1](hd[2]()) for hd in heads])


def _gdn_setup(t, zq_ref, zk_ref, zv_ref, gz_ref, sm_ref, cwq_ref, cwk_ref, cwv_ref, alog_ref, dtb_ref,
               on_ref, o_ref, s_ref, xs_ref, pw_ref, pu_ref, pattn_ref, *, nsub, **kw):
    @pl.when(t == 0)
    def _():
        s_ref[...] = jnp.zeros_like(s_ref)
        xs_ref[:, :, 0:8, :] = jnp.zeros((nsub, 3, 8, LANE), F32)

    lanes = lambda s: slice(s * LANE, (s + 1) * LANE)
    return [_gdn_head(pl.program_id(1) * nsub + s, t,
                      zq_ref.at[:, lanes(s)], zk_ref.at[:, lanes(s)], zv_ref.at[:, lanes(s)],
                      gz_ref.at[:, lanes(s)], sm_ref, cwq_ref.at[:, lanes(s)], cwk_ref.at[:, lanes(s)],
                      cwv_ref.at[:, lanes(s)], alog_ref, dtb_ref, on_ref, o_ref.at[:, lanes(s)],
                      s_ref.at[s], xs_ref.at[s], pw_ref.at[s], pu_ref.at[s], pattn_ref.at[s], **kw)
            for s in range(nsub)]


def _gdn_head(h, t, zq_ref, zk_ref, zv_ref, gz_ref, sm_ref, cwq_ref, cwk_ref, cwv_ref, alog_ref, dtb_ref,
              on_ref, o_ref, s_ref, xs_ref, pw_ref, pu_ref, pattn_ref, *, tb, a_lane0, b_lane0, q_scale):
    C = CHUNK

    def conv_silu(idx, z_ref, cw_ref):
        xs_ref[idx, 8:8 + tb, :] = z_ref[...]
        cw = cw_ref[...]
        acc = cw[CONV_WIDTH - 1:CONV_WIDTH, :] * xs_ref[idx, 8:8 + tb, :]
        for i in range(1, CONV_WIDTH):
            acc = acc + cw[CONV_WIDTH - 1 - i:CONV_WIDTH - i, :] * xs_ref[idx, 8 - i:8 - i + tb, :]
        xs_ref[idx, 0:8, :] = xs_ref[idx, tb:tb + 8, :]
        return _silu(acc)

    def l2n(x):
        return x * lax.rsqrt(jnp.sum(x * x, axis=-1, keepdims=True) + 1e-6)

    def phase_ab(out):
        q_all = l2n(conv_silu(0, zq_ref, cwq_ref)) * q_scale
        yield
        k_all = l2n(conv_silu(1, zk_ref, cwk_ref))
        yield
        v_all = conv_silu(2, zv_ref, cwv_ref)
        sm = sm_ref[...]
        lane = lax.broadcasted_iota(jnp.int32, (1, LANE), 1)
        g_full = -jnp.exp(alog_ref[...]) * _softplus(sm + dtb_ref[...])
        g_all = jnp.sum(jnp.where(lane == a_lane0 + h, g_full, 0.0), axis=1, keepdims=True)
        beta_all = jnp.sum(jnp.where(lane == b_lane0 + h, _sigmoid(sm), 0.0), axis=1, keepdims=True)
        yield
        r = lax.broadcasted_iota(jnp.int32, (tb, tb), 0)
        c = lax.broadcasted_iota(jnp.int32, (tb, tb), 1)
        same = (r // C) == (c // C)
        causal = jnp.logical_and(same, r >= c)
        strict = jnp.logical_and(same, r > c)
        gb = jnp.broadcast_to(g_all, (tb, LANE))
        sums = _dot_sel(jnp.concatenate([causal, same], axis=0).astype(BF16), gb, 2)
        gc_lanes, g_tot = sums[:tb], sums[tb:]
        yield
        gc_col = jnp.concatenate([gc_lanes] * (tb // LANE), axis=1)
        gc_row = gc_col.T
        decay = jnp.where(causal, jnp.exp(jnp.minimum(gc_col - gc_row, 0.0)), 0.0)
        gc = gc_col[:, 0:1]
        e_gc = jnp.exp(gc)
        kb = k_all * beta_all
        sc = _dot_nt(jnp.concatenate([kb, q_all], axis=0), k_all)
        yield
        L = sc[:tb] * jnp.where(strict, decay, 0.0)
        T = yield from _inv_unit_lower(L, tb, C)
        uw = _dot(T, jnp.concatenate([v_all * beta_all, kb * e_gc], axis=1))
        yield
        out["w"] = uw[:, LANE:].astype(BF16)
        out["q_dec"] = (q_all * e_gc).astype(BF16)
        out["k_dec"] = (k_all * jnp.exp(g_tot[:, 0:1] - gc)).astype(BF16)
        out["u"] = uw[:, :LANE]
        out["e_gl"] = jnp.exp(g_tot[:, :LANE])
        out["gzn"] = on_ref[...] * _silu(gz_ref[...])
        out["attn"] = (sc[tb:] * decay).astype(BF16)

    wslot = lax.rem(t, 2)
    rslot = 1 - wslot

    def store(out):
        pw_ref[wslot, 0], pw_ref[wslot, 1], pw_ref[wslot, 2] = out["w"], out["q_dec"], out["k_dec"]
        pu_ref[wslot, 0], pu_ref[wslot, 1], pu_ref[wslot, 2] = out["u"], out["e_gl"], out["gzn"]
        pattn_ref[wslot] = out["attn"]

    def load():
        return dict(w=pw_ref[rslot, 0], q_dec=pw_ref[rslot, 1], k_dec=pw_ref[rslot, 2],
                    u=pu_ref[rslot, 0], e_gl=pu_ref[rslot, 1], gzn=pu_ref[rslot, 2],
                    attn=pattn_ref[rslot])

    def phase_c(p):
        S = s_ref[...]
        v_news, o_qs = [], []
        for ci in range(tb // C):
            sl = slice(ci * C, (ci + 1) * C)
            ws = _dot(jnp.concatenate([p["w"][sl], p["q_dec"][sl]], axis=0), S)
            yield
            v_new = p["u"][sl] - ws[:C]
            v_news.append(v_new)
            o_qs.append(ws[C:])
            S = S * p["e_gl"][ci * C:ci * C + 1, :] + _dot_tn(p["k_dec"][sl], v_new)
            yield
        s_ref[...] = S
        o = jnp.concatenate(o_qs, axis=0) + _dot(p["attn"], jnp.concatenate(v_news, axis=0))
        yield
        o = o * lax.rsqrt(jnp.mean(o * o, axis=-1, keepdims=True) + RMS_EPS) * p["gzn"]
        o_ref[...] = o.astype(o_ref.dtype)

    return phase_ab, phase_c, load, store


def _gdn_branch(za, zb, conv_w, alog_pad, dtb_pad, onorm_g, *, batch, seq, heads, cb_q, cb_gz, cb_sm,
                a_lane0, b_lane0, tb):
    n = batch * seq
    nt = seq // tb
    nsub = SEQS_PER_STEP if heads % SEQS_PER_STEP == 0 else 1
    wide = nsub * LANE
    row_in = lambda b, t: b * nt + jnp.minimum(t, nt - 1)
    row_out = lambda b, t: b * nt + jnp.maximum(t - 1, 0)
    zspec = lambda cb: pl.BlockSpec((tb, wide), lambda b, h, t: (row_in(b, t), cb // nsub + h))
    cwspec = lambda cb: pl.BlockSpec((CONV_WIDTH, wide), lambda b, h, t: (0, cb // nsub + h))
    vec = pl.BlockSpec((1, LANE), lambda b, h, t: (0, 0))
    return dict(
        setup=functools.partial(_gdn_setup, tb=tb, nsub=nsub, a_lane0=a_lane0, b_lane0=b_lane0,
                                q_scale=float(LANE) ** -0.5),
        grid=(batch, heads // nsub, nt + 1),
        in_specs=[zspec(cb_q), zspec(cb_q + heads), zspec(cb_q + 2 * heads), zspec(cb_gz),
                  pl.BlockSpec((tb, LANE), lambda b, h, t: (row_in(b, t), cb_sm)),
                  cwspec(0), cwspec(heads), cwspec(2 * heads), vec, vec, vec],
        operands=(za, za, za, za, zb, conv_w, conv_w, conv_w, alog_pad, dtb_pad, onorm_g),
        out_spec=pl.BlockSpec((tb, wide), lambda b, h, t: (row_out(b, t), h)),
        out_shape=jax.ShapeDtypeStruct((n, heads * LANE), BF16),
        scratch=[pltpu.VMEM((nsub, LANE, LANE), F32), pltpu.VMEM((nsub, 3, tb + 8, LANE), F32),
                 pltpu.VMEM((nsub, 2, 3, tb, LANE), BF16), pltpu.VMEM((nsub, 2, 3, tb, LANE), F32),
                 pltpu.VMEM((nsub, 2, tb, tb), BF16)])


def _mixers_kernel(*refs, parts, nt):
    t = pl.program_id(2)
    n_in = [len(p["in_specs"]) for p in parts]
    n_sc = [len(p["scratch"]) for p in parts]
    ins, refs = refs[:sum(n_in)], refs[sum(n_in):]
    outs, scr = refs[:len(parts)], refs[len(parts):]
    heads = []
    for k, p in enumerate(parts):
        i0, s0 = sum(n_in[:k]), sum(n_sc[:k])
        heads += p["setup"](t, *ins[i0:i0 + n_in[k]], outs[k], *scr[s0:s0 + n_sc[k]])
    _run_pipelined(t, nt, heads)


def _mixers(parts, nt, name):
    assert all(p["grid"] == parts[0]["grid"] for p in parts)
    meta = [dict(setup=p["setup"], in_specs=p["in_specs"], scratch=p["scratch"]) for p in parts]
    return pl.pallas_call(
        functools.partial(_mixers_kernel, parts=meta, nt=nt),
        grid=parts[0]["grid"],
        in_specs=[s for p in parts for s in p["in_specs"]],
        out_specs=[p["out_spec"] for p in parts],
        out_shape=[p["out_shape"] for p in parts],
        scratch_shapes=[s for p in parts for s in p["scratch"]],
        compiler_params=_cparams(("parallel", "parallel", "arbitrary")),
        name=name,
    )(*[o for p in parts for o in p["operands"]])


_RW_ROWS = ("w0", "a0", "k_k", "k_a", "r_k", "ln_w", "ln_b", "mu_r", "mu_k", "mu_v")


def _rwkv_setup(t, zr_ref, zk_ref, zv_ref, sm0_ref, sm1_ref, sm2_ref, sm3_ref, pv_ref, musm_ref,
                wup_ref, aup_ref, gup_ref, o_ref, h_ref, c_ref, pwx_ref, pbs_ref, puo_ref, phk_ref,
                parb_ref, pvec_ref, *, nsub, **kw):
    @pl.when(t == 0)
    def _():
        h_ref[...] = jnp.zeros_like(h_ref)
        c_ref[...] = jnp.zeros_like(c_ref)

    lanes = lambda s: slice(s * LANE, (s + 1) * LANE)
    return [_rwkv_head_pair(t, zr_ref.at[:, lanes(s)], zk_ref.at[:, lanes(s)], zv_ref.at[:, lanes(s)],
                            sm0_ref, sm1_ref, sm2_ref, sm3_ref, pv_ref.at[:, lanes(s)], musm_ref,
                            wup_ref.at[:, lanes(s)], aup_ref.at[:, lanes(s)], gup_ref.at[:, lanes(s)],
                            o_ref.at[:, lanes(s)], h_ref.at[s], c_ref.at[s], pwx_ref.at[s], pbs_ref.at[s],
                            puo_ref.at[s], phk_ref.at[s], parb_ref.at[s], pvec_ref.at[s], **kw)
            for s in range(nsub)]


def _rwkv_head_pair(t, zr_ref, zk_ref, zv_ref, sm0_ref, sm1_ref, sm2_ref, sm3_ref, pv_ref, musm_ref,
                    wup_ref, aup_ref, gup_ref, o_ref, h_ref, c_ref, pwx_ref, pbs_ref, puo_ref, phk_ref,
                    parb_ref, pvec_ref, *, tb, hd):
    C = CHUNK
    G2, SG = 2 * C, 4 * C
    n_groups = tb // G2

    pv = pv_ref[...]
    prow = lambda name: pv[_RW_ROWS.index(name):_RW_ROWS.index(name) + 1, :]
    rr = lax.broadcasted_iota(jnp.int32, (LANE, LANE), 0)
    cc = lax.broadcasted_iota(jnp.int32, (LANE, LANE), 1)
    bd_ones = ((rr // hd) == (cc // hd)).astype(BF16)
    lane = lax.broadcasted_iota(jnp.int32, (1, LANE), 1)
    m0 = (lane < hd).astype(F32)
    m1 = 1.0 - m0

    def phase_a(a, out):
        musm = musm_ref[...]
        row0 = lax.broadcasted_iota(jnp.int32, (tb, LANE), 0) == 0

        def shift(idx, ref, mu):
            x = ref[...]
            prev = jnp.where(row0, c_ref[idx, 0:1, :], pltpu.roll(x, 1, 0))
            c_ref[idx, 0:1, :] = x[tb - 1:tb, :]
            return x + (prev - x) * mu

        r_all = shift(0, zr_ref, prow("mu_r"))
        k_all = shift(1, zk_ref, prow("mu_k"))
        v_all = shift(2, zv_ref, prow("mu_v"))
        yield
        wd = shift(3, sm0_ref, musm[0:1, 0:LANE])
        ad = shift(4, sm1_ref, musm[0:1, LANE:2 * LANE])
        gd0 = shift(5, sm2_ref, musm[0:1, 2 * LANE:3 * LANE])
        gd1 = shift(6, sm3_ref, musm[0:1, 3 * LANE:4 * LANE])
        w_lin = prow("w0") + _dot(jnp.tanh(wd), wup_ref[...])
        logw = -jnp.exp(-_softplus(-w_lin) - 0.5)
        yield
        alr = _sigmoid(prow("a0") + _dot(ad, aup_ref[...]))
        gate = _dot(_sigmoid(gd0), gup_ref[0:LANE, :]) + _dot(_sigmoid(gd1), gup_ref[LANE:2 * LANE, :])
        yield
        kraw = k_all * prow("k_k")
        kk = kraw * lax.rsqrt(_dot_sel(bd_ones, kraw * kraw, 1, left=False) + 1e-6)
        keff = k_all * (1.0 + (alr - 1.0) * prow("k_a"))
        bonus = _dot_sel(bd_ones, r_all * keff * prow("r_k"), 2, left=False) * v_all
        yield
        rt_i = lax.broadcasted_iota(jnp.int32, (tb, tb), 0)
        ct_i = lax.broadcasted_iota(jnp.int32, (tb, tb), 1)
        same_chunk = (rt_i // C) == (ct_i // C)
        sums = _dot_sel(jnp.concatenate([jnp.logical_and(same_chunk, rt_i >= ct_i),
                                         same_chunk], axis=0).astype(BF16), logw, 2)
        cum, ctot = sums[:tb], sums[tb:]
        yield
        e_neg = jnp.exp(-cum)
        e_end = jnp.exp(ctot - cum)
        b_all = kk * alr
        a.update(v=v_all, rt=r_all * jnp.exp(cum), at=-kk * jnp.exp(cum - logw),
                 bt=b_all * e_neg, kt=keff * e_neg, bh=b_all * e_end, kh=keff * e_end)
        out.update(e_tot=jnp.exp(ctot), bonus=bonus, gate=gate)
        yield [phase_b(a, out, g) for g in range(n_groups)]

    def phase_b(a, out, g):
        r4 = lax.broadcasted_iota(jnp.int32, (SG, SG), 0)
        c4 = lax.broadcasted_iota(jnp.int32, (SG, SG), 1)
        same_unit = (r4 // C) == (c4 // C)
        strict = jnp.logical_and(same_unit, r4 > c4)
        incl = jnp.logical_and(same_unit, r4 >= c4)

        def halves(x):
            return x[g * G2:g * G2 + C], x[g * G2 + C:(g + 1) * G2]

        def stack(x):
            x0, x1 = halves(x)
            return jnp.concatenate([x0 * m0, x0 * m1, x1 * m0, x1 * m1], axis=0)

        def dup(x):
            x0, x1 = halves(x)
            return jnp.concatenate([x0, x0, x1, x1], axis=0)

        xa, xr = stack(a["at"]), stack(a["rt"])
        v_st, b_st, k_st = stack(a["v"]), stack(a["bh"]), stack(a["kh"])
        sc = _dot_nt(jnp.concatenate([xa, xr], axis=0),
                     jnp.concatenate([dup(a["bt"]), dup(a["kt"])], axis=0))
        yield
        sv = _dot(jnp.concatenate([jnp.where(strict, sc[:SG, SG:], 0.0),
                                   jnp.where(incl, sc[SG:, SG:], 0.0)], axis=0), v_st)
        av, ork = sv[:SG], sv[SG:]
        T = yield from _inv_unit_lower(jnp.where(strict, -sc[:SG, :SG], 0.0), SG, hd)
        tw = _dot(T, jnp.concatenate([xa, av], axis=1))
        wt, ut = tw[:, :LANE], tw[:, LANE:]
        yield
        hk = [_dot_tn(v_st[i * G2:(i + 1) * G2], k_st[i * G2:(i + 1) * G2]) for i in range(2)]
        out[g] = dict(wt=wt.astype(BF16), xr=xr.astype(BF16), b_st=b_st.astype(BF16), ut=ut, ork=ork,
                      hk=hk, a_rb=jnp.where(incl, sc[SG:, :SG], 0.0).astype(BF16))

    ws = lax.rem(t, 2)
    rs = 1 - ws

    def store(out):
        for g in range(n_groups):
            o = out[g]
            pwx_ref[ws, g, 0], pwx_ref[ws, g, 1], pbs_ref[ws, g] = o["wt"], o["xr"], o["b_st"]
            puo_ref[ws, g, 0], puo_ref[ws, g, 1] = o["ut"], o["ork"]
            phk_ref[ws, 2 * g], phk_ref[ws, 2 * g + 1] = o["hk"]
            parb_ref[ws, g] = o["a_rb"]
        pvec_ref[ws, 0], pvec_ref[ws, 1], pvec_ref[ws, 2] = out["e_tot"], out["bonus"], out["gate"]

    def load():
        p = {g: dict(wt=pwx_ref[rs, g, 0], xr=pwx_ref[rs, g, 1], b_st=pbs_ref[rs, g],
                     ut=puo_ref[rs, g, 0], ork=puo_ref[rs, g, 1],
                     hk=[phk_ref[rs, 2 * g], phk_ref[rs, 2 * g + 1]], a_rb=parb_ref[rs, g])
             for g in range(n_groups)}
        p.update(e_tot=pvec_ref[rs, 0], bonus=pvec_ref[rs, 1], gate=pvec_ref[rs, 2])
        return p

    def phase_c(p):
        H = h_ref[...]
        us, xrs = [], []
        for ci in range(tb // C):
            gr = p[ci // 2]
            rows = slice((ci % 2) * G2, (ci % 2 + 1) * G2)
            xh = _dot_nt(jnp.concatenate([gr["wt"][rows], gr["xr"][rows]], axis=0), H)
            yield
            u_c = xh[:G2] + gr["ut"][rows]
            H = (H * p["e_tot"][ci * C:ci * C + 1, :] + _dot_tn(u_c, gr["b_st"][rows])
                 + gr["hk"][ci % 2])
            us.append(u_c)
            xrs.append(xh[G2:])
            yield
        h_ref[...] = H
        ys = []
        for g in range(n_groups):
            o_st = (jnp.concatenate(xrs[2 * g:2 * g + 2], axis=0)
                    + _dot(p[g]["a_rb"], jnp.concatenate(us[2 * g:2 * g + 2], axis=0)) + p[g]["ork"])
            ys += [o_st[0:C] + o_st[C:2 * C], o_st[2 * C:3 * C] + o_st[3 * C:4 * C]]
            yield
        y = jnp.concatenate(ys, axis=0)
        mu = _dot_sel(bd_ones, y, 2, left=False) * (1.0 / hd)
        yield
        yc = y - mu
        var = _dot_sel(bd_ones, yc * yc, 1, left=False) * (1.0 / hd)
        yield
        y = yc * lax.rsqrt(var + RWKV_GN_EPS) * prow("ln_w") + prow("ln_b") + p["bonus"]
        o_ref[...] = (y * p["gate"]).astype(o_ref.dtype)

    return (lambda out: phase_a({}, out)), phase_c, load, store


def _rwkv_branch(z, pv, musm, wup, aup, gup, *, batch, seq, width, hd, cb_r, cb_sm, tb):
    n = batch * seq
    nt = seq // tb
    hp = width // LANE
    nsub = SEQS_PER_STEP if hp % SEQS_PER_STEP == 0 else 1
    wide = nsub * LANE
    row_in = lambda b, t: b * nt + jnp.minimum(t, nt - 1)
    row_out = lambda b, t: b * nt + jnp.maximum(t - 1, 0)
    zspec = lambda cb: pl.BlockSpec((tb, wide), lambda b, p, t: (row_in(b, t), cb // nsub + p))
    smspec = lambda i: pl.BlockSpec((tb, LANE), lambda b, p, t: (row_in(b, t), cb_sm + i))
    colblk = lambda rows: pl.BlockSpec((rows, wide), lambda b, p, t: (0, p))
    ng, sg = tb // (2 * CHUNK), 4 * CHUNK
    return dict(
        setup=functools.partial(_rwkv_setup, tb=tb, nsub=nsub, hd=hd),
        grid=(batch, hp // nsub, nt + 1),
        in_specs=[zspec(cb_r), zspec(cb_r + hp), zspec(cb_r + 2 * hp),
                  smspec(0), smspec(1), smspec(2), smspec(3),
                  colblk(pv.shape[0]),
                  pl.BlockSpec(musm.shape, lambda b, p, t: (0, 0)),
                  colblk(LANE), colblk(LANE), colblk(2 * LANE)],
        operands=(z, z, z, z, z, z, z, pv, musm, wup, aup, gup),
        out_spec=pl.BlockSpec((tb, wide), lambda b, p, t: (row_out(b, t), p)),
        out_shape=jax.ShapeDtypeStruct((n, width), BF16),
        scratch=[pltpu.VMEM((nsub, LANE, LANE), F32), pltpu.VMEM((nsub, 7, 8, LANE), F32),
                 pltpu.VMEM((nsub, 2, ng, 2, sg, LANE), BF16),
                 pltpu.VMEM((nsub, 2, ng, sg, LANE), BF16),
                 pltpu.VMEM((nsub, 2, ng, 2, sg, LANE), F32),
                 pltpu.VMEM((nsub, 2, 2 * ng, LANE, LANE), F32),
                 pltpu.VMEM((nsub, 2, ng, sg, sg), BF16), pltpu.VMEM((nsub, 2, 3, tb, LANE), F32)])


class _Layout:
    def __init__(self, d_model, gdn_heads, rwkv_width, decay_lora, aaa_lora, gate_lora, tn):
        self.d_model, self.gh, self.rw = d_model, gdn_heads, rwkv_width
        self.dl, self.al, self.gl = decay_lora, aaa_lora, gate_lora
        self.gw = gdn_heads * LANE
        assert decay_lora + 2 * gdn_heads <= LANE and aaa_lora <= LANE and gate_lora == 2 * LANE
        self.a_lane0 = decay_lora
        self.b_lane0 = decay_lora + gdn_heads
        self.cb_q = 0
        self.cb_gz = 3 * self.gw // LANE
        self.cb_r = 0
        self.cb_sm = 3 * rwkv_width // LANE
        self.off_z = 3 * self.gw
        self.off_a = self.off_z + self.gw
        self.off_b = self.off_a + gdn_heads
        self.off_rwkv = self.off_b + gdn_heads
        self.off_gates = self.off_rwkv + 3 * rwkv_width + decay_lora + aaa_lora + gate_lora

    def split_cols(self, w, axis=-1):
        axis = axis % w.ndim
        cut = lambda a, b: lax.slice_in_dim(w, a, b, axis=axis)
        zeros = lambda n: jnp.zeros(w.shape[:axis] + (n,) + w.shape[axis + 1:], w.dtype)
        rw, dl, al, gl, gh = self.rw, self.dl, self.al, self.gl, self.gh
        o = self.off_rwkv + 3 * rw
        piece_b = jnp.concatenate(
            [cut(self.off_rwkv, o),
             cut(o, o + dl), cut(self.off_a, self.off_b), cut(self.off_b, self.off_rwkv),
             zeros(LANE - dl - 2 * gh),
             cut(o + dl, o + dl + al), zeros(LANE - al),
             cut(o + dl + al, o + dl + al + gl)], axis=axis)
        return cut(0, self.off_a), piece_b, cut(self.off_gates, w.shape[axis])


def _pad_rows(w, rows):
    return jnp.concatenate([w, jnp.zeros((rows - w.shape[0],) + w.shape[1:], w.dtype)], axis=0)


def _gdn_from_packed(za, zb, p, lay, batch, seq, tb=256):
    lane_vec = lambda v, lane0: jnp.zeros((1, LANE), F32).at[0, lane0:lane0 + v.shape[0]].set(v)
    return _gdn_branch(za, zb, p["conv_w"], lane_vec(p["gdn_a_log"], lay.a_lane0),
                       lane_vec(p["gdn_dt_bias"], lay.a_lane0), p["gdn_onorm_g"].reshape(1, LANE),
                       batch=batch, seq=seq, heads=lay.gh, cb_q=lay.cb_q, cb_gz=lay.cb_gz,
                       cb_sm=lay.cb_sm, a_lane0=lay.a_lane0, b_lane0=lay.b_lane0, tb=min(tb, seq))


def _rwkv_from_packed(z, p, lay, batch, seq, tb=256):
    rw = lay.rw
    hd = p["rwkv_r_k"].shape[-1]
    assert LANE % hd == 0 and hd * 2 == LANE
    mu = p["rwkv_mu"]
    rows = {"w0": p["rwkv_w0"], "a0": p["rwkv_a0"], "k_k": p["rwkv_k_k"], "k_a": p["rwkv_k_a"],
            "r_k": p["rwkv_r_k"].reshape(rw), "ln_w": p["rwkv_ln_w"], "ln_b": p["rwkv_ln_b"],
            "mu_r": mu[:rw], "mu_k": mu[rw:2 * rw], "mu_v": mu[2 * rw:3 * rw]}
    pv = _pad_rows(jnp.stack([rows[k] for k in _RW_ROWS]).astype(F32), 16)
    o = 3 * rw
    zl = lambda n: jnp.zeros((n,), F32)
    musm = jnp.concatenate([mu[o:o + lay.dl], zl(LANE - lay.dl), mu[o + lay.dl:o + lay.dl + lay.al],
                            zl(LANE - lay.al), mu[o + lay.dl + lay.al:]])
    musm = _pad_rows(musm.reshape(1, 4 * LANE), 8)
    wup = _pad_rows(p["rwkv_w_up"], LANE).astype(BF16)
    aup = _pad_rows(p["rwkv_a_up"], LANE).astype(BF16)
    gup = p["rwkv_g_up"].astype(BF16)
    return _rwkv_branch(z, pv, musm, wup, aup, gup, batch=batch, seq=seq, width=rw, hd=hd,
                        cb_r=lay.cb_r, cb_sm=lay.cb_sm, tb=min(tb, seq))


def _make_layout(p, tn):
    return _Layout(p["norm1_g"].shape[-1], p["gdn_a_log"].shape[-1], p["rwkv_w0"].shape[-1],
                   p["rwkv_w_up"].shape[0], p["rwkv_a_up"].shape[0], p["rwkv_g_up"].shape[0], tn)


def _ada_kernel(c_ref, w_ref, b_ref, o_ref):
    o_ref[...] = _dot(_silu(c_ref[...]), w_ref[...]) + b_ref[...]


def _ada_mod(c_pad, w_ada, b_ada, tn):
    d, cols = w_ada.shape
    return pl.pallas_call(
        _ada_kernel,
        grid=(cols // tn,),
        in_specs=[pl.BlockSpec((8, d), lambda j: (0, 0)),
                  pl.BlockSpec((d, tn), lambda j: (0, j)),
                  pl.BlockSpec((1, tn), lambda j: (0, j))],
        out_specs=pl.BlockSpec((8, tn), lambda j: (0, j)),
        out_shape=jax.ShapeDtypeStruct((8, cols), F32),
        compiler_params=_cparams(("arbitrary",)),
        name="ada_mod",
    )(c_pad, w_ada, b_ada.reshape(1, cols))


def _modnorm(x, g, sc, sh):
    y = x * lax.rsqrt(jnp.mean(x * x, axis=-1, keepdims=True) + RMS_EPS)
    return y * g * (1.0 + sc) + sh


def _norm1_kernel(x_ref, g_ref, sh_ref, sc_ref, o_ref):
    b = pl.program_id(0)
    h = _modnorm(x_ref[...], g_ref[...], sc_ref[pl.ds(b, 1), :], sh_ref[pl.ds(b, 1), :])
    o_ref[...] = h.astype(o_ref.dtype)


def _norm1(x2, g, mod, *, batch, seq, tm):
    n, d = x2.shape
    nt = seq // tm
    return pl.pallas_call(
        _norm1_kernel,
        grid=(batch, nt),
        in_specs=[pl.BlockSpec((tm, d), lambda b, t: (b * nt + t, 0)),
                  pl.BlockSpec((1, d), lambda b, t: (0, 0)),
                  pl.BlockSpec((8, d), lambda b, t: (0, 0)),
                  pl.BlockSpec((8, d), lambda b, t: (0, 1))],
        out_specs=pl.BlockSpec((tm, d), lambda b, t: (b * nt + t, 0)),
        out_shape=jax.ShapeDtypeStruct((n, d), BF16),
        compiler_params=_cparams(("parallel", "arbitrary")),
        name="norm1",
    )(x2, g, mod, mod)


def _merge_kernel(o_ref, y_ref, wa_ref, wb_ref, ga_ref, gb_ref, m_ref):
    ha = jnp.dot(o_ref[...], wa_ref[...], preferred_element_type=F32)
    hb = jnp.dot(y_ref[...], wb_ref[...], preferred_element_type=F32)
    m_ref[...] = (ga_ref[...].astype(F32) * ha + gb_ref[...].astype(F32) * hb).astype(m_ref.dtype)


def _merge(o, y, wa, wb, gates, *, tm, tn):
    n, d = o.shape[0], wa.shape[1]
    tn = math.gcd(tn, d)
    gb0 = 0
    return pl.pallas_call(
        _merge_kernel,
        grid=(n // tm, d // tn),
        in_specs=[pl.BlockSpec((tm, o.shape[1]), lambda i, j: (i, 0)),
                  pl.BlockSpec((tm, y.shape[1]), lambda i, j: (i, 0)),
                  pl.BlockSpec((wa.shape[0], tn), lambda i, j: (0, j)),
                  pl.BlockSpec((wb.shape[0], tn), lambda i, j: (0, j)),
                  pl.BlockSpec((tm, tn), lambda i, j: (i, gb0 + j)),
                  pl.BlockSpec((tm, tn), lambda i, j: (i, gb0 + d // tn + j))],
        out_specs=pl.BlockSpec((tm, tn), lambda i, j: (i, j)),
        out_shape=jax.ShapeDtypeStruct((n, d), BF16),
        compiler_params=_cparams(("parallel", "arbitrary")),
        name="merge",
    )(o, y, wa, wb, gates, gates)


def _outproj_kernel(m_ref, w_ref, x_ref, g_ref, o_ref, *, tiles_per_batch):
    b = pl.program_id(0) // tiles_per_batch
    y = jnp.dot(m_ref[...], w_ref[...], preferred_element_type=F32)
    o_ref[...] = x_ref[...] + g_ref[pl.ds(b, 1), :] * y


def _outproj(m, w, x2, mod, *, seq, tm, tn):
    n, d = x2.shape
    gcol = 2 * d // tn
    return pl.pallas_call(
        functools.partial(_outproj_kernel, tiles_per_batch=seq // tm),
        grid=(n // tm, d // tn),
        in_specs=[pl.BlockSpec((tm, d), lambda i, j: (i, 0)),
                  pl.BlockSpec((d, tn), lambda i, j: (0, j)),
                  pl.BlockSpec((tm, tn), lambda i, j: (i, j)),
                  pl.BlockSpec((8, tn), lambda i, j: (0, gcol + j))],
        out_specs=pl.BlockSpec((tm, tn), lambda i, j: (i, j)),
        out_shape=jax.ShapeDtypeStruct((n, d), F32),
        compiler_params=_cparams(("parallel", "arbitrary")),
        name="outproj",
    )(m, w, x2, mod)


def _router_kernel(x_ref, g_ref, sh_ref, sc_ref, wr_ref, br_ref, h_ref, id_ref, wt_ref, cnt_out_ref,
                   cnt_ref, *, n_groups, per_group):
    b = pl.program_id(0)
    h = _modnorm(x_ref[...], g_ref[...], sc_ref[pl.ds(b, 1), :], sh_ref[pl.ds(b, 1), :])
    h_ref[...] = _pack_bf16_pairs(h)
    logits = _dot(h, wr_ref[...]) + br_ref[...]
    tm = logits.shape[0]
    lane = lax.broadcasted_iota(jnp.int32, (tm, LANE), 1)
    neg = jnp.float32(-jnp.inf)
    big = jnp.int32(LANE)

    def first_argmax(vals, mask):
        vm = jnp.where(mask, vals, neg)
        mx = jnp.max(vm, axis=1, keepdims=True)
        idx = jnp.min(jnp.where(jnp.logical_and(mask, vm == mx), lane, big), axis=1, keepdims=True)
        return mx, idx

    gmask = lane < n_groups
    gmax, g_sel = first_argmax(logits, gmask)
    g_prob = 1.0 / jnp.sum(jnp.where(gmask, jnp.exp(logits - gmax), 0.0), axis=1, keepdims=True)
    e_lane = lane - n_groups
    emask = jnp.logical_and(e_lane >= g_sel * per_group, e_lane < (g_sel + 1) * per_group)
    emax = jnp.max(jnp.where(emask, logits, neg), axis=1, keepdims=True)
    ex = jnp.where(emask, jnp.exp(logits - emax), 0.0)
    probs = ex / jnp.sum(ex, axis=1, keepdims=True)
    p0, i0 = first_argmax(probs, emask)
    p1, i1 = first_argmax(probs, jnp.logical_and(emask, lane != i0))
    den = p0 + p1
    wt_ref[...] = jnp.where(lane == 0, g_prob * (p0 / den), jnp.where(lane == 1, g_prob * (p1 / den), 0.0))

    @pl.when(jnp.logical_and(b == 0, pl.program_id(1) == 0))
    def _():
        cnt_ref[...] = jnp.zeros_like(cnt_ref)

    e0, e1 = i0 - n_groups, i1 - n_groups
    hit0, hit1 = lane == e0, lane == e1
    hits = jnp.logical_or(hit0, hit1).astype(BF16)
    rr = lax.broadcasted_iota(jnp.int32, (tm, tm), 0)
    cc = lax.broadcasted_iota(jnp.int32, (tm, tm), 1)
    before = jnp.dot((rr > cc).astype(BF16), hits, preferred_element_type=F32) + cnt_ref[0:1, :]
    rank0 = jnp.sum(jnp.where(hit0, before, 0.0), axis=1, keepdims=True).astype(jnp.int32)
    rank1 = jnp.sum(jnp.where(hit1, before, 0.0), axis=1, keepdims=True).astype(jnp.int32)
    cnt_ref[0:1, :] = cnt_ref[0:1, :] + jnp.sum(hits.astype(F32), axis=0, keepdims=True)
    id_ref[...] = jnp.where(lane == 0, e0, jnp.where(lane == 1, e1,
                            jnp.where(lane == 2, rank0, jnp.where(lane == 3, rank1, 0))))
    cnt_out_ref[...] = cnt_ref[...].astype(jnp.int32)


def _router(x1, g, mod, wr, br, *, batch, seq, tm, n_groups, per_group):
    n, d = x1.shape
    nt = seq // tm
    rowblk = lambda w: pl.BlockSpec((tm, w), lambda b, t: (b * nt + t, 0))
    return pl.pallas_call(
        functools.partial(_router_kernel, n_groups=n_groups, per_group=per_group),
        grid=(batch, nt),
        in_specs=[rowblk(d),
                  pl.BlockSpec((1, d), lambda b, t: (0, 0)),
                  pl.BlockSpec((8, d), lambda b, t: (0, 3)),
                  pl.BlockSpec((8, d), lambda b, t: (0, 4)),
                  pl.BlockSpec((d, LANE), lambda b, t: (0, 0)),
                  pl.BlockSpec((1, LANE), lambda b, t: (0, 0))],
        out_specs=[rowblk(d // 2), rowblk(LANE), rowblk(LANE), pl.BlockSpec((8, LANE), lambda b, t: (0, 0))],
        out_shape=[jax.ShapeDtypeStruct((n, d // 2), jnp.uint32), jax.ShapeDtypeStruct((n, LANE), jnp.int32),
                   jax.ShapeDtypeStruct((n, LANE), F32), jax.ShapeDtypeStruct((8, LANE), jnp.int32)],
        scratch_shapes=[pltpu.VMEM((8, LANE), F32)],
        compiler_params=_cparams(("arbitrary", "arbitrary")),
        name="router",
    )(x1, g, mod, mod, wr, br)


def _expert_kernel(be_ref, tok_ref, nused_ref, h_hbm, w1_ref, w3_ref, w2_ref, bw_ref, o_ref, xbuf, sem):
    i = pl.program_id(0)
    nb = nused_ref[0]

    def row_copy(blk, slot, r):
        tok = tok_ref[blk * MOE_BLOCK + r]
        return pltpu.make_async_copy(h_hbm.at[pl.ds(tok, 1), :], xbuf.at[slot, pl.ds(r, 1), :],
                                     sem.at[slot])

    def start_block(blk, slot):
        def body(r, carry):
            row_copy(blk, slot, r).start()
            return carry
        lax.fori_loop(0, MOE_BLOCK, body, 0, unroll=DMA_UNROLL)

    for ahead in range(GATHER_DEPTH - 1):
        @pl.when(jnp.logical_and(i == 0, ahead < nb))
        def _():
            start_block(ahead, ahead)

    @pl.when(i + GATHER_DEPTH - 1 < nb)
    def _():
        start_block(i + GATHER_DEPTH - 1, (i + GATHER_DEPTH - 1) % GATHER_DEPTH)

    slot = i % GATHER_DEPTH

    @pl.when(i < nb)
    def _():
        def wait_body(r, carry):
            row_copy(i, slot, r).wait()
            return carry
        lax.fori_loop(0, MOE_BLOCK, wait_body, 0, unroll=DMA_UNROLL)

        x_lo, x_hi = _unpack_bf16_pairs(xbuf[slot])
        half = x_lo.shape[1]
        a = _dot(x_lo, w1_ref[0:half, :]) + _dot(x_hi, w1_ref[half:, :])
        g = _dot(x_lo, w3_ref[0:half, :]) + _dot(x_hi, w3_ref[half:, :])
        y = _dot(_silu(a) * g, w2_ref[...])
        o_ref[...] = _pack_bf16_pairs(y * bw_ref[...])

    @pl.when(i >= nb)
    def _():
        o_ref[...] = jnp.zeros_like(o_ref)


def _experts(h2, w1, w3, w2, block_expert, buf_tok, n_used, buf_w):
    half = h2.shape[1]
    d = 2 * half
    de = w1.shape[2]
    p_rows = buf_tok.shape[0]
    wspec = lambda shape: pl.BlockSpec((None,) + shape, lambda i, be, tok, nu: (be[i], 0, 0))
    grid_spec = pltpu.PrefetchScalarGridSpec(
        num_scalar_prefetch=3,
        grid=(p_rows // MOE_BLOCK,),
        in_specs=[pl.BlockSpec(memory_space=pl.ANY),
                  wspec((d, de)), wspec((d, de)), wspec((de, d)),
                  pl.BlockSpec((MOE_BLOCK, 1), lambda i, be, tok, nu: (i, 0))],
        out_specs=pl.BlockSpec((MOE_BLOCK, half), lambda i, be, tok, nu: (i, 0)),
        scratch_shapes=[pltpu.VMEM((GATHER_DEPTH, MOE_BLOCK, half), jnp.uint32),
                        pltpu.SemaphoreType.DMA((GATHER_DEPTH,))],
    )
    return pl.pallas_call(
        _expert_kernel,
        grid_spec=grid_spec,
        out_shape=jax.ShapeDtypeStruct((p_rows, half), jnp.uint32),
        compiler_params=_cparams(("arbitrary",)),
        name="experts",
    )(block_expert, buf_tok, n_used, h2, w1, w3, w2, buf_w)


def _combine_kernel(pos_ref, x_ref, g_ref, gf_ref, y_hbm, o_ref, ybuf, sem, *, tm, tiles_per_batch):
    i = pl.program_id(0)
    nb = pl.num_programs(0)
    rows = TOP_K * tm

    def row_copy(blk, slot, r):
        return pltpu.make_async_copy(y_hbm.at[pl.ds(pos_ref[blk * rows + r], 1), :],
                                     ybuf.at[slot, pl.ds(r, 1), :], sem.at[slot])

    def start_block(blk, slot):
        def body(r, carry):
            row_copy(blk, slot, r).start()
            return carry
        lax.fori_loop(0, rows, body, 0, unroll=DMA_UNROLL)

    @pl.when(i == 0)
    def _():
        start_block(0, 0)

    @pl.when(i + 1 < nb)
    def _():
        start_block(i + 1, (i + 1) % 2)

    slot = i % 2

    def wait_body(r, carry):
        row_copy(i, slot, r).wait()
        return carry
    lax.fori_loop(0, rows, wait_body, 0, unroll=DMA_UNROLL)

    b = i // tiles_per_batch
    lo0, hi0 = _unpack_bf16_pairs(ybuf[slot, 0:tm, :])
    lo1, hi1 = _unpack_bf16_pairs(ybuf[slot, tm:rows, :])
    moe = jnp.concatenate([lo0 + lo1, hi0 + hi1], axis=1)
    x = x_ref[...] + g_ref[pl.ds(b, 1), :] * moe
    o_ref[...] = x * lax.rsqrt(jnp.mean(x * x, axis=-1, keepdims=True) + RMS_EPS) * gf_ref[...]


def _combine(pos, x1, mod, gf, yb, *, seq, tm):
    n, d = x1.shape
    grid_spec = pltpu.PrefetchScalarGridSpec(
        num_scalar_prefetch=1,
        grid=(n // tm,),
        in_specs=[pl.BlockSpec((tm, d), lambda i, pos: (i, 0)),
                  pl.BlockSpec((8, d), lambda i, pos: (0, 5)),
                  pl.BlockSpec((1, d), lambda i, pos: (0, 0)),
                  pl.BlockSpec(memory_space=pl.ANY)],
        out_specs=pl.BlockSpec((tm, d), lambda i, pos: (i, 0)),
        scratch_shapes=[pltpu.VMEM((2, TOP_K * tm, d // 2), jnp.uint32), pltpu.SemaphoreType.DMA((2,))],
    )
    return pl.pallas_call(
        functools.partial(_combine_kernel, tm=tm, tiles_per_batch=seq // tm),
        grid_spec=grid_spec,
        out_shape=jax.ShapeDtypeStruct((n, d), F32),
        compiler_params=_cparams(("arbitrary",)),
        name="combine",
    )(pos, x1, mod, gf, yb)


def _dispatch(ids, rank, wts, counts):
    n = ids.shape[0]
    n_experts = counts.shape[0]
    a = n * TOP_K
    expert_id = ids.reshape(a)
    w_flat = wts.reshape(a)
    order = jnp.argsort(expert_id).astype(jnp.int32)
    ei = jnp.arange(n_experts, dtype=jnp.int32)
    upto = ei[None, :] <= ei[:, None]
    padded = (counts + MOE_BLOCK - 1) // MOE_BLOCK * MOE_BLOCK
    c_start = jnp.sum(jnp.where(upto, counts[None, :], 0), axis=1) - counts
    p_end = jnp.sum(jnp.where(upto, padded[None, :], 0), axis=1)
    p_start = p_end - padded
    pos = rank.reshape(a) + p_start[expert_id]
    p_rows = (a + n_experts * (MOE_BLOCK - 1) + MOE_BLOCK - 1) // MOE_BLOCK * MOE_BLOCK
    n_blocks = p_rows // MOE_BLOCK
    starts = jnp.arange(n_blocks, dtype=jnp.int32) * MOE_BLOCK
    block_expert = jnp.minimum(jnp.sum((p_end[None, :] <= starts[:, None]).astype(jnp.int32), axis=1),
                               n_experts - 1)
    slot = jnp.arange(p_rows, dtype=jnp.int32)
    per_slot = lambda table: jnp.repeat(table[block_expert], MOE_BLOCK)
    within = slot - per_slot(p_start)
    valid = within < per_slot(counts)
    src = order[jnp.clip(within + per_slot(c_start), 0, a - 1)]
    buf_tok = jnp.where(valid, src // TOP_K, 0)
    buf_w = jnp.where(valid, w_flat[src], 0.0)
    n_used = (p_end[n_experts - 1:] // MOE_BLOCK).astype(jnp.int32)
    return pos, buf_tok, buf_w, block_expert, n_used


def _pick(n, pref):
    t = min(n, pref)
    while n % t:
        t //= 2
    return t


def kernel(x, c, w_ada, b_ada, norm1_g, w_in, conv_w, gdn_a_log, gdn_dt_bias, gdn_onorm_g, rwkv_mu,
           rwkv_w0, rwkv_w_up, rwkv_a0, rwkv_a_up, rwkv_g_up, rwkv_k_k, rwkv_k_a, rwkv_r_k, rwkv_ln_w,
           rwkv_ln_b, w_gdn_o, w_rwkv_o, w_out, norm2_g, w_group, b_group, w_expert, b_expert, w1, w3,
           w2, norm_f_g):
    batch, seq, d = x.shape
    n = batch * seq
    depth = w_ada.shape[0]
    assert depth == 1, "the final norm is fused into the last layer's combine"
    assert batch <= 8 and seq % CHUNK == 0 and d % 512 == 0
    x2 = x.reshape(n, d)
    c_pad = _pad_rows(c, 8)
    tm_big = _pick(seq, 1024)
    tm_row = _pick(seq, 256)
    tn = 512
    for l in range(depth):
        p = {"norm1_g": norm1_g[l], "conv_w": conv_w[l], "gdn_a_log": gdn_a_log[l],
             "gdn_dt_bias": gdn_dt_bias[l], "gdn_onorm_g": gdn_onorm_g[l], "rwkv_mu": rwkv_mu[l],
             "rwkv_w0": rwkv_w0[l], "rwkv_w_up": rwkv_w_up[l], "rwkv_a0": rwkv_a0[l],
             "rwkv_a_up": rwkv_a_up[l], "rwkv_g_up": rwkv_g_up[l], "rwkv_k_k": rwkv_k_k[l],
             "rwkv_k_a": rwkv_k_a[l], "rwkv_r_k": rwkv_r_k[l], "rwkv_ln_w": rwkv_ln_w[l],
             "rwkv_ln_b": rwkv_ln_b[l]}
        lay = _make_layout(p, tn)
        mod = _ada_mod(c_pad, w_ada[l], b_ada[l], tn)

        h1 = _norm1(x2, norm1_g[l].reshape(1, d), mod, batch=batch, seq=seq, tm=tm_row)
        w_a, w_b, w_c = (w.astype(BF16) for w in lay.split_cols(w_in[l].T, axis=0))
        n_exp, _, d_exp = w1.shape[1:]
        flat = lambda w: w.reshape(-1, w.shape[-1])
        za, w1b, w2b = _matmul(h1, w_a, tm_big, tn, F32, name="in_proj_gdn",
                               cast=[flat(w1[l]), flat(w2[l])], w_rows_are_outputs=True)
        zb = _matmul(h1, w_b, tm_big, tn, F32, name="in_proj_rwkv", w_rows_are_outputs=True)
        gates, w3b = _matmul(h1, w_c, tm_big, tn, BF16, act=_sigmoid, name="in_proj_gates",
                             cast=[flat(w3[l])], w_rows_are_outputs=True)
        w1b, w3b = w1b.reshape(n_exp, d, d_exp), w3b.reshape(n_exp, d, d_exp)
        w2b = w2b.reshape(n_exp, d_exp, d)
        tb = min(256, seq)
        o_gdn, y_rwkv = _mixers([_gdn_from_packed(za, zb, p, lay, batch, seq, tb),
                                 _rwkv_from_packed(zb, p, lay, batch, seq, tb)], seq // tb, "mixers")
        m = _merge(o_gdn, y_rwkv, w_gdn_o[l].astype(BF16), w_rwkv_o[l].astype(BF16), gates,
                   tm=tm_big, tn=tn)
        x1 = _outproj(m, w_out[l].astype(BF16), x2, mod, seq=seq, tm=tm_big, tn=tn)

        n_groups, n_experts = w_group.shape[-1], w_expert.shape[-1]
        assert n_groups + n_experts <= LANE
        wr = jnp.concatenate([w_group[l], w_expert[l],
                              jnp.zeros((d, LANE - n_groups - n_experts), F32)], axis=1).astype(BF16)
        br = jnp.concatenate([b_group[l], b_expert[l],
                              jnp.zeros((LANE - n_groups - n_experts,), F32)]).reshape(1, LANE)
        h2, ids, wts, cnt = _router(x1, norm2_g[l].reshape(1, d), mod, wr, br, batch=batch, seq=seq,
                                    tm=tm_row, n_groups=n_groups, per_group=n_experts // n_groups)
        pos, buf_tok, buf_w, block_expert, n_used = _dispatch(
            ids[:, :TOP_K], ids[:, TOP_K:2 * TOP_K], wts[:, :TOP_K], cnt[0, :n_experts])
        yb = _experts(h2, w1b, w3b, w2b, block_expert, buf_tok, n_used, buf_w.reshape(-1, 1))
        tm_c = _pick(seq, 128)
        pos_tiles = pos.reshape(n // tm_c, tm_c, TOP_K).transpose(0, 2, 1).reshape(n * TOP_K)
        x2 = _combine(pos_tiles, x1, mod, norm_f_g.reshape(1, d), yb, seq=seq, tm=tm_c)
    return x2.reshape(batch, seq, d)
```

```python
import functools
import math

import jax
import jax.numpy as jnp
from jax import lax
from jax.experimental import pallas as pl
from jax.experimental.pallas import tpu as pltpu

F32 = jnp.float32
BF16 = jnp.bfloat16

LANE = 128
CHUNK = 64
CONV_WIDTH = 4
RMS_EPS = 1e-6
RWKV_GN_EPS = 64e-5
MOE_BLOCK = 128
TOP_K = 2
DMA_UNROLL = 8
GATHER_DEPTH = 3
SEQS_PER_STEP = 4
VMEM_LIMIT = 56 * 1024 * 1024


def _cparams(sem):
    return pltpu.CompilerParams(dimension_semantics=sem, vmem_limit_bytes=VMEM_LIMIT)


def _dot(a, b):
    return jnp.dot(a.astype(BF16), b.astype(BF16), preferred_element_type=F32)


def _dot_nt(a, b):
    return lax.dot_general(a.astype(BF16), b.astype(BF16), (((1,), (1,)), ((), ())),
                           preferred_element_type=F32)


def _dot_tn(a, b):
    return lax.dot_general(a.astype(BF16), b.astype(BF16), (((0,), (0,)), ((), ())),
                           preferred_element_type=F32)


def _pack_bf16_pairs(x):
    w = x.shape[1] // 2
    bits = lax.bitcast_convert_type(x.astype(BF16).astype(F32), jnp.uint32)
    return (bits[:, w:] & jnp.uint32(0xFFFF0000)) | (bits[:, :w] >> 16)


def _unpack_bf16_pairs(words):
    lo = lax.bitcast_convert_type(words << 16, F32)
    hi = lax.bitcast_convert_type(words & jnp.uint32(0xFFFF0000), F32)
    return lo, hi


def _split(x, terms):
    pieces = []
    for _ in range(terms - 1):
        hi = x.astype(BF16)
        pieces.append(hi)
        x = x - hi.astype(F32)
    pieces.append(x.astype(BF16))
    return pieces


def _dot_sel(sel, x, terms, left=True):
    out = None
    for piece in _split(x, terms):
        ops = (sel, piece) if left else (piece, sel)
        d = jnp.dot(*ops, preferred_element_type=F32)
        out = d if out is None else out + d
    return out


def _sigmoid(x):
    return 1.0 / (1.0 + jnp.exp(-x))


def _silu(x):
    return x * _sigmoid(x)


def _softplus(x):
    return jnp.maximum(x, 0.0) + jnp.log(1.0 + jnp.exp(-jnp.abs(x)))


def _inv_unit_lower(L, n, top):
    r = lax.broadcasted_iota(jnp.int32, (n, n), 0)
    c = lax.broadcasted_iota(jnp.int32, (n, n), 1)
    eye = (r == c).astype(F32)
    same16 = (r // 16) == (c // 16)
    Ld = jnp.where(same16, L, 0.0)
    X = eye - Ld
    P = _dot(Ld, Ld)
    yield
    for _ in range(2):
        X = X + _dot(X, P)
        P = _dot(P, P)
        yield
    X = X + _dot(X, P)
    yield
    bs = 32
    while bs <= top:
        inner = (r // (bs // 2)) == (c // (bs // 2))
        outer = (r // bs) == (c // bs)
        Lo = jnp.where(jnp.logical_and(outer, jnp.logical_not(inner)), L, 0.0)
        Y = _dot(Lo, X)
        yield
        X = X - _dot(X, Y)
        yield
        bs *= 2
    return X


def _round_robin(gens):
    gens = list(gens)
    while gens:
        for g in list(gens):
            try:
                more = next(g)
            except StopIteration:
                gens.remove(g)
                continue
            if more:
                gens.extend(more)


def _mm_kernel(a_ref, w_ref, *refs, act, n_cast, w_rows_are_outputs):
    cast_in, o_ref, cast_out = refs[:n_cast], refs[n_cast], refs[n_cast + 1:]
    contract_w = 1 if w_rows_are_outputs else 0
    y = lax.dot_general(a_ref[...], w_ref[...], (((1,), (contract_w,)), ((), ())),
                        preferred_element_type=F32)
    o_ref[...] = (act(y) if act else y).astype(o_ref.dtype)
    for src, dst in zip(cast_in, cast_out):
        dst[...] = src[...].astype(dst.dtype)


def _matmul(a, w, tm, tn, out_dtype, act=None, name="matmul", cast=(), w_rows_are_outputs=False):
    M, K = a.shape
    Nn = w.shape[0] if w_rows_are_outputs else w.shape[1]
    tn = math.gcd(tn, Nn)
    nj = Nn // tn
    steps = (M // tm) * nj
    cast = [c for c in cast]
    assert all(c.shape[0] % (16 * steps) == 0 for c in cast)
    slab = lambda c: pl.BlockSpec((c.shape[0] // steps, c.shape[1]), lambda i, j: (i * nj + j, 0))
    w_spec = (pl.BlockSpec((tn, K), lambda i, j: (j, 0)) if w_rows_are_outputs
              else pl.BlockSpec((K, tn), lambda i, j: (0, j)))
    outs = pl.pallas_call(
        functools.partial(_mm_kernel, act=act, n_cast=len(cast), w_rows_are_outputs=w_rows_are_outputs),
        grid=(M // tm, nj),
        in_specs=[pl.BlockSpec((tm, K), lambda i, j: (i, 0)), w_spec] + [slab(c) for c in cast],
        out_specs=[pl.BlockSpec((tm, tn), lambda i, j: (i, j))] + [slab(c) for c in cast],
        out_shape=[jax.ShapeDtypeStruct((M, Nn), out_dtype)]
                  + [jax.ShapeDtypeStruct(c.shape, BF16) for c in cast],
        compiler_params=_cparams(("parallel", "arbitrary")),
        name=name,
    )(a, w, *cast)
    return outs if cast else outs[0]


def _run_pipelined(t, nt, heads):
    def prepare():
        outs = [{} for _ in heads]
        return outs, [hd[0](out) for hd, out in zip(heads, outs)]

    def finish(outs):
        for hd, out in zip(heads, outs):
            hd[3](out)

    @pl.when(t == 0)
    def _():
        outs, gens = prepare()
        _round_robin(gens)
        finish(outs)

    @pl.when(jnp.logical_and(t > 0, t < nt))
    def _():
        prevs = [hd[2]() for hd in heads]
        outs, gens = prepare()
        _round_robin([hd[1](p) for hd, p in zip(heads, prevs)] + gens)
        finish(outs)

    @pl.when(t == nt)
    def _():
        _round_robin([hd[1](hd[2]()) for hd in heads])


def _gdn_setup(t, zq_ref, zk_ref, zv_ref, gz_ref, sm_ref, cwq_ref, cwk_ref, cwv_ref, alog_ref, dtb_ref,
               on_ref, o_ref, s_ref, xs_ref, pw_ref, pu_ref, pattn_ref, *, nsub, **kw):
    @pl.when(t == 0)
    def _():
        s_ref[...] = jnp.zeros_like(s_ref)
        xs_ref[:, :, 0:8, :] = jnp.zeros((nsub, 3, 8, LANE), F32)

    lanes = lambda s: slice(s * LANE, (s + 1) * LANE)
    return [_gdn_head(pl.program_id(1) * nsub + s, t,
                      zq_ref.at[:, lanes(s)], zk_ref.at[:, lanes(s)], zv_ref.at[:, lanes(s)],
                      gz_ref.at[:, lanes(s)], sm_ref, cwq_ref.at[:, lanes(s)], cwk_ref.at[:, lanes(s)],
                      cwv_ref.at[:, lanes(s)], alog_ref, dtb_ref, on_ref, o_ref.at[:, lanes(s)],
                      s_ref.at[s], xs_ref.at[s], pw_ref.at[s], pu_ref.at[s], pattn_ref.at[s], **kw)
            for s in range(nsub)]


def _gdn_head(h, t, zq_ref, zk_ref, zv_ref, gz_ref, sm_ref, cwq_ref, cwk_ref, cwv_ref, alog_ref, dtb_ref,
              on_ref, o_ref, s_ref, xs_ref, pw_ref, pu_ref, pattn_ref, *, tb, a_lane0, b_lane0, q_scale):
    C = CHUNK

    def conv_silu(idx, z_ref, cw_ref):
        xs_ref[idx, 8:8 + tb, :] = z_ref[...]
        cw = cw_ref[...]
        acc = cw[CONV_WIDTH - 1:CONV_WIDTH, :] * xs_ref[idx, 8:8 + tb, :]
        for i in range(1, CONV_WIDTH):
            acc = acc + cw[CONV_WIDTH - 1 - i:CONV_WIDTH - i, :] * xs_ref[idx, 8 - i:8 - i + tb, :]
        xs_ref[idx, 0:8, :] = xs_ref[idx, tb:tb + 8, :]
        return _silu(acc)

    def l2n(x):
        return x * lax.rsqrt(jnp.sum(x * x, axis=-1, keepdims=True) + 1e-6)

    def phase_ab(out):
        q_all = l2n(conv_silu(0, zq_ref, cwq_ref)) * q_scale
        yield
        k_all = l2n(conv_silu(1, zk_ref, cwk_ref))
        yield
        v_all = conv_silu(2, zv_ref, cwv_ref)
        sm = sm_ref[...]
        lane = lax.broadcasted_iota(jnp.int32, (1, LANE), 1)
        g_full = -jnp.exp(alog_ref[...]) * _softplus(sm + dtb_ref[...])
        g_all = jnp.sum(jnp.where(lane == a_lane0 + h, g_full, 0.0), axis=1, keepdims=True)
        beta_all = jnp.sum(jnp.where(lane == b_lane0 + h, _sigmoid(sm), 0.0), axis=1, keepdims=True)
        yield
        r = lax.broadcasted_iota(jnp.int32, (tb, tb), 0)
        c = lax.broadcasted_iota(jnp.int32, (tb, tb), 1)
        same = (r // C) == (c // C)
        causal = jnp.logical_and(same, r >= c)
        strict = jnp.logical_and(same, r > c)
        gb = jnp.broadcast_to(g_all, (tb, LANE))
        sums = _dot_sel(jnp.concatenate([causal, same], axis=0).astype(BF16), gb, 2)
        gc_lanes, g_tot = sums[:tb], sums[tb:]
        yield
        gc_col = jnp.concatenate([gc_lanes] * (tb // LANE), axis=1)
        gc_row = gc_col.T
        decay = jnp.where(causal, jnp.exp(jnp.minimum(gc_col - gc_row, 0.0)), 0.0)
        gc = gc_col[:, 0:1]
        e_gc = jnp.exp(gc)
        kb = k_all * beta_all
        sc = _dot_nt(jnp.concatenate([kb, q_all], axis=0), k_all)
        yield
        L = sc[:tb] * jnp.where(strict, decay, 0.0)
        T = yield from _inv_unit_lower(L, tb, C)
        uw = _dot(T, jnp.concatenate([v_all * beta_all, kb * e_gc], axis=1))
        yield
        out["w"] = uw[:, LANE:].astype(BF16)
        out["q_dec"] = (q_all * e_gc).astype(BF16)
        out["k_dec"] = (k_all * jnp.exp(g_tot[:, 0:1] - gc)).astype(BF16)
        out["u"] = uw[:, :LANE]
        out["e_gl"] = jnp.exp(g_tot[:, :LANE])
        out["gzn"] = on_ref[...] * _silu(gz_ref[...])
        out["attn"] = (sc[tb:] * decay).astype(BF16)

    wslot = lax.rem(t, 2)
    rslot = 1 - wslot

    def store(out):
        pw_ref[wslot, 0], pw_ref[wslot, 1], pw_ref[wslot, 2] = out["w"], out["q_dec"], out["k_dec"]
        pu_ref[wslot, 0], pu_ref[wslot, 1], pu_ref[wslot, 2] = out["u"], out["e_gl"], out["gzn"]
        pattn_ref[wslot] = out["attn"]

    def load():
        return dict(w=pw_ref[rslot, 0], q_dec=pw_ref[rslot, 1], k_dec=pw_ref[rslot, 2],
                    u=pu_ref[rslot, 0], e_gl=pu_ref[rslot, 1], gzn=pu_ref[rslot, 2],
                    attn=pattn_ref[rslot])

    def phase_c(p):
        S = s_ref[...]
        v_news, o_qs = [], []
        for ci in range(tb // C):
            sl = slice(ci * C, (ci + 1) * C)
            ws = _dot(jnp.concatenate([p["w"][sl], p["q_dec"][sl]], axis=0), S)
            yield
            v_new = p["u"][sl] - ws[:C]
            v_news.append(v_new)
            o_qs.append(ws[C:])
            S = S * p["e_gl"][ci * C:ci * C + 1, :] + _dot_tn(p["k_dec"][sl], v_new)
            yield
        s_ref[...] = S
        o = jnp.concatenate(o_qs, axis=0) + _dot(p["attn"], jnp.concatenate(v_news, axis=0))
        yield
        o = o * lax.rsqrt(jnp.mean(o * o, axis=-1, keepdims=True) + RMS_EPS) * p["gzn"]
        o_ref[...] = o.astype(o_ref.dtype)

    return phase_ab, phase_c, load, store


def _gdn_branch(za, zb, conv_w, alog_pad, dtb_pad, onorm_g, *, batch, seq, heads, cb_q, cb_gz, cb_sm,
                a_lane0, b_lane0, tb):
    n = batch * seq
    nt = seq // tb
    nsub = SEQS_PER_STEP if heads % SEQS_PER_STEP == 0 else 1
    wide = nsub * LANE
    row_in = lambda b, t: b * nt + jnp.minimum(t, nt - 1)
    row_out = lambda b, t: b * nt + jnp.maximum(t - 1, 0)
    zspec = lambda cb: pl.BlockSpec((tb, wide), lambda b, h, t: (row_in(b, t), cb // nsub + h))
    cwspec = lambda cb: pl.BlockSpec((CONV_WIDTH, wide), lambda b, h, t: (0, cb // nsub + h))
    vec = pl.BlockSpec((1, LANE), lambda b, h, t: (0, 0))
    return dict(
        setup=functools.partial(_gdn_setup, tb=tb, nsub=nsub, a_lane0=a_lane0, b_lane0=b_lane0,
                                q_scale=float(LANE) ** -0.5),
        grid=(batch, heads // nsub, nt + 1),
        in_specs=[zspec(cb_q), zspec(cb_q + heads), zspec(cb_q + 2 * heads), zspec(cb_gz),
                  pl.BlockSpec((tb, LANE), lambda b, h, t: (row_in(b, t), cb_sm)),
                  cwspec(0), cwspec(heads), cwspec(2 * heads), vec, vec, vec],
        operands=(za, za, za, za, zb, conv_w, conv_w, conv_w, alog_pad, dtb_pad, onorm_g),
        out_spec=pl.BlockSpec((tb, wide), lambda b, h, t: (row_out(b, t), h)),
        out_shape=jax.ShapeDtypeStruct((n, heads * LANE), BF16),
        scratch=[pltpu.VMEM((nsub, LANE, LANE), F32), pltpu.VMEM((nsub, 3, tb + 8, LANE), F32),
                 pltpu.VMEM((nsub, 2, 3, tb, LANE), BF16), pltpu.VMEM((nsub, 2, 3, tb, LANE), F32),
                 pltpu.VMEM((nsub, 2, tb, tb), BF16)])


def _mixers_kernel(*refs, parts, nt):
    t = pl.program_id(2)
    n_in = [len(p["in_specs"]) for p in parts]
    n_sc = [len(p["scratch"]) for p in parts]
    ins, refs = refs[:sum(n_in)], refs[sum(n_in):]
    outs, scr = refs[:len(parts)], refs[len(parts):]
    heads = []
    for k, p in enumerate(parts):
        i0, s0 = sum(n_in[:k]), sum(n_sc[:k])
        heads += p["setup"](t, *ins[i0:i0 + n_in[k]], outs[k], *scr[s0:s0 + n_sc[k]])
    _run_pipelined(t, nt, heads)


def _mixers(parts, nt, name):
    assert all(p["grid"] == parts[0]["grid"] for p in parts)
    meta = [dict(setup=p["setup"], in_specs=p["in_specs"], scratch=p["scratch"]) for p in parts]
    return pl.pallas_call(
        functools.partial(_mixers_kernel, parts=meta, nt=nt),
        grid=parts[0]["grid"],
        in_specs=[s for p in parts for s in p["in_specs"]],
        out_specs=[p["out_spec"] for p in parts],
        out_shape=[p["out_shape"] for p in parts],
        scratch_shapes=[s for p in parts for s in p["scratch"]],
        compiler_params=_cparams(("parallel", "parallel", "arbitrary")),
        name=name,
    )(*[o for p in parts for o in p["operands"]])


_RW_ROWS = ("w0", "a0", "k_k", "k_a", "r_k", "ln_w", "ln_b", "mu_r", "mu_k", "mu_v")


def _rwkv_setup(t, zr_ref, zk_ref, zv_ref, sm0_ref, sm1_ref, sm2_ref, sm3_ref, pv_ref, musm_ref,
                wup_ref, aup_ref, gup_ref, o_ref, h_ref, c_ref, pwx_ref, pbs_ref, puo_ref, phk_ref,
                parb_ref, pvec_ref, *, nsub, **kw):
    @pl.when(t == 0)
    def _():
        h_ref[...] = jnp.zeros_like(h_ref)
        c_ref[...] = jnp.zeros_like(c_ref)

    lanes = lambda s: slice(s * LANE, (s + 1) * LANE)
    return [_rwkv_head_pair(t, zr_ref.at[:, lanes(s)], zk_ref.at[:, lanes(s)], zv_ref.at[:, lanes(s)],
                            sm0_ref, sm1_ref, sm2_ref, sm3_ref, pv_ref.at[:, lanes(s)], musm_ref,
                            wup_ref.at[:, lanes(s)], aup_ref.at[:, lanes(s)], gup_ref.at[:, lanes(s)],
                            o_ref.at[:, lanes(s)], h_ref.at[s], c_ref.at[s], pwx_ref.at[s], pbs_ref.at[s],
                            puo_ref.at[s], phk_ref.at[s], parb_ref.at[s], pvec_ref.at[s], **kw)
            for s in range(nsub)]


def _rwkv_head_pair(t, zr_ref, zk_ref, zv_ref, sm0_ref, sm1_ref, sm2_ref, sm3_ref, pv_ref, musm_ref,
                    wup_ref, aup_ref, gup_ref, o_ref, h_ref, c_ref, pwx_ref, pbs_ref, puo_ref, phk_ref,
                    parb_ref, pvec_ref, *, tb, hd):
    C = CHUNK
    G2, SG = 2 * C, 4 * C
    n_groups = tb // G2

    pv = pv_ref[...]
    prow = lambda name: pv[_RW_ROWS.index(name):_RW_ROWS.index(name) + 1, :]
    rr = lax.broadcasted_iota(jnp.int32, (LANE, LANE), 0)
    cc = lax.broadcasted_iota(jnp.int32, (LANE, LANE), 1)
    bd_ones = ((rr // hd) == (cc // hd)).astype(BF16)
    lane = lax.broadcasted_iota(jnp.int32, (1, LANE), 1)
    m0 = (lane < hd).astype(F32)
    m1 = 1.0 - m0

    def phase_a(a, out):
        musm = musm_ref[...]
        row0 = lax.broadcasted_iota(jnp.int32, (tb, LANE), 0) == 0

        def shift(idx, ref, mu):
            x = ref[...]
            prev = jnp.where(row0, c_ref[idx, 0:1, :], pltpu.roll(x, 1, 0))
            c_ref[idx, 0:1, :] = x[tb - 1:tb, :]
            return x + (prev - x) * mu

        r_all = shift(0, zr_ref, prow("mu_r"))
        k_all = shift(1, zk_ref, prow("mu_k"))
        v_all = shift(2, zv_ref, prow("mu_v"))
        yield
        wd = shift(3, sm0_ref, musm[0:1, 0:LANE])
        ad = shift(4, sm1_ref, musm[0:1, LANE:2 * LANE])
        gd0 = shift(5, sm2_ref, musm[0:1, 2 * LANE:3 * LANE])
        gd1 = shift(6, sm3_ref, musm[0:1, 3 * LANE:4 * LANE])
        w_lin = prow("w0") + _dot(jnp.tanh(wd), wup_ref[...])
        logw = -jnp.exp(-_softplus(-w_lin) - 0.5)
        yield
        alr = _sigmoid(prow("a0") + _dot(ad, aup_ref[...]))
        gate = _dot(_sigmoid(gd0), gup_ref[0:LANE, :]) + _dot(_sigmoid(gd1), gup_ref[LANE:2 * LANE, :])
        yield
        kraw = k_all * prow("k_k")
        kk = kraw * lax.rsqrt(_dot_sel(bd_ones, kraw * kraw, 1, left=False) + 1e-6)
        keff = k_all * (1.0 + (alr - 1.0) * prow("k_a"))
        bonus = _dot_sel(bd_ones, r_all * keff * prow("r_k"), 2, left=False) * v_all
        yield
        rt_i = lax.broadcasted_iota(jnp.int32, (tb, tb), 0)
        ct_i = lax.broadcasted_iota(jnp.int32, (tb, tb), 1)
        same_chunk = (rt_i // C) == (ct_i // C)
        sums = _dot_sel(jnp.concatenate([jnp.logical_and(same_chunk, rt_i >= ct_i),
                                         same_chunk], axis=0).astype(BF16), logw, 2)
        cum, ctot = sums[:tb], sums[tb:]
        yield
        e_neg = jnp.exp(-cum)
        e_end = jnp.exp(ctot - cum)
        b_all = kk * alr
        a.update(v=v_all, rt=r_all * jnp.exp(cum), at=-kk * jnp.exp(cum - logw),
                 bt=b_all * e_neg, kt=keff * e_neg, bh=b_all * e_end, kh=keff * e_end)
        out.update(e_tot=jnp.exp(ctot), bonus=bonus, gate=gate)
        yield [phase_b(a, out, g) for g in range(n_groups)]

    def phase_b(a, out, g):
        r4 = lax.broadcasted_iota(jnp.int32, (SG, SG), 0)
        c4 = lax.broadcasted_iota(jnp.int32, (SG, SG), 1)
        same_unit = (r4 // C) == (c4 // C)
        strict = jnp.logical_and(same_unit, r4 > c4)
        incl = jnp.logical_and(same_unit, r4 >= c4)

        def halves(x):
            return x[g * G2:g * G2 + C], x[g * G2 + C:(g + 1) * G2]

        def stack(x):
            x0, x1 = halves(x)
            return jnp.concatenate([x0 * m0, x0 * m1, x1 * m0, x1 * m1], axis=0)

        def dup(x):
            x0, x1 = halves(x)
            return jnp.concatenate([x0, x0, x1, x1], axis=0)

        xa, xr = stack(a["at"]), stack(a["rt"])
        v_st, b_st, k_st = stack(a["v"]), stack(a["bh"]), stack(a["kh"])
        sc = _dot_nt(jnp.concatenate([xa, xr], axis=0),
                     jnp.concatenate([dup(a["bt"]), dup(a["kt"])], axis=0))
        yield
        sv = _dot(jnp.concatenate([jnp.where(strict, sc[:SG, SG:], 0.0),
                                   jnp.where(incl, sc[SG:, SG:], 0.0)], axis=0), v_st)
        av, ork = sv[:SG], sv[SG:]
        T = yield from _inv_unit_lower(jnp.where(strict, -sc[:SG, :SG], 0.0), SG, hd)
        tw = _dot(T, jnp.concatenate([xa, av], axis=1))
        wt, ut = tw[:, :LANE], tw[:, LANE:]
        yield
        hk = [_dot_tn(v_st[i * G2:(i + 1) * G2], k_st[i * G2:(i + 1) * G2]) for i in range(2)]
        out[g] = dict(wt=wt.astype(BF16), xr=xr.astype(BF16), b_st=b_st.astype(BF16), ut=ut, ork=ork,
                      hk=hk, a_rb=jnp.where(incl, sc[SG:, :SG], 0.0).astype(BF16))

    ws = lax.rem(t, 2)
    rs = 1 - ws

    def store(out):
        for g in range(n_groups):
            o = out[g]
            pwx_ref[ws, g, 0], pwx_ref[ws, g, 1], pbs_ref[ws, g] = o["wt"], o["xr"], o["b_st"]
            puo_ref[ws, g, 0], puo_ref[ws, g, 1] = o["ut"], o["ork"]
            phk_ref[ws, 2 * g], phk_ref[ws, 2 * g + 1] = o["hk"]
            parb_ref[ws, g] = o["a_rb"]
        pvec_ref[ws, 0], pvec_ref[ws, 1], pvec_ref[ws, 2] = out["e_tot"], out["bonus"], out["gate"]

    def load():
        p = {g: dict(wt=pwx_ref[rs, g, 0], xr=pwx_ref[rs, g, 1], b_st=pbs_ref[rs, g],
                     ut=puo_ref[rs, g, 0], ork=puo_ref[rs, g, 1],
                     hk=[phk_ref[rs, 2 * g], phk_ref[rs, 2 * g + 1]], a_rb=parb_ref[rs, g])
             for g in range(n_groups)}
        p.update(e_tot=pvec_ref[rs, 0], bonus=pvec_ref[rs, 1], gate=pvec_ref[rs, 2])
        return p

    def phase_c(p):
        H = h_ref[...]
        us, xrs = [], []
        for ci in range(tb // C):
            gr = p[ci // 2]
            rows = slice((ci % 2) * G2, (ci % 2 + 1) * G2)
            xh = _dot_nt(jnp.concatenate([gr["wt"][rows], gr["xr"][rows]], axis=0), H)
            yield
            u_c = xh[:G2] + gr["ut"][rows]
            H = (H * p["e_tot"][ci * C:ci * C + 1, :] + _dot_tn(u_c, gr["b_st"][rows])
                 + gr["hk"][ci % 2])
            us.append(u_c)
            xrs.append(xh[G2:])
            yield
        h_ref[...] = H
        ys = []
        for g in range(n_groups):
            o_st = (jnp.concatenate(xrs[2 * g:2 * g + 2], axis=0)
                    + _dot(p[g]["a_rb"], jnp.concatenate(us[2 * g:2 * g + 2], axis=0)) + p[g]["ork"])
            ys += [o_st[0:C] + o_st[C:2 * C], o_st[2 * C:3 * C] + o_st[3 * C:4 * C]]
            yield
        y = jnp.concatenate(ys, axis=0)
        mu = _dot_sel(bd_ones, y, 2, left=False) * (1.0 / hd)
        yield
        yc = y - mu
        var = _dot_sel(bd_ones, yc * yc, 1, left=False) * (1.0 / hd)
        yield
        y = yc * lax.rsqrt(var + RWKV_GN_EPS) * prow("ln_w") + prow("ln_b") + p["bonus"]
        o_ref[...] = (y * p["gate"]).astype(o_ref.dtype)

    return (lambda out: phase_a({}, out)), phase_c, load, store


def _rwkv_branch(z, pv, musm, wup, aup, gup, *, batch, seq, width, hd, cb_r, cb_sm, tb):
    n = batch * seq
    nt = seq // tb
    hp = width // LANE
    nsub = SEQS_PER_STEP if hp % SEQS_PER_STEP == 0 else 1
    wide = nsub * LANE
    row_in = lambda b, t: b * nt + jnp.minimum(t, nt - 1)
    row_out = lambda b, t: b * nt + jnp.maximum(t - 1, 0)
    zspec = lambda cb: pl.BlockSpec((tb, wide), lambda b, p, t: (row_in(b, t), cb // nsub + p))
    smspec = lambda i: pl.BlockSpec((tb, LANE), lambda b, p, t: (row_in(b, t), cb_sm + i))
    colblk = lambda rows: pl.BlockSpec((rows, wide), lambda b, p, t: (0, p))
    ng, sg = tb // (2 * CHUNK), 4 * CHUNK
    return dict(
        setup=functools.partial(_rwkv_setup, tb=tb, nsub=nsub, hd=hd),
        grid=(batch, hp // nsub, nt + 1),
        in_specs=[zspec(cb_r), zspec(cb_r + hp), zspec(cb_r + 2 * hp),
                  smspec(0), smspec(1), smspec(2), smspec(3),
                  colblk(pv.shape[0]),
                  pl.BlockSpec(musm.shape, lambda b, p, t: (0, 0)),
                  colblk(LANE), colblk(LANE), colblk(2 * LANE)],
        operands=(z, z, z, z, z, z, z, pv, musm, wup, aup, gup),
        out_spec=pl.BlockSpec((tb, wide), lambda b, p, t: (row_out(b, t), p)),
        out_shape=jax.ShapeDtypeStruct((n, width), BF16),
        scratch=[pltpu.VMEM((nsub, LANE, LANE), F32), pltpu.VMEM((nsub, 7, 8, LANE), F32),
                 pltpu.VMEM((nsub, 2, ng, 2, sg, LANE), BF16),
                 pltpu.VMEM((nsub, 2, ng, sg, LANE), BF16),
                 pltpu.VMEM((nsub, 2, ng, 2, sg, LANE), F32),
                 pltpu.VMEM((nsub, 2, 2 * ng, LANE, LANE), F32),
                 pltpu.VMEM((nsub, 2, ng, sg, sg), BF16), pltpu.VMEM((nsub, 2, 3, tb, LANE), F32)])


class _Layout:
    def __init__(self, d_model, gdn_heads, rwkv_width, decay_lora, aaa_lora, gate_lora, tn):
        self.d_model, self.gh, self.rw = d_model, gdn_heads, rwkv_width
        self.dl, self.al, self.gl = decay_lora, aaa_lora, gate_lora
        self.gw = gdn_heads * LANE
        assert decay_lora + 2 * gdn_heads <= LANE and aaa_lora <= LANE and gate_lora == 2 * LANE
        self.a_lane0 = decay_lora
        self.b_lane0 = decay_lora + gdn_heads
        self.cb_q = 0
        self.cb_gz = 3 * self.gw // LANE
        self.cb_r = 0
        self.cb_sm = 3 * rwkv_width // LANE
        self.off_z = 3 * self.gw
        self.off_a = self.off_z + self.gw
        self.off_b = self.off_a + gdn_heads
        self.off_rwkv = self.off_b + gdn_heads
        self.off_gates = self.off_rwkv + 3 * rwkv_width + decay_lora + aaa_lora + gate_lora

    def split_cols(self, w, axis=-1):
        axis = axis % w.ndim
        cut = lambda a, b: lax.slice_in_dim(w, a, b, axis=axis)
        zeros = lambda n: jnp.zeros(w.shape[:axis] + (n,) + w.shape[axis + 1:], w.dtype)
        rw, dl, al, gl, gh = self.rw, self.dl, self.al, self.gl, self.gh
        o = self.off_rwkv + 3 * rw
        piece_b = jnp.concatenate(
            [cut(self.off_rwkv, o),
             cut(o, o + dl), cut(self.off_a, self.off_b), cut(self.off_b, self.off_rwkv),
             zeros(LANE - dl - 2 * gh),
             cut(o + dl, o + dl + al), zeros(LANE - al),
             cut(o + dl + al, o + dl + al + gl)], axis=axis)
        return cut(0, self.off_a), piece_b, cut(self.off_gates, w.shape[axis])


def _pad_rows(w, rows):
    return jnp.concatenate([w, jnp.zeros((rows - w.shape[0],) + w.shape[1:], w.dtype)], axis=0)


def _gdn_from_packed(za, zb, p, lay, batch, seq, tb=256):
    lane_vec = lambda v, lane0: jnp.zeros((1, LANE), F32).at[0, lane0:lane0 + v.shape[0]].set(v)
    return _gdn_branch(za, zb, p["conv_w"], lane_vec(p["gdn_a_log"], lay.a_lane0),
                       lane_vec(p["gdn_dt_bias"], lay.a_lane0), p["gdn_onorm_g"].reshape(1, LANE),
                       batch=batch, seq=seq, heads=lay.gh, cb_q=lay.cb_q, cb_gz=lay.cb_gz,
                       cb_sm=lay.cb_sm, a_lane0=lay.a_lane0, b_lane0=lay.b_lane0, tb=min(tb, seq))


def _rwkv_from_packed(z, p, lay, batch, seq, tb=256):
    rw = lay.rw
    hd = p["rwkv_r_k"].shape[-1]
    assert LANE % hd == 0 and hd * 2 == LANE
    mu = p["rwkv_mu"]
    rows = {"w0": p["rwkv_w0"], "a0": p["rwkv_a0"], "k_k": p["rwkv_k_k"], "k_a": p["rwkv_k_a"],
            "r_k": p["rwkv_r_k"].reshape(rw), "ln_w": p["rwkv_ln_w"], "ln_b": p["rwkv_ln_b"],
            "mu_r": mu[:rw], "mu_k": mu[rw:2 * rw], "mu_v": mu[2 * rw:3 * rw]}
    pv = _pad_rows(jnp.stack([rows[k] for k in _RW_ROWS]).astype(F32), 16)
    o = 3 * rw
    zl = lambda n: jnp.zeros((n,), F32)
    musm = jnp.concatenate([mu[o:o + lay.dl], zl(LANE - lay.dl), mu[o + lay.dl:o + lay.dl + lay.al],
                            zl(LANE - lay.al), mu[o + lay.dl + lay.al:]])
    musm = _pad_rows(musm.reshape(1, 4 * LANE), 8)
    wup = _pad_rows(p["rwkv_w_up"], LANE).astype(BF16)
    aup = _pad_rows(p["rwkv_a_up"], LANE).astype(BF16)
    gup = p["rwkv_g_up"].astype(BF16)
    return _rwkv_branch(z, pv, musm, wup, aup, gup, batch=batch, seq=seq, width=rw, hd=hd,
                        cb_r=lay.cb_r, cb_sm=lay.cb_sm, tb=min(tb, seq))


def _make_layout(p, tn):
    return _Layout(p["norm1_g"].shape[-1], p["gdn_a_log"].shape[-1], p["rwkv_w0"].shape[-1],
                   p["rwkv_w_up"].shape[0], p["rwkv_a_up"].shape[0], p["rwkv_g_up"].shape[0], tn)


def _ada_kernel(c_ref, w_ref, b_ref, o_ref):
    o_ref[...] = _dot(_silu(c_ref[...]), w_ref[...]) + b_ref[...]


def _ada_mod(c_pad, w_ada, b_ada, tn):
    d, cols = w_ada.shape
    return pl.pallas_call(
        _ada_kernel,
        grid=(cols // tn,),
        in_specs=[pl.BlockSpec((8, d), lambda j: (0, 0)),
                  pl.BlockSpec((d, tn), lambda j: (0, j)),
                  pl.BlockSpec((1, tn), lambda j: (0, j))],
        out_specs=pl.BlockSpec((8, tn), lambda j: (0, j)),
        out_shape=jax.ShapeDtypeStruct((8, cols), F32),
        compiler_params=_cparams(("arbitrary",)),
        name="ada_mod",
    )(c_pad, w_ada, b_ada.reshape(1, cols))


def _modnorm(x, g, sc, sh):
    y = x * lax.rsqrt(jnp.mean(x * x, axis=-1, keepdims=True) + RMS_EPS)
    return y * g * (1.0 + sc) + sh


def _norm1_kernel(x_ref, g_ref, sh_ref, sc_ref, o_ref):
    b = pl.program_id(0)
    h = _modnorm(x_ref[...], g_ref[...], sc_ref[pl.ds(b, 1), :], sh_ref[pl.ds(b, 1), :])
    o_ref[...] = h.astype(o_ref.dtype)


def _norm1(x2, g, mod, *, batch, seq, tm):
    n, d = x2.shape
    nt = seq // tm
    return pl.pallas_call(
        _norm1_kernel,
        grid=(batch, nt),
        in_specs=[pl.BlockSpec((tm, d), lambda b, t: (b * nt + t, 0)),
                  pl.BlockSpec((1, d), lambda b, t: (0, 0)),
                  pl.BlockSpec((8, d), lambda b, t: (0, 0)),
                  pl.BlockSpec((8, d), lambda b, t: (0, 1))],
        out_specs=pl.BlockSpec((tm, d), lambda b, t: (b * nt + t, 0)),
        out_shape=jax.ShapeDtypeStruct((n, d), BF16),
        compiler_params=_cparams(("parallel", "arbitrary")),
        name="norm1",
    )(x2, g, mod, mod)


def _merge_kernel(o_ref, y_ref, wa_ref, wb_ref, ga_ref, gb_ref, m_ref):
    ha = jnp.dot(o_ref[...], wa_ref[...], preferred_element_type=F32)
    hb = jnp.dot(y_ref[...], wb_ref[...], preferred_element_type=F32)
    m_ref[...] = (ga_ref[...].astype(F32) * ha + gb_ref[...].astype(F32) * hb).astype(m_ref.dtype)


def _merge(o, y, wa, wb, gates, *, tm, tn):
    n, d = o.shape[0], wa.shape[1]
    tn = math.gcd(tn, d)
    gb0 = 0
    return pl.pallas_call(
        _merge_kernel,
        grid=(n // tm, d // tn),
        in_specs=[pl.BlockSpec((tm, o.shape[1]), lambda i, j: (i, 0)),
                  pl.BlockSpec((tm, y.shape[1]), lambda i, j: (i, 0)),
                  pl.BlockSpec((wa.shape[0], tn), lambda i, j: (0, j)),
                  pl.BlockSpec((wb.shape[0], tn), lambda i, j: (0, j)),
                  pl.BlockSpec((tm, tn), lambda i, j: (i, gb0 + j)),
                  pl.BlockSpec((tm, tn), lambda i, j: (i, gb0 + d // tn + j))],
        out_specs=pl.BlockSpec((tm, tn), lambda i, j: (i, j)),
        out_shape=jax.ShapeDtypeStruct((n, d), BF16),
        compiler_params=_cparams(("parallel", "arbitrary")),
        name="merge",
    )(o, y, wa, wb, gates, gates)


def _outproj_kernel(m_ref, w_ref, x_ref, g_ref, o_ref, *, tiles_per_batch):
    b = pl.program_id(0) // tiles_per_batch
    y = jnp.dot(m_ref[...], w_ref[...], preferred_element_type=F32)
    o_ref[...] = x_ref[...] + g_ref[pl.ds(b, 1), :] * y


def _outproj(m, w, x2, mod, *, seq, tm, tn):
    n, d = x2.shape
    gcol = 2 * d // tn
    return pl.pallas_call(
        functools.partial(_outproj_kernel, tiles_per_batch=seq // tm),
        grid=(n // tm, d // tn),
        in_specs=[pl.BlockSpec((tm, d), lambda i, j: (i, 0)),
                  pl.BlockSpec((d, tn), lambda i, j: (0, j)),
                  pl.BlockSpec((tm, tn), lambda i, j: (i, j)),
                  pl.BlockSpec((8, tn), lambda i, j: (0, gcol + j))],
        out_specs=pl.BlockSpec((tm, tn), lambda i, j: (i, j)),
        out_shape=jax.ShapeDtypeStruct((n, d), F32),
        compiler_params=_cparams(("parallel", "arbitrary")),
        name="outproj",
    )(m, w, x2, mod)


def _router_kernel(x_ref, g_ref, sh_ref, sc_ref, wr_ref, br_ref, h_ref, id_ref, wt_ref, cnt_out_ref,
                   cnt_ref, *, n_groups, per_group):
    b = pl.program_id(0)
    h = _modnorm(x_ref[...], g_ref[...], sc_ref[pl.ds(b, 1), :], sh_ref[pl.ds(b, 1), :])
    h_ref[...] = _pack_bf16_pairs(h)
    logits = _dot(h, wr_ref[...]) + br_ref[...]
    tm = logits.shape[0]
    lane = lax.broadcasted_iota(jnp.int32, (tm, LANE), 1)
    neg = jnp.float32(-jnp.inf)
    big = jnp.int32(LANE)

    def first_argmax(vals, mask):
        vm = jnp.where(mask, vals, neg)
        mx = jnp.max(vm, axis=1, keepdims=True)
        idx = jnp.min(jnp.where(jnp.logical_and(mask, vm == mx), lane, big), axis=1, keepdims=True)
        return mx, idx

    gmask = lane < n_groups
    gmax, g_sel = first_argmax(logits, gmask)
    g_prob = 1.0 / jnp.sum(jnp.where(gmask, jnp.exp(logits - gmax), 0.0), axis=1, keepdims=True)
    e_lane = lane - n_groups
    emask = jnp.logical_and(e_lane >= g_sel * per_group, e_lane < (g_sel + 1) * per_group)
    emax = jnp.max(jnp.where(emask, logits, neg), axis=1, keepdims=True)
    ex = jnp.where(emask, jnp.exp(logits - emax), 0.0)
    probs = ex / jnp.sum(ex, axis=1, keepdims=True)
    p0, i0 = first_argmax(probs, emask)
    p1, i1 = first_argmax(probs, jnp.logical_and(emask, lane != i0))
    den = p0 + p1
    wt_ref[...] = jnp.where(lane == 0, g_prob * (p0 / den), jnp.where(lane == 1, g_prob * (p1 / den), 0.0))

    @pl.when(jnp.logical_and(b == 0, pl.program_id(1) == 0))
    def _():
        cnt_ref[...] = jnp.zeros_like(cnt_ref)

    e0, e1 = i0 - n_groups, i1 - n_groups
    hit0, hit1 = lane == e0, lane == e1
    hits = jnp.logical_or(hit0, hit1).astype(BF16)
    rr = lax.broadcasted_iota(jnp.int32, (tm, tm), 0)
    cc = lax.broadcasted_iota(jnp.int32, (tm, tm), 1)
    before = jnp.dot((rr > cc).astype(BF16), hits, preferred_element_type=F32) + cnt_ref[0:1, :]
    rank0 = jnp.sum(jnp.where(hit0, before, 0.0), axis=1, keepdims=True).astype(jnp.int32)
    rank1 = jnp.sum(jnp.where(hit1, before, 0.0), axis=1, keepdims=True).astype(jnp.int32)
    cnt_ref[0:1, :] = cnt_ref[0:1, :] + jnp.sum(hits.astype(F32), axis=0, keepdims=True)
    id_ref[...] = jnp.where(lane == 0, e0, jnp.where(lane == 1, e1,
                            jnp.where(lane == 2, rank0, jnp.where(lane == 3, rank1, 0))))
    cnt_out_ref[...] = cnt_ref[...].astype(jnp.int32)


def _router(x1, g, mod, wr, br, *, batch, seq, tm, n_groups, per_group):
    n, d = x1.shape
    nt = seq // tm
    rowblk = lambda w: pl.BlockSpec((tm, w), lambda b, t: (b * nt + t, 0))
    return pl.pallas_call(
        functools.partial(_router_kernel, n_groups=n_groups, per_group=per_group),
        grid=(batch, nt),
        in_specs=[rowblk(d),
                  pl.BlockSpec((1, d), lambda b, t: (0, 0)),
                  pl.BlockSpec((8, d), lambda b, t: (0, 3)),
                  pl.BlockSpec((8, d), lambda b, t: (0, 4)),
                  pl.BlockSpec((d, LANE), lambda b, t: (0, 0)),
                  pl.BlockSpec((1, LANE), lambda b, t: (0, 0))],
        out_specs=[rowblk(d // 2), rowblk(LANE), rowblk(LANE), pl.BlockSpec((8, LANE), lambda b, t: (0, 0))],
        out_shape=[jax.ShapeDtypeStruct((n, d // 2), jnp.uint32), jax.ShapeDtypeStruct((n, LANE), jnp.int32),
                   jax.ShapeDtypeStruct((n, LANE), F32), jax.ShapeDtypeStruct((8, LANE), jnp.int32)],
        scratch_shapes=[pltpu.VMEM((8, LANE), F32)],
        compiler_params=_cparams(("arbitrary", "arbitrary")),
        name="router",
    )(x1, g, mod, mod, wr, br)


def _expert_kernel(be_ref, tok_ref, nused_ref, h_hbm, w1_ref, w3_ref, w2_ref, bw_ref, o_ref, xbuf, sem):
    i = pl.program_id(0)
    nb = nused_ref[0]

    def row_copy(blk, slot, r):
        tok = tok_ref[blk * MOE_BLOCK + r]
        return pltpu.make_async_copy(h_hbm.at[pl.ds(tok, 1), :], xbuf.at[slot, pl.ds(r, 1), :],
                                     sem.at[slot])

    def start_block(blk, slot):
        def body(r, carry):
            row_copy(blk, slot, r).start()
            return carry
        lax.fori_loop(0, MOE_BLOCK, body, 0, unroll=DMA_UNROLL)

    for ahead in range(GATHER_DEPTH - 1):
        @pl.when(jnp.logical_and(i == 0, ahead < nb))
        def _():
            start_block(ahead, ahead)

    @pl.when(i + GATHER_DEPTH - 1 < nb)
    def _():
        start_block(i + GATHER_DEPTH - 1, (i + GATHER_DEPTH - 1) % GATHER_DEPTH)

    slot = i % GATHER_DEPTH

    @pl.when(i < nb)
    def _():
        def wait_body(r, carry):
            row_copy(i, slot, r).wait()
            return carry
        lax.fori_loop(0, MOE_BLOCK, wait_body, 0, unroll=DMA_UNROLL)

        x_lo, x_hi = _unpack_bf16_pairs(xbuf[slot])
        half = x_lo.shape[1]
        a = _dot(x_lo, w1_ref[0:half, :]) + _dot(x_hi, w1_ref[half:, :])
        g = _dot(x_lo, w3_ref[0:half, :]) + _dot(x_hi, w3_ref[half:, :])
        y = _dot(_silu(a) * g, w2_ref[...])
        o_ref[...] = _pack_bf16_pairs(y * bw_ref[...])

    @pl.when(i >= nb)
    def _():
        o_ref[...] = jnp.zeros_like(o_ref)


def _experts(h2, w1, w3, w2, block_expert, buf_tok, n_used, buf_w):
    half = h2.shape[1]
    d = 2 * half
    de = w1.shape[2]
    p_rows = buf_tok.shape[0]
    wspec = lambda shape: pl.BlockSpec((None,) + shape, lambda i, be, tok, nu: (be[i], 0, 0))
    grid_spec = pltpu.PrefetchScalarGridSpec(
        num_scalar_prefetch=3,
        grid=(p_rows // MOE_BLOCK,),
        in_specs=[pl.BlockSpec(memory_space=pl.ANY),
                  wspec((d, de)), wspec((d, de)), wspec((de, d)),
                  pl.BlockSpec((MOE_BLOCK, 1), lambda i, be, tok, nu: (i, 0))],
        out_specs=pl.BlockSpec((MOE_BLOCK, half), lambda i, be, tok, nu: (i, 0)),
        scratch_shapes=[pltpu.VMEM((GATHER_DEPTH, MOE_BLOCK, half), jnp.uint32),
                        pltpu.SemaphoreType.DMA((GATHER_DEPTH,))],
    )
    return pl.pallas_call(
        _expert_kernel,
        grid_spec=grid_spec,
        out_shape=jax.ShapeDtypeStruct((p_rows, half), jnp.uint32),
        compiler_params=_cparams(("arbitrary",)),
        name="experts",
    )(block_expert, buf_tok, n_used, h2, w1, w3, w2, buf_w)


def _combine_kernel(pos_ref, x_ref, g_ref, gf_ref, y_hbm, o_ref, ybuf, sem, *, tm, tiles_per_batch):
    i = pl.program_id(0)
    nb = pl.num_programs(0)
    rows = TOP_K * tm

    def row_copy(blk, slot, r):
        return pltpu.make_async_copy(y_hbm.at[pl.ds(pos_ref[blk * rows + r], 1), :],
                                     ybuf.at[slot, pl.ds(r, 1), :], sem.at[slot])

    def start_block(blk, slot):
        def body(r, carry):
            row_copy(blk, slot, r).start()
            return carry
        lax.fori_loop(0, rows, body, 0, unroll=DMA_UNROLL)

    @pl.when(i == 0)
    def _():
        start_block(0, 0)

    @pl.when(i + 1 < nb)
    def _():
        start_block(i + 1, (i + 1) % 2)

    slot = i % 2

    def wait_body(r, carry):
        row_copy(i, slot, r).wait()
        return carry
    lax.fori_loop(0, rows, wait_body, 0, unroll=DMA_UNROLL)

    b = i // tiles_per_batch
    lo0, hi0 = _unpack_bf16_pairs(ybuf[slot, 0:tm, :])
    lo1, hi1 = _unpack_bf16_pairs(ybuf[slot, tm:rows, :])
    moe = jnp.concatenate([lo0 + lo1, hi0 + hi1], axis=1)
    x = x_ref[...] + g_ref[pl.ds(b, 1), :] * moe
    o_ref[...] = x * lax.rsqrt(jnp.mean(x * x, axis=-1, keepdims=True) + RMS_EPS) * gf_ref[...]


def _combine(pos, x1, mod, gf, yb, *, seq, tm):
    n, d = x1.shape
    grid_spec = pltpu.PrefetchScalarGridSpec(
        num_scalar_prefetch=1,
        grid=(n // tm,),
        in_specs=[pl.BlockSpec((tm, d), lambda i, pos: (i, 0)),
                  pl.BlockSpec((8, d), lambda i, pos: (0, 5)),
                  pl.BlockSpec((1, d), lambda i, pos: (0, 0)),
                  pl.BlockSpec(memory_space=pl.ANY)],
        out_specs=pl.BlockSpec((tm, d), lambda i, pos: (i, 0)),
        scratch_shapes=[pltpu.VMEM((2, TOP_K * tm, d // 2), jnp.uint32), pltpu.SemaphoreType.DMA((2,))],
    )
    return pl.pallas_call(
        functools.partial(_combine_kernel, tm=tm, tiles_per_batch=seq // tm),
        grid_spec=grid_spec,
        out_shape=jax.ShapeDtypeStruct((n, d), F32),
        compiler_params=_cparams(("arbitrary",)),
        name="combine",
    )(pos, x1, mod, gf, yb)


def _dispatch(ids, rank, wts, counts):
    n = ids.shape[0]
    n_experts = counts.shape[0]
    a = n * TOP_K
    expert_id = ids.reshape(a)
    w_flat = wts.reshape(a)
    order = jnp.argsort(expert_id).astype(jnp.int32)
    ei = jnp.arange(n_experts, dtype=jnp.int32)
    upto = ei[None, :] <= ei[:, None]
    padded = (counts + MOE_BLOCK - 1) // MOE_BLOCK * MOE_BLOCK
    c_start = jnp.sum(jnp.where(upto, counts[None, :], 0), axis=1) - counts
    p_end = jnp.sum(jnp.where(upto, padded[None, :], 0), axis=1)
    p_start = p_end - padded
    lookup = lambda table, idx: jnp.sum(jnp.where(idx[:, None] == ei[None, :], table[None, :], 0), axis=1)
    pos = rank.reshape(a) + lookup(p_start, expert_id)
    p_rows = (a + n_experts * (MOE_BLOCK - 1) + MOE_BLOCK - 1) // MOE_BLOCK * MOE_BLOCK
    n_blocks = p_rows // MOE_BLOCK
    starts = jnp.arange(n_blocks, dtype=jnp.int32) * MOE_BLOCK
    block_expert = jnp.minimum(jnp.sum((p_end[None, :] <= starts[:, None]).astype(jnp.int32), axis=1),
                               n_experts - 1)
    slot = jnp.arange(p_rows, dtype=jnp.int32)
    per_slot = lambda table: jnp.repeat(lookup(table, block_expert), MOE_BLOCK)
    within = slot - per_slot(p_start)
    valid = within < per_slot(counts)
    src = order[jnp.clip(within + per_slot(c_start), 0, a - 1)]
    buf_tok = jnp.where(valid, src // TOP_K, 0)
    buf_w = jnp.where(valid, w_flat[src], 0.0)
    n_used = (p_end[n_experts - 1:] // MOE_BLOCK).astype(jnp.int32)
    return pos, buf_tok, buf_w, block_expert, n_used


def _pick(n, pref):
    t = min(n, pref)
    while n % t:
        t //= 2
    return t


def kernel(x, c, w_ada, b_ada, norm1_g, w_in, conv_w, gdn_a_log, gdn_dt_bias, gdn_onorm_g, rwkv_mu,
           rwkv_w0, rwkv_w_up, rwkv_a0, rwkv_a_up, rwkv_g_up, rwkv_k_k, rwkv_k_a, rwkv_r_k, rwkv_ln_w,
           rwkv_ln_b, w_gdn_o, w_rwkv_o, w_out, norm2_g, w_group, b_group, w_expert, b_expert, w1, w3,
           w2, norm_f_g):
    batch, seq, d = x.shape
    n = batch * seq
    depth = w_ada.shape[0]
    assert depth == 1, "the final norm is fused into the last layer's combine"
    assert batch <= 8 and seq % CHUNK == 0 and d % 512 == 0
    x2 = x.reshape(n, d)
    c_pad = _pad_rows(c, 8)
    tm_big = _pick(seq, 1024)
    tm_row = _pick(seq, 256)
    tn = 512
    for l in range(depth):
        p = {"norm1_g": norm1_g[l], "conv_w": conv_w[l], "gdn_a_log": gdn_a_log[l],
             "gdn_dt_bias": gdn_dt_bias[l], "gdn_onorm_g": gdn_onorm_g[l], "rwkv_mu": rwkv_mu[l],
             "rwkv_w0": rwkv_w0[l], "rwkv_w_up": rwkv_w_up[l], "rwkv_a0": rwkv_a0[l],
             "rwkv_a_up": rwkv_a_up[l], "rwkv_g_up": rwkv_g_up[l], "rwkv_k_k": rwkv_k_k[l],
             "rwkv_k_a": rwkv_k_a[l], "rwkv_r_k": rwkv_r_k[l], "rwkv_ln_w": rwkv_ln_w[l],
             "rwkv_ln_b": rwkv_ln_b[l]}
        lay = _make_layout(p, tn)
        mod = _ada_mod(c_pad, w_ada[l], b_ada[l], tn)

        h1 = _norm1(x2, norm1_g[l].reshape(1, d), mod, batch=batch, seq=seq, tm=tm_row)
        w_a, w_b, w_c = (w.astype(BF16) for w in lay.split_cols(w_in[l].T, axis=0))
        n_exp, _, d_exp = w1.shape[1:]
        flat = lambda w: w.reshape(-1, w.shape[-1])
        za, w1b, w2b = _matmul(h1, w_a, tm_big, tn, F32, name="in_proj_gdn",
                               cast=[flat(w1[l]), flat(w2[l])], w_rows_are_outputs=True)
        zb = _matmul(h1, w_b, tm_big, tn, F32, name="in_proj_rwkv", w_rows_are_outputs=True)
        gates, w3b = _matmul(h1, w_c, tm_big, tn, BF16, act=_sigmoid, name="in_proj_gates",
                             cast=[flat(w3[l])], w_rows_are_outputs=True)
        w1b, w3b = w1b.reshape(n_exp, d, d_exp), w3b.reshape(n_exp, d, d_exp)
        w2b = w2b.reshape(n_exp, d_exp, d)
        tb = min(256, seq)
        o_gdn, y_rwkv = _mixers([_gdn_from_packed(za, zb, p, lay, batch, seq, tb),
                                 _rwkv_from_packed(zb, p, lay, batch, seq, tb)], seq // tb, "mixers")
        m = _merge(o_gdn, y_rwkv, w_gdn_o[l].astype(BF16), w_rwkv_o[l].astype(BF16), gates,
                   tm=tm_big, tn=tn)
        x1 = _outproj(m, w_out[l].astype(BF16), x2, mod, seq=seq, tm=tm_big, tn=tn)

        n_groups, n_experts = w_group.shape[-1], w_expert.shape[-1]
        assert n_groups + n_experts <= LANE
        wr = jnp.concatenate([w_group[l], w_expert[l],
                              jnp.zeros((d, LANE - n_groups - n_experts), F32)], axis=1).astype(BF16)
        br = jnp.concatenate([b_group[l], b_expert[l],
                              jnp.zeros((LANE - n_groups - n_experts,), F32)]).reshape(1, LANE)
        h2, ids, wts, cnt = _router(x1, norm2_g[l].reshape(1, d), mod, wr, br, batch=batch, seq=seq,
                                    tm=tm_row, n_groups=n_groups, per_group=n_experts // n_groups)
        pos, buf_tok, buf_w, block_expert, n_used = _dispatch(
            ids[:, :TOP_K], ids[:, TOP_K:2 * TOP_K], wts[:, :TOP_K], cnt[0, :n_experts])
        yb = _experts(h2, w1b, w3b, w2b, block_expert, buf_tok, n_used, buf_w.reshape(-1, 1))
        tm_c = _pick(seq, 128)
        pos_tiles = pos.reshape(n // tm_c, tm_c, TOP_K).transpose(0, 2, 1).reshape(n * TOP_K)
        x2 = _combine(pos_tiles, x1, mod, norm_f_g.reshape(1, d), yb, seq=seq, tm=tm_c)
    return x2.reshape(batch, seq, d)
```
